```python
import math
import jax, jax.numpy as jnp
from jax import lax
import numpy as np

D_MODEL = 1024
BATCH = 8
SEQ = 8192
DEPTH = 4
DEC_BATCH = 4
DEC_SEQ = 4096
PAST_LEN = 128

N_EVEN = (DEPTH + 1) // 2
N_ODD = DEPTH // 2
D_A = D_MODEL // 2
D_B = D_MODEL // 2
A_HEADS = 4
A_DV = D_A // A_HEADS
A_DK = A_DV // 2
B_ORDER = 2
D_IN_AB = 3 * D_A + (B_ORDER + 1) * D_B
Q_BLOCK = 128
REL_BUCKETS = 32
REL_MAX_DIST = 128
FILTER_EMB = 33
FILTER_BANDS = (FILTER_EMB - 1) // 2
FILTER_WIDTH = 64
FILTER_CH = 2 * B_ORDER * D_B
HY_TARGET = 1e-2
HY_FAST = 0.3
HY_SLOW = 1.5
C_GROUPS = 4
C_GROUP_W = D_MODEL // C_GROUPS
N_EXPERTS = 16
EC_FACTOR = 2
D_FF_E = 2816
LN_EPS = 1e-5
DN_ALPHA = (2 * DEPTH) ** 0.25
DN_BETA = (8 * DEPTH) ** -0.25

kernel_name = "hybrid_diffattn_hyena_fnet_ec_encoder"


def layer_norm(x, g, b):
    xf = x.astype(jnp.float32)
    mu = jnp.mean(xf, -1, keepdims=True)
    var = jnp.mean(jnp.square(xf - mu), -1, keepdims=True)
    return ((xf - mu) * lax.rsqrt(var + LN_EPS) * g.astype(jnp.float32) + b.astype(jnp.float32)).astype(x.dtype)


def rel_bucket(rel):
    nb = REL_BUCKETS // 2
    max_exact = nb // 2
    n = jnp.abs(rel)
    large = max_exact + (jnp.log(jnp.maximum(n, 1).astype(jnp.float32) / max_exact)
                         / math.log(REL_MAX_DIST / max_exact) * (nb - max_exact)).astype(jnp.int32)
    large = jnp.minimum(large, nb - 1)
    return jnp.where(rel > 0, nb, 0) + jnp.where(n < max_exact, n, large)


def diff_attention(q, k, v, lam, sub_g, rel_table, lambda_init):
    bsz, L = q.shape[0], q.shape[1]
    n_blk = L // Q_BLOCK
    lf = lam.astype(jnp.float32)
    lam_full = jnp.exp(jnp.sum(lf[0] * lf[1])) - jnp.exp(jnp.sum(lf[2] * lf[3])) + lambda_init
    scale = A_DK ** -0.5
    kpos = jnp.arange(L)
    qb = q.reshape(bsz, n_blk, Q_BLOCK, A_HEADS, 2, A_DK).transpose(1, 0, 2, 3, 4, 5)
    tab = rel_table.astype(jnp.float32)

    def block(args):
        i, qi = args
        s = jnp.einsum('bqhmd,bkhmd->bhmqk', qi, k, preferred_element_type=jnp.float32) * scale
        qpos = i * Q_BLOCK + jnp.arange(Q_BLOCK)
        bias = tab[rel_bucket(kpos[None, :] - qpos[:, None])]
        s = s + bias.transpose(2, 0, 1)[None, :, None]
        p = jax.nn.softmax(s, axis=-1)
        w = p[:, :, 0] - lam_full * p[:, :, 1]
        return jnp.einsum('bhqk,bkhd->bqhd', w.astype(v.dtype), v)

    o = lax.map(block, (jnp.arange(n_blk), qb))
    o = o.transpose(1, 0, 2, 3, 4).reshape(bsz, L, A_HEADS, A_DV).astype(jnp.float32)
    o = o * lax.rsqrt(jnp.mean(o * o, -1, keepdims=True) + LN_EPS) * sub_g.astype(jnp.float32) * (1.0 - lambda_init)
    return o.reshape(bsz, L, D_A).astype(q.dtype)


def short_conv(u, w, b):
    L = u.shape[1]
    up = jnp.pad(u, ((0, 0), (1, 1), (0, 0)))
    return up[:, :L] * w[0] + up[:, 1:L + 1] * w[1] + up[:, 2:] * w[2] + b


def hyena_filters(L, w1, b1, w2, b2, w3, freq, decay):
    f32 = jnp.float32
    t = jnp.linspace(0.0, 1.0, L, dtype=f32)[:, None]
    wpos = 2.0 * math.pi * jnp.arange(L, dtype=f32)[:, None] / L
    fr = jnp.linspace(1e-4, FILTER_BANDS - 1, FILTER_BANDS, dtype=f32)[None, :]
    z = jnp.concatenate([t, jnp.cos(fr * wpos), -jnp.sin(fr * wpos)], -1)
    fq = freq.astype(f32)
    h = jnp.sin(fq[0] * (z @ w1.astype(f32) + b1.astype(f32)))
    h = jnp.sin(fq[1] * (h @ w2.astype(f32) + b2.astype(f32)))
    h = (h @ w3.astype(f32)) * jnp.exp(-t * jnp.abs(decay.astype(f32)))
    h = h.reshape(L, 2, B_ORDER, D_B)
    return h / jnp.sum(jnp.abs(h), axis=(0, 1), keepdims=True)


def hyena_mixer(u, conv_w, conv_b, w1, b1, w2, b2, w3, freq, decay, skip):
    L = u.shape[1]
    uc = short_conv(u, conv_w, conv_b).astype(jnp.float32)
    v, x1, x2 = jnp.split(uc, 3, axis=-1)
    h = hyena_filters(L, w1, b1, w2, b2, w3, freq, decay)
    k2 = jnp.concatenate([h[:, 0], jnp.zeros((1, B_ORDER, D_B), jnp.float32), h[:0:-1, 1]], axis=0)
    kf = jnp.fft.rfft(k2, axis=0)
    sk = skip.astype(jnp.float32)
    z = v
    for n, gate in enumerate((x1, x2)):
        zf = jnp.fft.rfft(z, n=2 * L, axis=1)
        z = gate * (jnp.fft.irfft(zf * kf[None, :, n], n=2 * L, axis=1)[:, :L] + z * sk[n])
    return z.astype(u.dtype)


def fourier_mix(x, w_out):
    bsz, L, d = x.shape
    xg = x.astype(jnp.float32).reshape(bsz, L, C_GROUPS, C_GROUP_W)
    f = jnp.fft.fftn(xg, axes=(1, 3), norm='ortho').real
    return f.reshape(bsz, L, d).astype(x.dtype) @ w_out


def expert_choice(x, w_router, w_gate, w_up, w_down):
    bsz, L, d = x.shape
    T = bsz * L
    cap = max(1, EC_FACTOR * T // N_EXPERTS)
    xt = x.reshape(T, d)
    aff = jax.nn.softmax((xt @ w_router).astype(jnp.float32), axis=-1)
    gate, idx = lax.top_k(aff.T, cap)

    def expert(args):
        ids, g, wg, wu, wd = args
        xe = xt[ids]
        he = jax.nn.silu(xe @ wg) * (xe @ wu)
        return (he @ wd) * g[:, None].astype(x.dtype)

    out = lax.map(expert, (idx, gate, w_gate, w_up, w_down))
    y = jnp.zeros_like(xt).at[idx.reshape(-1)].add(out.reshape(-1, d))
    return y.reshape(bsz, L, d)


def trunk(x, rel_bias, ab_w_in, ab_w_out, diff_lambda, diff_subln_g, hy_conv_w, hy_conv_b,
          hy_f_w1, hy_f_b1, hy_f_w2, hy_f_b2, hy_f_w3, hy_f_freq, hy_decay, hy_skip,
          c_w_out, ec_router, ec_w_gate, ec_w_up, ec_w_down, ln_g, ln_b):
    bsz, L, _ = x.shape
    for layer in range(DEPTH):
        j = layer // 2
        if layer % 2 == 0:
            hproj = x @ ab_w_in[j]
            q = hproj[..., :D_A].reshape(bsz, L, A_HEADS, 2, A_DK)
            k = hproj[..., D_A:2 * D_A].reshape(bsz, L, A_HEADS, 2, A_DK)
            v = hproj[..., 2 * D_A:3 * D_A].reshape(bsz, L, A_HEADS, A_DV)
            lambda_init = 0.8 - 0.6 * math.exp(-0.3 * layer)
            a_out = diff_attention(q, k, v, diff_lambda[j], diff_subln_g[j], rel_bias, lambda_init)
            b_out = hyena_mixer(hproj[..., 3 * D_A:], hy_conv_w[j], hy_conv_b[j], hy_f_w1[j], hy_f_b1[j],
                                hy_f_w2[j], hy_f_b2[j], hy_f_w3[j], hy_f_freq[j], hy_decay[j], hy_skip[j])
            mix = jnp.concatenate([a_out, b_out], axis=-1) @ ab_w_out[j]
        else:
            mix = fourier_mix(x, c_w_out[j])
        x = layer_norm(DN_ALPHA * x + mix, ln_g[layer, 0], ln_b[layer, 0])
        ffn = expert_choice(x, ec_router[layer], ec_w_gate[layer], ec_w_up[layer], ec_w_down[layer])
        x = layer_norm(DN_ALPHA * x + ffn, ln_g[layer, 1], ln_b[layer, 1])
    return x


def setup_inputs(seed: int = 0) -> dict:
    key = jax.random.key(seed)
    ks = jax.random.split(key, 26)
    nrm = lambda k, s, sc: jax.random.normal(k, s, jnp.float32) * sc
    min_decay = math.log(HY_TARGET) / HY_SLOW
    max_decay = math.log(HY_TARGET) / HY_FAST
    decay_base = jnp.linspace(min_decay, max_decay, FILTER_CH, dtype=jnp.float32)[None, :]
    return {
        "x_prompt": nrm(ks[0], (BATCH, SEQ, D_MODEL), 1.0),
        "x_sample": nrm(ks[1], (DEC_BATCH, DEC_SEQ, D_MODEL), 1.0),
        "rel_bias": nrm(ks[2], (REL_BUCKETS, A_HEADS), 0.2),
        "ab_w_in": nrm(ks[3], (N_EVEN, D_MODEL, D_IN_AB), D_MODEL ** -0.5),
        "ab_w_out": nrm(ks[4], (N_EVEN, D_A + D_B, D_MODEL), (D_A + D_B) ** -0.5 * DN_BETA),
        "diff_lambda": nrm(ks[5], (N_EVEN, 4, A_DK), 0.1),
        "diff_subln_g": 1.0 + nrm(ks[6], (N_EVEN, A_DV), 0.01),
        "hy_conv_w": nrm(ks[7], (N_EVEN, 3, (B_ORDER + 1) * D_B), 0.5),
        "hy_conv_b": nrm(ks[8], (N_EVEN, (B_ORDER + 1) * D_B), 0.01),
        "hy_f_w1": nrm(ks[9], (N_EVEN, FILTER_EMB, FILTER_WIDTH), FILTER_EMB ** -0.5),
        "hy_f_b1": nrm(ks[10], (N_EVEN, FILTER_WIDTH), 0.1),
        "hy_f_w2": nrm(ks[11], (N_EVEN, FILTER_WIDTH, FILTER_WIDTH), FILTER_WIDTH ** -0.5),
        "hy_f_b2": nrm(ks[12], (N_EVEN, FILTER_WIDTH), 0.1),
        "hy_f_w3": nrm(ks[13], (N_EVEN, FILTER_WIDTH, FILTER_CH), FILTER_WIDTH ** -0.5),
        "hy_f_freq": 1.0 + nrm(ks[14], (N_EVEN, 2, FILTER_WIDTH), 0.01),
        "hy_decay": decay_base + nrm(ks[15], (N_EVEN, FILTER_CH), 0.01),
        "hy_skip": nrm(ks[16], (N_EVEN, B_ORDER, D_B), 0.1),
        "c_w_out": nrm(ks[17], (N_ODD, D_MODEL, D_MODEL), D_MODEL ** -0.5 * DN_BETA),
        "ec_router": nrm(ks[18], (DEPTH, D_MODEL, N_EXPERTS), D_MODEL ** -0.5),
        "ec_w_gate": nrm(ks[19], (DEPTH, N_EXPERTS, D_MODEL, D_FF_E), D_MODEL ** -0.5),
        "ec_w_up": nrm(ks[20], (DEPTH, N_EXPERTS, D_MODEL, D_FF_E), D_MODEL ** -0.5),
        "ec_w_down": nrm(ks[21], (DEPTH, N_EXPERTS, D_FF_E, D_MODEL), D_FF_E ** -0.5 * DN_BETA),
        "ln_g": 1.0 + nrm(ks[22], (DEPTH, 2, D_MODEL), 0.01),
        "ln_b": nrm(ks[23], (DEPTH, 2, D_MODEL), 0.01),
    }


def reference(x_prompt, x_sample, rel_bias, ab_w_in, ab_w_out, diff_lambda, diff_subln_g, hy_conv_w, hy_conv_b,
              hy_f_w1, hy_f_b1, hy_f_w2, hy_f_b2, hy_f_w3, hy_f_freq, hy_decay, hy_skip,
              c_w_out, ec_router, ec_w_gate, ec_w_up, ec_w_down, ln_g, ln_b):
    y_prompt = trunk(x_prompt, rel_bias, ab_w_in, ab_w_out, diff_lambda, diff_subln_g, hy_conv_w, hy_conv_b,
                     hy_f_w1, hy_f_b1, hy_f_w2, hy_f_b2, hy_f_w3, hy_f_freq, hy_decay, hy_skip,
                     c_w_out, ec_router, ec_w_gate, ec_w_up, ec_w_down, ln_g, ln_b)
    y_sample = trunk(x_sample, rel_bias, ab_w_in, ab_w_out, diff_lambda, diff_subln_g, hy_conv_w, hy_conv_b,
                     hy_f_w1, hy_f_b1, hy_f_w2, hy_f_b2, hy_f_w3, hy_f_freq, hy_decay, hy_skip,
                     c_w_out, ec_router, ec_w_gate, ec_w_up, ec_w_down, ln_g, ln_b)
    return (y_prompt, y_sample)
```

```python
import functools
import math

import jax
import jax.numpy as jnp
from jax import lax
from jax.experimental import pallas as pl
from jax.experimental.pallas import tpu as pltpu

BF = jnp.bfloat16
F32 = jnp.float32
I32 = jnp.int32

A_HEADS = 4
A_DV = 128
A_DK = 64
D_A = 512
D_B = 512
B_ORDER = 2
REL_BUCKETS = 32
REL_MAX_DIST = 128
FILTER_BANDS = 16
N_EXPERTS = 16
EC_FACTOR = 2
LN_EPS = 1e-5
C_GROUPS = 4
LANES = 128
FFT_N2 = 128
VMEM_LIMIT = 52 * 1024 * 1024


def _cp(sem, vmem=VMEM_LIMIT):
    return pltpu.CompilerParams(dimension_semantics=sem, vmem_limit_bytes=vmem)


def _dot(a, b):
    return jnp.dot(a, b, preferred_element_type=F32)


def _dot_nt(a, b):
    return lax.dot_general(a, b, (((1,), (1,)), ((), ())), preferred_element_type=F32)


def _mm_kernel(a_ref, b_ref, o_ref, *scratch, nk, kaxis, scale):
    prod = _dot(a_ref[...].astype(BF), b_ref[...].astype(BF))
    if nk == 1:
        o_ref[...] = (prod * scale).astype(o_ref.dtype)
        return
    acc = scratch[0]
    k = pl.program_id(kaxis)

    @pl.when(k == 0)
    def _():
        acc[...] = jnp.zeros_like(acc)

    acc[...] += prod

    @pl.when(k == nk - 1)
    def _():
        o_ref[...] = (acc[...] * scale).astype(o_ref.dtype)


def _mm(a, b, out_shape, grid, a_spec, b_spec, o_spec, mn, *, nk=1, kaxis=None, scale=1.0, name):
    sem = tuple("arbitrary" if i == kaxis else "parallel" for i in range(len(grid)))
    return pl.pallas_call(
        functools.partial(_mm_kernel, nk=nk, kaxis=kaxis, scale=scale),
        out_shape=out_shape, grid=grid, in_specs=[a_spec, b_spec], out_specs=o_spec,
        scratch_shapes=[pltpu.VMEM(mn, F32)] if nk > 1 else [],
        compiler_params=_cp(sem), name=name)(a, b)


def _xw(x, w, tm=512, name="xw"):
    M, K = x.shape
    N = w.shape[1]
    tm = min(tm, M)
    return _mm(x, w, jax.ShapeDtypeStruct((M, N), F32), (M // tm,),
               pl.BlockSpec((tm, K), lambda i: (i, 0)), pl.BlockSpec((K, N), lambda i: (0, 0)),
               pl.BlockSpec((tm, N), lambda i: (i, 0)), (tm, N), name=name)


def _ln_rows(y, g, b):
    mu = jnp.mean(y, -1, keepdims=True)
    d = y - mu
    var = jnp.mean(d * d, -1, keepdims=True)
    return d * lax.rsqrt(var + LN_EPS) * g + b


def _mm_res_ln_kernel(*refs, npairs, alpha):
    x_ref, g_ref, b_ref = refs[2 * npairs:2 * npairs + 3]
    o_ref = refs[2 * npairs + 3]
    y = alpha * x_ref[...]
    for i in range(npairs):
        y = y + _dot(refs[2 * i][...].astype(BF), refs[2 * i + 1][...])
    o_ref[...] = _ln_rows(y, g_ref[...], b_ref[...])


def _mm_res_ln(pairs, x, g, b, alpha, tm=512, name="mm_res_ln"):
    M, D = x.shape
    tm = min(tm, M)
    specs, args = [], []
    for a, w in pairs:
        specs += [pl.BlockSpec((tm, a.shape[1]), lambda i: (i, 0)), pl.BlockSpec(w.shape, lambda i: (0, 0))]
        args += [a, w]
    specs += [pl.BlockSpec((tm, D), lambda i: (i, 0)), pl.BlockSpec((1, D), lambda i: (0, 0)),
              pl.BlockSpec((1, D), lambda i: (0, 0))]
    args += [x, g.reshape(1, D), b.reshape(1, D)]
    return pl.pallas_call(
        functools.partial(_mm_res_ln_kernel, npairs=len(pairs), alpha=alpha),
        out_shape=jax.ShapeDtypeStruct((M, D), F32), grid=(M // tm,), in_specs=specs,
        out_specs=pl.BlockSpec((tm, D), lambda i: (i, 0)), compiler_params=_cp(("parallel",)), name=name)(*args)


def _rel_bucket(rel):
    nb = REL_BUCKETS // 2
    max_exact = nb // 2
    n = jnp.abs(rel)
    large = max_exact + (jnp.log(jnp.maximum(n, 1).astype(F32) / max_exact)
                         / math.log(REL_MAX_DIST / max_exact) * (nb - max_exact)).astype(I32)
    large = jnp.minimum(large, nb - 1)
    return jnp.where(rel > 0, nb, 0) + jnp.where(n < max_exact, n, large)


def _bias_tiles(rel_table, T):
    assert T >= REL_MAX_DIST
    i = jnp.arange(T)
    rel = (jnp.arange(-2, 3)[:, None, None] * T + i[None, None, :]) - i[None, :, None]
    return rel_table.astype(F32)[_rel_bucket(rel)].transpose(3, 0, 1, 2)


def _attn_kernel(lam_ref, q_ref, k_ref, v_ref, bias_ref, g_ref, o_ref, *, T, nk, out_scale):
    qi = pl.program_id(2)
    tq = q_ref.shape[0]
    q = q_ref[...] * (A_DK ** -0.5)
    lane = lax.broadcasted_iota(I32, q.shape, 1)
    qs = (jnp.where(lane < A_DK, q, 0.0).astype(BF), jnp.where(lane >= A_DK, q, 0.0).astype(BF))

    def body(kj, carry):
        start = pl.multiple_of(kj * T, T)
        kc = k_ref[pl.ds(start, T), :].astype(BF)
        vc = v_ref[pl.ds(start, T), :].astype(BF)
        bias = bias_ref[jnp.clip(kj - qi, -2, 2) + 2]
        new = []
        for mi in range(2):
            m, l, a = carry[3 * mi:3 * mi + 3]
            s = _dot_nt(qs[mi], kc) + bias
            m_new = jnp.maximum(m, jnp.max(s, axis=1, keepdims=True))
            alpha = jnp.exp(m - m_new)
            p = jnp.exp(s - m_new)
            l = alpha * l + jnp.sum(p, axis=1, keepdims=True)
            a = alpha * a + _dot(p.astype(BF), vc)
            new += [m_new, l, a]
        return tuple(new)

    init = (jnp.full((tq, 1), -jnp.inf, F32), jnp.zeros((tq, 1), F32), jnp.zeros((tq, A_DV), F32)) * 2
    m0, l0, a0, m1, l1, a1 = lax.fori_loop(0, nk, body, init)
    o = a0 / l0 - lam_ref[0] * (a1 / l1)
    o = o * lax.rsqrt(jnp.mean(o * o, -1, keepdims=True) + LN_EPS) * g_ref[...] * out_scale
    o_ref[...] = o


def _diff_attention(hproj, lam_full, sub_g, rel_table, lambda_init, T):
    B, L, _ = hproj.shape
    T = min(T, L)
    bias = _bias_tiles(rel_table, T)
    H = A_HEADS
    return pl.pallas_call(
        functools.partial(_attn_kernel, T=T, nk=L // T, out_scale=1.0 - lambda_init),
        out_shape=jax.ShapeDtypeStruct((B, L, D_A), F32), grid=(B, H, L // T),
        in_specs=[pl.BlockSpec(memory_space=pltpu.SMEM),
                  pl.BlockSpec((None, T, LANES), lambda b, h, i: (b, i, h)),
                  pl.BlockSpec((None, L, LANES), lambda b, h, i: (b, 0, H + h)),
                  pl.BlockSpec((None, L, LANES), lambda b, h, i: (b, 0, 2 * H + h)),
                  pl.BlockSpec((None, 5, T, T), lambda b, h, i: (h, 0, 0, 0)),
                  pl.BlockSpec((1, A_DV), lambda b, h, i: (0, 0))],
        out_specs=pl.BlockSpec((None, T, LANES), lambda b, h, i: (b, i, h)),
        compiler_params=_cp(("parallel", "parallel", "arbitrary")), name="diff_attn",
    )(lam_full.reshape(1), hproj, hproj, hproj, bias, sub_g.astype(F32).reshape(1, A_DV))


def _short_conv_kernel(u_ref, w_ref, b_ref, o_ref):
    u = u_ref[...]
    L = u.shape[0]
    row = lax.broadcasted_iota(I32, u.shape, 0)
    up = jnp.where(row == 0, 0.0, pltpu.roll(u, 1, 0))
    dn = jnp.where(row == L - 1, 0.0, pltpu.roll(u, L - 1, 0))
    w = w_ref[...]
    o_ref[...] = up * w[0:1] + u * w[1:2] + dn * w[2:3] + b_ref[...]


def _short_conv(hproj, conv_w, conv_b):
    B, L, _ = hproj.shape
    nblk = 3 * D_B // LANES
    off = 3 * D_A // LANES
    per = D_B // LANES
    return pl.pallas_call(
        _short_conv_kernel, out_shape=jax.ShapeDtypeStruct((3, B, L, D_B), F32), grid=(B, nblk),
        in_specs=[pl.BlockSpec((None, L, LANES), lambda b, j: (b, 0, off + j)),
                  pl.BlockSpec((3, LANES), lambda b, j: (0, j)),
                  pl.BlockSpec((1, LANES), lambda b, j: (0, j))],
        out_specs=pl.BlockSpec((None, None, L, LANES), lambda b, j: (j // per, b, 0, j % per)),
        compiler_params=_cp(("parallel", "parallel")), name="short_conv",
    )(hproj, conv_w.astype(F32), conv_b.astype(F32).reshape(1, -1))


def _filter_mlp_kernel(z_ref, w1_ref, b1_ref, w2_ref, b2_ref, w3_ref, fq_ref, dec_ref, h_ref, s_ref):
    i = pl.program_id(0)
    z = z_ref[...]
    fq = fq_ref[...]
    h = jnp.sin(fq[0:1] * (_dot(z.astype(BF), w1_ref[...]) + b1_ref[...]))
    h = jnp.sin(fq[1:2] * (_dot(h.astype(BF), w2_ref[...]) + b2_ref[...]))
    t = z[:, 0:1]
    h = _dot(h.astype(BF), w3_ref[...]) * jnp.exp(-t * jnp.abs(dec_ref[...]))

    @pl.when(i == 0)
    def _():
        s_ref[...] = jnp.zeros_like(s_ref)

    s_ref[...] += jnp.sum(jnp.abs(h), axis=0, keepdims=True)
    row = lax.broadcasted_iota(I32, h.shape, 0)
    col = lax.broadcasted_iota(I32, h.shape, 1)
    half = h.shape[1] // 2
    h_ref[...] = jnp.where((row + i * h.shape[0] == 0) & (col >= half), 0.0, h)


def _filter_mlp(L, w1, b1, w2, b2, w3, freq, decay, tl=512):
    t = jnp.linspace(0.0, 1.0, L, dtype=F32)[:, None]
    wpos = 2.0 * math.pi * jnp.arange(L, dtype=F32)[:, None] / L
    fr = jnp.linspace(1e-4, FILTER_BANDS - 1, FILTER_BANDS, dtype=F32)[None, :]
    z = jnp.concatenate([t, jnp.cos(fr * wpos), -jnp.sin(fr * wpos)], -1)
    emb, width = w1.shape
    ch = w3.shape[1]
    z = jnp.pad(z, ((0, 0), (0, LANES - emb)))
    padw = LANES - width
    w1p = jnp.pad(w1, ((0, LANES - emb), (0, padw))).astype(BF)
    w2p = jnp.pad(w2, ((0, padw), (0, padw))).astype(BF)
    w3p = jnp.pad(w3, ((0, padw), (0, 0))).astype(BF)
    b1p = jnp.pad(b1.astype(F32), (0, padw)).reshape(1, LANES)
    b2p = jnp.pad(b2.astype(F32), (0, padw)).reshape(1, LANES)
    fqp = jnp.pad(freq.astype(F32), ((0, 0), (0, padw)))
    tl = min(tl, L)
    full = lambda shp: pl.BlockSpec(shp, lambda i: (0, 0))
    return pl.pallas_call(
        _filter_mlp_kernel,
        out_shape=(jax.ShapeDtypeStruct((L, ch), F32), jax.ShapeDtypeStruct((1, ch), F32)), grid=(L // tl,),
        in_specs=[pl.BlockSpec((tl, LANES), lambda i: (i, 0)), full((LANES, LANES)), full((1, LANES)),
                  full((LANES, LANES)), full((1, LANES)), full((LANES, ch)), full((2, LANES)), full((1, ch))],
        out_specs=(pl.BlockSpec((tl, ch), lambda i: (i, 0)), full((1, ch))),
        compiler_params=_cp(("arbitrary",)), name="filter_mlp",
    )(z, w1p, b1p, w2p, b2p, w3p, fqp, decay.astype(F32).reshape(1, ch))


def _cis(idx, n):
    ang = (2.0 * math.pi / n) * (idx % n).astype(F32)
    return jnp.cos(ang), -jnp.sin(ang)


def _blk(re, im):
    return jnp.concatenate([jnp.concatenate([re, -im], -1), jnp.concatenate([im, re], -1)], -2)


def _dft_tables(N):
    N2 = FFT_N2
    N1 = N // N2
    a = jnp.arange(N1)
    f1r, f1i = _cis(a[:, None] * a[None, :], N1)
    k = a[:, None, None] + N1 * jnp.arange(N2)[None, :, None]
    gr, gi = _cis(k * jnp.arange(N2)[None, None, :], N)
    return N1, f1r, f1i, gr, gi


def _mid_kernel(g_ref, gi_ref, kf_ref, a_ref, o_ref):
    n2, c = a_ref.shape[1], a_ref.shape[2]
    x = _dot(g_ref[...], a_ref[...].reshape(2 * n2, c).astype(BF))
    xr, xi = x[:n2], x[n2:]
    kr, ki = kf_ref[0], kf_ref[1]
    y = jnp.concatenate([xr * kr - xi * ki, xr * ki + xi * kr], axis=0).astype(BF)
    o_ref[...] = _dot(gi_ref[...], y).reshape(2, n2, c)


def _filt_mid_kernel(g_ref, a_ref, s_ref, o_ref):
    n2 = a_ref.shape[1]
    c = a_ref.shape[2] // 2
    x = _dot(g_ref[...], a_ref[...].reshape(2 * n2, 2 * c).astype(BF))
    s = s_ref[...]
    inv = 1.0 / (s[:, :c] + s[:, c:])
    o_ref[0] = (x[:n2, :c] + x[:n2, c:]) * inv
    o_ref[1] = (x[n2:, :c] - x[n2:, c:]) * inv


def _gate_inv_kernel(m_ref, d_ref, z_ref, gate_ref, sk_ref, o_ref, *, scale):
    conv = _dot(m_ref[...], d_ref[...].astype(BF)) * scale
    o_ref[...] = gate_ref[...] * (conv + z_ref[...] * sk_ref[...])


def _hyena(uc, filt, filt_sum, skip):
    _, B, L, C = uc.shape
    assert B % 2 == 0
    P = B // 2
    N = 2 * L
    N2 = FFT_N2
    N1, f1r, f1i, gr, gi = _dft_tables(N)
    h = N1 // 2
    cols = N2 * C
    m1 = jnp.concatenate([jnp.concatenate([f1r[:, :h], -f1i[:, :h]], 1),
                          jnp.concatenate([f1i[:, :h], f1r[:, :h]], 1)], 0).astype(BF)
    m1_real = jnp.concatenate([f1r[:, :h], f1i[:, :h]], 0).astype(BF)
    m1_inv = jnp.concatenate([jnp.concatenate([f1r[:h], f1i[:h]], 1),
                              jnp.concatenate([-f1i[:h], f1r[:h]], 1)], 0).astype(BF)
    g_fwd = _blk(gr, gi).astype(BF)
    g_inv = _blk(jnp.swapaxes(gr, 1, 2), -jnp.swapaxes(gi, 1, 2)).astype(BF)

    tc = min(cols, 4096)
    CF = filt.shape[1]
    fcols = N2 * CF
    tcf = min(fcols, 8192)
    fa = _mm(m1_real, filt.reshape(h, fcols), jax.ShapeDtypeStruct((2 * N1, fcols), F32), (fcols // tcf,),
             pl.BlockSpec((2 * N1, h), lambda j: (0, 0)), pl.BlockSpec((h, tcf), lambda j: (0, j)),
             pl.BlockSpec((2 * N1, tcf), lambda j: (0, j)), None, name="filt_s1")
    kf = pl.pallas_call(
        _filt_mid_kernel, out_shape=jax.ShapeDtypeStruct((N1, 2, N2, CF // 2), F32), grid=(N1,),
        in_specs=[pl.BlockSpec((None, 2 * N2, 2 * N2), lambda k: (k, 0, 0)),
                  pl.BlockSpec((2, None, N2, CF), lambda k: (0, k, 0, 0)),
                  pl.BlockSpec((1, CF), lambda k: (0, 0))],
        out_specs=pl.BlockSpec((None, 2, N2, CF // 2), lambda k: (k, 0, 0, 0)),
        compiler_params=_cp(("parallel",)), name="filt_mid",
    )(g_fwd, fa.reshape(2, N1, N2, CF), filt_sum)

    sk = jnp.tile(skip.astype(F32), (1, N2))
    z = uc[0].reshape(P, N1, cols)
    for n in range(B_ORDER):
        gate = uc[1 + n].reshape(P, N1, cols)
        a = _mm(m1, z, jax.ShapeDtypeStruct((P, 2 * N1, cols), F32), (P, cols // tc),
                pl.BlockSpec((2 * N1, N1), lambda p, j: (0, 0)), pl.BlockSpec((None, N1, tc), lambda p, j: (p, 0, j)),
                pl.BlockSpec((None, 2 * N1, tc), lambda p, j: (p, 0, j)), None, name="hy_s1")
        d = pl.pallas_call(
            _mid_kernel, out_shape=jax.ShapeDtypeStruct((P, 2, N1, N2, C), F32), grid=(N1, P),
            in_specs=[pl.BlockSpec((None, 2 * N2, 2 * N2), lambda k, p: (k, 0, 0)),
                      pl.BlockSpec((None, 2 * N2, 2 * N2), lambda k, p: (k, 0, 0)),
                      pl.BlockSpec((None, 2, N2, C), lambda k, p: (k, 0, 0, n)),
                      pl.BlockSpec((None, 2, None, N2, C), lambda k, p: (p, 0, k, 0, 0))],
            out_specs=pl.BlockSpec((None, 2, None, N2, C), lambda k, p: (p, 0, k, 0, 0)),
            compiler_params=_cp(("parallel", "parallel")), name="hy_mid",
        )(g_fwd, g_inv, kf, a.reshape(P, 2, N1, N2, C))
        z = pl.pallas_call(
            functools.partial(_gate_inv_kernel, scale=1.0 / N),
            out_shape=jax.ShapeDtypeStruct((P, N1, cols), F32), grid=(P, cols // tc),
            in_specs=[pl.BlockSpec((N1, 2 * N1), lambda p, j: (0, 0)),
                      pl.BlockSpec((None, 2 * N1, tc), lambda p, j: (p, 0, j)),
                      pl.BlockSpec((None, N1, tc), lambda p, j: (p, 0, j)),
                      pl.BlockSpec((None, N1, tc), lambda p, j: (p, 0, j)),
                      pl.BlockSpec((1, tc), lambda p, j: (0, j))],
            out_specs=pl.BlockSpec((None, N1, tc), lambda p, j: (p, 0, j)),
            compiler_params=_cp(("parallel", "parallel")), name="hy_inv_gate",
        )(m1_inv, d.reshape(P, 2 * N1, cols), z, gate, sk[n:n + 1])
    return z.reshape(B, L, C)


def _fnet_tables(L, D):
    gw = D // C_GROUPS
    c = jnp.arange(gw)
    cr, ci = _cis(c[:, None] * c[None, :], gw)
    eye = jnp.eye(C_GROUPS, dtype=F32)
    wch = jnp.concatenate([jnp.kron(eye, cr), jnp.kron(eye, ci)], 1).astype(BF)
    N1, f1r, f1i, gr, gi = _dft_tables(L)
    m1 = _blk(f1r, f1i).astype(BF)
    g_re = jnp.concatenate([gr, -gi], -1).astype(BF)
    return N1, wch, m1, g_re


def _fourier_mix(x, B, L):
    T, D = x.shape
    N2 = FFT_N2
    N1, wch, m1, g_re = _fnet_tables(L, D)
    tm = min(512, T)
    y = _mm(x, wch, jax.ShapeDtypeStruct((2, T, D), F32), (T // tm, 2),
            pl.BlockSpec((tm, D), lambda i, p: (i, 0)), pl.BlockSpec((D, D), lambda i, p: (0, p)),
            pl.BlockSpec((None, tm, D), lambda i, p: (p, i, 0)), None, name="fnet_ch")
    cols = N2 * D
    tc = min(cols, 8192)
    a = _mm(jnp.stack([m1[:, :N1], m1[:, N1:]]), y.reshape(2, B, N1, cols), jax.ShapeDtypeStruct((B, 2 * N1, cols), F32), (B, cols // tc, 2),
            pl.BlockSpec((None, 2 * N1, N1), lambda b, j, p: (p, 0, 0)),
            pl.BlockSpec((None, None, N1, tc), lambda b, j, p: (p, b, 0, j)),
            pl.BlockSpec((None, 2 * N1, tc), lambda b, j, p: (b, 0, j)), (2 * N1, tc), nk=2, kaxis=2, name="fnet_s1")
    gw = D // C_GROUPS
    f = _mm(g_re, a.reshape(B, 2, N1, N2, D), jax.ShapeDtypeStruct((B, N2, N1 * D), F32), (B, N1, 2),
            pl.BlockSpec((None, N2, N2), lambda b, k, p: (k, 0, p)),
            pl.BlockSpec((None, None, None, N2, D), lambda b, k, p: (b, p, k, 0, 0)),
            pl.BlockSpec((None, N2, D), lambda b, k, p: (b, 0, k)), (N2, D), nk=2, kaxis=2,
            scale=1.0 / math.sqrt(L * gw), name="fnet_s2")
    return f.reshape(T, D)


def _router_kernel(w_ref, x_ref, o_ref):
    logits = _dot_nt(w_ref[...], x_ref[...].astype(BF))
    m = jnp.max(logits, axis=0, keepdims=True)
    p = jnp.exp(logits - m)
    o_ref[...] = p / jnp.sum(p, axis=0, keepdims=True)


def _router(x, w_router_t, tm=1024):
    T, D = x.shape
    E = w_router_t.shape[0]
    tm = min(tm, T)
    return pl.pallas_call(
        _router_kernel, out_shape=jax.ShapeDtypeStruct((E, T), F32), grid=(T // tm,),
        in_specs=[pl.BlockSpec((E, D), lambda i: (0, 0)), pl.BlockSpec((tm, D), lambda i: (i, 0))],
        out_specs=pl.BlockSpec((E, tm), lambda i: (0, i)), compiler_params=_cp(("parallel",)), name="router")(
            w_router_t, x)


def _prefix_counts(mask_f, upper, lower):
    rowcs = _dot(mask_f.astype(BF), upper)
    tot = jnp.broadcast_to(rowcs[:, LANES - 1:LANES], mask_f.shape)
    offs = _dot(lower, tot.astype(BF))
    return offs + rowcs - mask_f, offs


def _select_kernel(a_ref, rank_ref, off_ref, *, cap):
    v = a_ref[...]
    R = v.shape[0]
    bits = pltpu.bitcast(v, I32)

    def step(i, thr):
        cand = thr | (jnp.int32(1) << (30 - i))
        cnt = jnp.sum((bits >= cand).astype(F32))
        return jnp.where(cnt >= cap, cand, thr)

    thr = lax.fori_loop(0, 31, step, jnp.int32(0))
    gt = bits > thr
    eq = bits == thr
    need = cap - jnp.sum(gt.astype(F32))
    li = lax.broadcasted_iota(I32, (LANES, LANES), 0)
    lj = lax.broadcasted_iota(I32, (LANES, LANES), 1)
    upper = (li <= lj).astype(BF)
    ri = lax.broadcasted_iota(I32, (R, R), 0)
    rj = lax.broadcasted_iota(I32, (R, R), 1)
    lower = (rj < ri).astype(BF)
    eq_rank, _ = _prefix_counts(eq.astype(F32), upper, lower)
    sel = gt | (eq & (eq_rank < need))
    rank, offs = _prefix_counts(sel.astype(F32), upper, lower)
    rank_ref[...] = jnp.where(sel, rank.astype(I32), -1)
    off_ref[...] = offs.astype(I32)


def _select(aff, cap):
    E, T = aff.shape
    R = T // LANES
    spec = pl.BlockSpec((None, R, LANES), lambda e: (e, 0, 0))
    return pl.pallas_call(
        functools.partial(_select_kernel, cap=cap),
        out_shape=(jax.ShapeDtypeStruct((E, R, LANES), I32), jax.ShapeDtypeStruct((E, R, LANES), I32)),
        grid=(E,), in_specs=[spec], out_specs=(spec, spec), compiler_params=_cp(("parallel",)), name="select")(
            aff.reshape(E, R, LANES))


def _compact_kernel(off_ref, rank_ref, idx_ref, *, R):
    e = pl.program_id(0)
    idx_ref[...] = jnp.zeros_like(idx_ref)
    sub = lax.broadcasted_iota(I32, (8, LANES), 0)
    lane = lax.broadcasted_iota(I32, (8, LANES), 1)
    lhs = jnp.where(sub == 0, lane, jnp.where(sub == 1, 1, 0)).astype(BF)
    slot = lax.broadcasted_iota(I32, (2 * LANES, LANES), 0)

    def chunk(r, c):
        a = off_ref[e, r] // LANES
        local = rank_ref[pl.ds(r, 1), :] - a * LANES
        onehot = (slot == local).astype(BF)
        res = _dot_nt(lhs, onehot)
        val = (res[0:1] + res[1:2] * lax.convert_element_type(r * LANES, F32)).astype(I32)
        idx_ref[pl.ds(a, 1), :] += val[:, :LANES]
        idx_ref[pl.ds(a + 1, 1), :] += val[:, LANES:]
        return c

    lax.fori_loop(0, R, chunk, 0)


def _compact(rank, rowoff, cap):
    E, R, _ = rank.shape
    nrow = cap // LANES + 2
    return pl.pallas_call(
        functools.partial(_compact_kernel, R=R),
        out_shape=jax.ShapeDtypeStruct((E, nrow, LANES), I32),
        grid_spec=pltpu.PrefetchScalarGridSpec(
            num_scalar_prefetch=1, grid=(E,),
            in_specs=[pl.BlockSpec((None, R, LANES), lambda e, off: (e, 0, 0))],
            out_specs=pl.BlockSpec((None, nrow, LANES), lambda e, off: (e, 0, 0))),
        compiler_params=_cp(("arbitrary",)), name="compact")(rowoff, rank)


def _ffn_kernel(idx_ref, x_hbm, wg_ref, wu_ref, wd_ref, o_ref, xbuf, xb, acc, sem, *, tm, nf):
    f = pl.program_id(2)

    @pl.when(f == 0)
    def _():
        def issue(r, c):
            t = idx_ref[0, 0, r]
            pltpu.make_async_copy(x_hbm.at[pl.ds(t, 1)], xbuf.at[pl.ds(r, 1)], sem).start()
            return c

        lax.fori_loop(0, tm, issue, 0)
        pltpu.make_async_copy(x_hbm.at[pl.ds(0, tm)], xbuf, sem).wait()
        xb[...] = xbuf[...].astype(BF)
        acc[...] = jnp.zeros_like(acc)

    x = xb[...]
    g = _dot(x, wg_ref[...])
    u = _dot(x, wu_ref[...])
    hcur = (g * jax.nn.sigmoid(g)) * u
    acc[...] += _dot(hcur.astype(BF), wd_ref[...])

    @pl.when(f == nf - 1)
    def _():
        o_ref[...] = acc[...]


def _ffn(x, idx, wg, wu, wd, layer, cap, tm=512, fc=1408):
    T, D = x.shape
    E = wg.shape[1]
    FF = wg.shape[3]
    tm = min(tm, cap)
    nt = cap // tm
    nf = FF // fc
    idx3 = idx[:, :cap // LANES, :].reshape(E * nt, 1, tm)
    return pl.pallas_call(
        functools.partial(_ffn_kernel, tm=tm, nf=nf),
        out_shape=jax.ShapeDtypeStruct((E, cap, D), F32), grid=(E, nt, nf),
        in_specs=[pl.BlockSpec((1, 1, tm), lambda e, i, f: (e * nt + i, 0, 0), memory_space=pltpu.SMEM),
                  pl.BlockSpec(memory_space=pl.ANY),
                  pl.BlockSpec((None, None, D, fc), lambda e, i, f: (layer, e, 0, f)),
                  pl.BlockSpec((None, None, D, fc), lambda e, i, f: (layer, e, 0, f)),
                  pl.BlockSpec((None, None, fc, D), lambda e, i, f: (layer, e, f, 0))],
        out_specs=pl.BlockSpec((None, tm, D), lambda e, i, f: (e, i, 0)),
        scratch_shapes=[pltpu.VMEM((tm, D), F32), pltpu.VMEM((tm, D), BF), pltpu.VMEM((tm, D), F32),
                        pltpu.SemaphoreType.DMA],
        compiler_params=_cp(("arbitrary", "arbitrary", "arbitrary")), name="expert_ffn")(idx3, x, wg, wu, wd)


def _combine_kernel(off_ref, rank_ref, aff_ref, x_ref, g_ref, b_ref, out_hbm, o_ref, win, win2, yacc, sem, sem2,
                    *, tt, E, cap, rpt, ntile, alpha):
    ti = pl.program_id(0)
    e = pl.program_id(1)
    step = ti * E + e
    slot = step % 2

    def window(ti_, e_):
        r0 = off_ref[e_, ti_ * rpt]
        return jnp.minimum((r0 // 8) * 8, cap - tt)

    def copy(ti_, e_, slot_):
        return pltpu.make_async_copy(out_hbm.at[e_, pl.ds(window(ti_, e_), tt)], win.at[slot_], sem.at[slot_])

    @pl.when(step == 0)
    def _():
        copy(ti, e, slot).start()

    nxt = step + 1

    @pl.when(nxt < ntile * E)
    def _():
        copy(nxt // E, nxt % E, 1 - slot).start()

    @pl.when(e == 0)
    def _():
        yacc[...] = jnp.zeros_like(yacc)

    lane = lax.broadcasted_iota(I32, rank_ref.shape, 1)
    rank = jnp.sum(jnp.where(lane == e, rank_ref[...], 0).astype(F32), axis=1, keepdims=True).astype(I32)
    gate = jnp.sum(jnp.where(lane == e, aff_ref[...], 0.0), axis=1, keepdims=True)
    r0a = window(ti, e)
    col = lax.broadcasted_iota(I32, (tt, tt), 1)
    local = rank - r0a
    copy(ti, e, slot).wait()
    p = ((col == local) & (rank >= 0)).astype(BF)
    yacc[...] += gate * _dot(p, win[slot].astype(BF))

    r_end = off_ref[e, (ti + 1) * rpt]

    @pl.when(r_end > r0a + tt)
    def _():
        r0b = r0a + tt
        c2 = pltpu.make_async_copy(out_hbm.at[e, pl.ds(r0b, 8)], win2, sem2)
        c2.start()
        c2.wait()
        extra = jnp.zeros_like(yacc)
        for j in range(8):
            row = win2[j:j + 1, :].astype(BF).astype(F32)
            extra = extra + jnp.where(rank - r0b == j, 1.0, 0.0) * row
        yacc[...] += gate * extra

    @pl.when(e == E - 1)
    def _():
        o_ref[...] = _ln_rows(alpha * x_ref[...] + yacc[...], g_ref[...], b_ref[...])


def _combine_ln(out, rank_t, aff_t, rowoff, x, g, b, cap, alpha):
    T, D = x.shape
    E = out.shape[0]
    tt = min(256, cap // 2)
    rpt = tt // LANES
    ntile = T // tt
    kern = functools.partial(_combine_kernel, tt=tt, E=E, cap=cap, rpt=rpt, ntile=ntile, alpha=alpha)
    return pl.pallas_call(
        kern, out_shape=jax.ShapeDtypeStruct((T, D), F32),
        grid_spec=pltpu.PrefetchScalarGridSpec(
            num_scalar_prefetch=1, grid=(ntile, E),
            in_specs=[pl.BlockSpec((tt, E), lambda i, e, off: (i, 0)),
                      pl.BlockSpec((tt, E), lambda i, e, off: (i, 0)),
                      pl.BlockSpec((tt, D), lambda i, e, off: (i, 0)),
                      pl.BlockSpec((1, D), lambda i, e, off: (0, 0)),
                      pl.BlockSpec((1, D), lambda i, e, off: (0, 0)),
                      pl.BlockSpec(memory_space=pl.ANY)],
            out_specs=pl.BlockSpec((tt, D), lambda i, e, off: (i, 0)),
            scratch_shapes=[pltpu.VMEM((2, tt, D), F32), pltpu.VMEM((8, D), F32), pltpu.VMEM((tt, D), F32),
                            pltpu.SemaphoreType.DMA((2,)), pltpu.SemaphoreType.DMA]),
        compiler_params=_cp(("arbitrary", "arbitrary")), name="combine_ln",
    )(rowoff, rank_t, aff_t, x, g.reshape(1, D), b.reshape(1, D), out)


def _expert_choice_ln(x, w_router_t, wg, wu, wd, layer, g, b, alpha):
    T, D = x.shape
    E = N_EXPERTS
    cap = max(1, EC_FACTOR * T // E)
    aff = _router(x, w_router_t)
    rank, rowoff = _select(aff, cap)
    rowoff = jnp.concatenate([rowoff[:, :, 0], jnp.full((E, 1), cap, I32)], axis=1)
    idx = _compact(rank, rowoff, cap)
    out = _ffn(x, idx, wg, wu, wd, layer, cap)
    return _combine_ln(out, rank.reshape(E, T).T, aff.T, rowoff, x, g, b, cap, alpha)


def _trunk(x, p, depth):
    B, L, D = x.shape
    T = B * L
    alpha = (2 * depth) ** 0.25
    x = x.reshape(T, D)
    filt_cache = {}
    for layer in range(depth):
        j = layer // 2
        if layer % 2 == 0:
            hproj = _xw(x, p["ab_w_in"][j], name="ab_in").reshape(B, L, -1)
            lambda_init = 0.8 - 0.6 * math.exp(-0.3 * layer)
            lf = p["diff_lambda"][j].astype(F32)
            lam_full = jnp.exp(jnp.sum(lf[0] * lf[1])) - jnp.exp(jnp.sum(lf[2] * lf[3])) + lambda_init
            a_out = _diff_attention(hproj, lam_full, p["diff_subln_g"][j], p["rel_bias"], lambda_init, T=512)
            uc = _short_conv(hproj, p["hy_conv_w"][j], p["hy_conv_b"][j])
            filt, filt_sum = _filter_mlp(L, p["hy_f_w1"][j], p["hy_f_b1"][j], p["hy_f_w2"][j], p["hy_f_b2"][j],
                                         p["hy_f_w3"][j], p["hy_f_freq"][j], p["hy_decay"][j])
            b_out = _hyena(uc, filt, filt_sum, p["hy_skip"][j])
            w_out = p["ab_w_out"][j]
            pairs = [(a_out.reshape(T, D_A), w_out[:D_A]), (b_out.reshape(T, D_B), w_out[D_A:])]
        else:
            pairs = [(_fourier_mix(x, B, L), p["c_w_out"][j])]
        x = _mm_res_ln(pairs, x, p["ln_g"][layer, 0], p["ln_b"][layer, 0], alpha)
        x = _expert_choice_ln(x, p["ec_router_t"][layer], p["ec_w_gate"], p["ec_w_up"], p["ec_w_down"], layer,
                              p["ln_g"][layer, 1], p["ln_b"][layer, 1], alpha)
    return x.reshape(B, L, D)


def kernel(x_prompt, x_sample, rel_bias, ab_w_in, ab_w_out, diff_lambda, diff_subln_g, hy_conv_w, hy_conv_b,
           hy_f_w1, hy_f_b1, hy_f_w2, hy_f_b2, hy_f_w3, hy_f_freq, hy_decay, hy_skip,
           c_w_out, ec_router, ec_w_gate, ec_w_up, ec_w_down, ln_g, ln_b):
    depth = ec_router.shape[0]
    p = dict(
        rel_bias=rel_bias, ab_w_in=ab_w_in.astype(BF), ab_w_out=ab_w_out.astype(BF), diff_lambda=diff_lambda,
        diff_subln_g=diff_subln_g, hy_conv_w=hy_conv_w, hy_conv_b=hy_conv_b, hy_f_w1=hy_f_w1, hy_f_b1=hy_f_b1,
        hy_f_w2=hy_f_w2, hy_f_b2=hy_f_b2, hy_f_w3=hy_f_w3, hy_f_freq=hy_f_freq, hy_decay=hy_decay, hy_skip=hy_skip,
        c_w_out=c_w_out.astype(BF), ec_router_t=jnp.swapaxes(ec_router, 1, 2).astype(BF),
        ec_w_gate=ec_w_gate.astype(BF), ec_w_up=ec_w_up.astype(BF), ec_w_down=ec_w_down.astype(BF),
        ln_g=ln_g.astype(F32), ln_b=ln_b.astype(F32))
    return _trunk(x_prompt, p, depth), _trunk(x_sample, p, depth)
```

```python
import functools
import math

import jax
import jax.numpy as jnp
from jax import lax
from jax.experimental import pallas as pl
from jax.experimental.pallas import tpu as pltpu

BF = jnp.bfloat16
F32 = jnp.float32
I32 = jnp.int32

A_HEADS = 4
A_DV = 128
A_DK = 64
D_A = 512
D_B = 512
B_ORDER = 2
REL_BUCKETS = 32
REL_MAX_DIST = 128
FILTER_BANDS = 16
N_EXPERTS = 16
EC_FACTOR = 2
LN_EPS = 1e-5
C_GROUPS = 4
LANES = 128
FFT_N2 = 128
VMEM_LIMIT = 52 * 1024 * 1024


def _cp(sem, vmem=VMEM_LIMIT):
    return pltpu.CompilerParams(dimension_semantics=sem, vmem_limit_bytes=vmem)


def _dot(a, b):
    return jnp.dot(a, b, preferred_element_type=F32)


def _dot_nt(a, b):
    return lax.dot_general(a, b, (((1,), (1,)), ((), ())), preferred_element_type=F32)


def _mm_kernel(a_ref, b_ref, o_ref, *scratch, nk, kaxis, scale):
    prod = _dot(a_ref[...].astype(BF), b_ref[...].astype(BF))
    if nk == 1:
        o_ref[...] = (prod * scale).astype(o_ref.dtype)
        return
    acc = scratch[0]
    k = pl.program_id(kaxis)

    @pl.when(k == 0)
    def _():
        acc[...] = jnp.zeros_like(acc)

    acc[...] += prod

    @pl.when(k == nk - 1)
    def _():
        o_ref[...] = (acc[...] * scale).astype(o_ref.dtype)


def _mm(a, b, out_shape, grid, a_spec, b_spec, o_spec, mn, *, nk=1, kaxis=None, scale=1.0, name):
    sem = tuple("arbitrary" if i == kaxis else "parallel" for i in range(len(grid)))
    return pl.pallas_call(
        functools.partial(_mm_kernel, nk=nk, kaxis=kaxis, scale=scale),
        out_shape=out_shape, grid=grid, in_specs=[a_spec, b_spec], out_specs=o_spec,
        scratch_shapes=[pltpu.VMEM(mn, F32)] if nk > 1 else [],
        compiler_params=_cp(sem), name=name)(a, b)


def _xw(x, w, tm=512, name="xw"):
    M, K = x.shape
    N = w.shape[1]
    tm = min(tm, M)
    return _mm(x, w, jax.ShapeDtypeStruct((M, N), F32), (M // tm,),
               pl.BlockSpec((tm, K), lambda i: (i, 0)), pl.BlockSpec((K, N), lambda i: (0, 0)),
               pl.BlockSpec((tm, N), lambda i: (i, 0)), (tm, N), name=name)


def _ln_rows(y, g, b):
    mu = jnp.mean(y, -1, keepdims=True)
    d = y - mu
    var = jnp.mean(d * d, -1, keepdims=True)
    return d * lax.rsqrt(var + LN_EPS) * g + b


def _mm_res_ln_kernel(*refs, npairs, alpha):
    x_ref, g_ref, b_ref = refs[2 * npairs:2 * npairs + 3]
    o_ref = refs[2 * npairs + 3]
    y = alpha * x_ref[...]
    for i in range(npairs):
        y = y + _dot(refs[2 * i][...].astype(BF), refs[2 * i + 1][...])
    o_ref[...] = _ln_rows(y, g_ref[...], b_ref[...])


def _mm_res_ln(pairs, x, g, b, alpha, tm=512, name="mm_res_ln"):
    M, D = x.shape
    tm = min(tm, M)
    specs, args = [], []
    for a, w in pairs:
        specs += [pl.BlockSpec((tm, a.shape[1]), lambda i: (i, 0)), pl.BlockSpec(w.shape, lambda i: (0, 0))]
        args += [a, w]
    specs += [pl.BlockSpec((tm, D), lambda i: (i, 0)), pl.BlockSpec((1, D), lambda i: (0, 0)),
              pl.BlockSpec((1, D), lambda i: (0, 0))]
    args += [x, g.reshape(1, D), b.reshape(1, D)]
    return pl.pallas_call(
        functools.partial(_mm_res_ln_kernel, npairs=len(pairs), alpha=alpha),
        out_shape=jax.ShapeDtypeStruct((M, D), F32), grid=(M // tm,), in_specs=specs,
        out_specs=pl.BlockSpec((tm, D), lambda i: (i, 0)), compiler_params=_cp(("parallel",)), name=name)(*args)


def _rel_bucket(rel):
    nb = REL_BUCKETS // 2
    max_exact = nb // 2
    n = jnp.abs(rel)
    large = max_exact + (jnp.log(jnp.maximum(n, 1).astype(F32) / max_exact)
                         / math.log(REL_MAX_DIST / max_exact) * (nb - max_exact)).astype(I32)
    large = jnp.minimum(large, nb - 1)
    return jnp.where(rel > 0, nb, 0) + jnp.where(n < max_exact, n, large)


def _bias_tiles(rel_table, T):
    assert T >= REL_MAX_DIST
    i = jnp.arange(T)
    rel = (jnp.arange(-2, 3)[:, None, None] * T + i[None, None, :]) - i[None, :, None]
    bucket = _rel_bucket(rel)[None]
    tab = rel_table.astype(F32).T[:, :, None, None, None]
    out = jnp.zeros((rel_table.shape[1],) + rel.shape, F32)
    for b in range(REL_BUCKETS):
        out = jnp.where(bucket == b, tab[:, b], out)
    return out


def _attn_kernel(lam_ref, q_ref, k_ref, v_ref, bias_ref, g_ref, o_ref, *, T, nk, out_scale):
    qi = pl.program_id(2)
    tq = q_ref.shape[0]
    q = q_ref[...] * (A_DK ** -0.5)
    lane = lax.broadcasted_iota(I32, q.shape, 1)
    qs = (jnp.where(lane < A_DK, q, 0.0).astype(BF), jnp.where(lane >= A_DK, q, 0.0).astype(BF))

    def body(kj, carry):
        start = pl.multiple_of(kj * T, T)
        kc = k_ref[pl.ds(start, T), :].astype(BF)
        vc = v_ref[pl.ds(start, T), :].astype(BF)
        bias = bias_ref[jnp.clip(kj - qi, -2, 2) + 2]
        new = []
        for mi in range(2):
            m, l, a = carry[3 * mi:3 * mi + 3]
            s = _dot_nt(qs[mi], kc) + bias
            m_new = jnp.maximum(m, jnp.max(s, axis=1, keepdims=True))
            alpha = jnp.exp(m - m_new)
            p = jnp.exp(s - m_new)
            l = alpha * l + jnp.sum(p, axis=1, keepdims=True)
            a = alpha * a + _dot(p.astype(BF), vc)
            new += [m_new, l, a]
        return tuple(new)

    init = (jnp.full((tq, 1), -jnp.inf, F32), jnp.zeros((tq, 1), F32), jnp.zeros((tq, A_DV), F32)) * 2
    m0, l0, a0, m1, l1, a1 = lax.fori_loop(0, nk, body, init)
    o = a0 / l0 - lam_ref[0] * (a1 / l1)
    o = o * lax.rsqrt(jnp.mean(o * o, -1, keepdims=True) + LN_EPS) * g_ref[...] * out_scale
    o_ref[...] = o


def _diff_attention(hproj, lam_full, sub_g, rel_table, lambda_init, T):
    B, L, _ = hproj.shape
    T = min(T, L)
    bias = _bias_tiles(rel_table, T)
    H = A_HEADS
    return pl.pallas_call(
        functools.partial(_attn_kernel, T=T, nk=L // T, out_scale=1.0 - lambda_init),
        out_shape=jax.ShapeDtypeStruct((B, L, D_A), F32), grid=(B, H, L // T),
        in_specs=[pl.BlockSpec(memory_space=pltpu.SMEM),
                  pl.BlockSpec((None, T, LANES), lambda b, h, i: (b, i, h)),
                  pl.BlockSpec((None, L, LANES), lambda b, h, i: (b, 0, H + h)),
                  pl.BlockSpec((None, L, LANES), lambda b, h, i: (b, 0, 2 * H + h)),
                  pl.BlockSpec((None, 5, T, T), lambda b, h, i: (h, 0, 0, 0)),
                  pl.BlockSpec((1, A_DV), lambda b, h, i: (0, 0))],
        out_specs=pl.BlockSpec((None, T, LANES), lambda b, h, i: (b, i, h)),
        compiler_params=_cp(("parallel", "parallel", "arbitrary")), name="diff_attn",
    )(lam_full.reshape(1), hproj, hproj, hproj, bias, sub_g.astype(F32).reshape(1, A_DV))


def _short_conv_kernel(u_ref, w_ref, b_ref, o_ref):
    u = u_ref[...]
    L = u.shape[0]
    row = lax.broadcasted_iota(I32, u.shape, 0)
    up = jnp.where(row == 0, 0.0, pltpu.roll(u, 1, 0))
    dn = jnp.where(row == L - 1, 0.0, pltpu.roll(u, L - 1, 0))
    w = w_ref[...]
    o_ref[...] = up * w[0:1] + u * w[1:2] + dn * w[2:3] + b_ref[...]


def _short_conv(hproj, conv_w, conv_b):
    B, L, _ = hproj.shape
    nblk = 3 * D_B // LANES
    off = 3 * D_A // LANES
    per = D_B // LANES
    return pl.pallas_call(
        _short_conv_kernel, out_shape=jax.ShapeDtypeStruct((3, B, L, D_B), F32), grid=(B, nblk),
        in_specs=[pl.BlockSpec((None, L, LANES), lambda b, j: (b, 0, off + j)),
                  pl.BlockSpec((3, LANES), lambda b, j: (0, j)),
                  pl.BlockSpec((1, LANES), lambda b, j: (0, j))],
        out_specs=pl.BlockSpec((None, None, L, LANES), lambda b, j: (j // per, b, 0, j % per)),
        compiler_params=_cp(("parallel", "parallel")), name="short_conv",
    )(hproj, conv_w.astype(F32), conv_b.astype(F32).reshape(1, -1))


def _filter_mlp_kernel(z_ref, w1_ref, b1_ref, w2_ref, b2_ref, w3_ref, fq_ref, dec_ref, h_ref, s_ref):
    i = pl.program_id(0)
    z = z_ref[...]
    fq = fq_ref[...]
    h = jnp.sin(fq[0:1] * (_dot(z.astype(BF), w1_ref[...]) + b1_ref[...]))
    h = jnp.sin(fq[1:2] * (_dot(h.astype(BF), w2_ref[...]) + b2_ref[...]))
    t = z[:, 0:1]
    h = _dot(h.astype(BF), w3_ref[...]) * jnp.exp(-t * jnp.abs(dec_ref[...]))

    @pl.when(i == 0)
    def _():
        s_ref[...] = jnp.zeros_like(s_ref)

    s_ref[...] += jnp.sum(jnp.abs(h), axis=0, keepdims=True)
    row = lax.broadcasted_iota(I32, h.shape, 0)
    col = lax.broadcasted_iota(I32, h.shape, 1)
    half = h.shape[1] // 2
    h_ref[...] = jnp.where((row + i * h.shape[0] == 0) & (col >= half), 0.0, h)


def _filter_mlp(L, w1, b1, w2, b2, w3, freq, decay, tl=512):
    t = jnp.linspace(0.0, 1.0, L, dtype=F32)[:, None]
    wpos = 2.0 * math.pi * jnp.arange(L, dtype=F32)[:, None] / L
    fr = jnp.linspace(1e-4, FILTER_BANDS - 1, FILTER_BANDS, dtype=F32)[None, :]
    z = jnp.concatenate([t, jnp.cos(fr * wpos), -jnp.sin(fr * wpos)], -1)
    emb, width = w1.shape
    ch = w3.shape[1]
    z = jnp.pad(z, ((0, 0), (0, LANES - emb)))
    padw = LANES - width
    w1p = jnp.pad(w1, ((0, LANES - emb), (0, padw))).astype(BF)
    w2p = jnp.pad(w2, ((0, padw), (0, padw))).astype(BF)
    w3p = jnp.pad(w3, ((0, padw), (0, 0))).astype(BF)
    b1p = jnp.pad(b1.astype(F32), (0, padw)).reshape(1, LANES)
    b2p = jnp.pad(b2.astype(F32), (0, padw)).reshape(1, LANES)
    fqp = jnp.pad(freq.astype(F32), ((0, 0), (0, padw)))
    tl = min(tl, L)
    full = lambda shp: pl.BlockSpec(shp, lambda i: (0, 0))
    return pl.pallas_call(
        _filter_mlp_kernel,
        out_shape=(jax.ShapeDtypeStruct((L, ch), F32), jax.ShapeDtypeStruct((1, ch), F32)), grid=(L // tl,),
        in_specs=[pl.BlockSpec((tl, LANES), lambda i: (i, 0)), full((LANES, LANES)), full((1, LANES)),
                  full((LANES, LANES)), full((1, LANES)), full((LANES, ch)), full((2, LANES)), full((1, ch))],
        out_specs=(pl.BlockSpec((tl, ch), lambda i: (i, 0)), full((1, ch))),
        compiler_params=_cp(("arbitrary",)), name="filter_mlp",
    )(z, w1p, b1p, w2p, b2p, w3p, fqp, decay.astype(F32).reshape(1, ch))


def _cis(idx, n):
    ang = (2.0 * math.pi / n) * (idx % n).astype(F32)
    return jnp.cos(ang), -jnp.sin(ang)


def _blk(re, im):
    return jnp.concatenate([jnp.concatenate([re, -im], -1), jnp.concatenate([im, re], -1)], -2)


def _dft_tables(N):
    N2 = FFT_N2
    N1 = N // N2
    a = jnp.arange(N1)
    f1r, f1i = _cis(a[:, None] * a[None, :], N1)
    k = a[:, None, None] + N1 * jnp.arange(N2)[None, :, None]
    gr, gi = _cis(k * jnp.arange(N2)[None, None, :], N)
    return N1, f1r, f1i, gr, gi


def _mid_kernel(g_ref, gi_ref, kf_ref, a_ref, o_ref):
    n2, c = a_ref.shape[1], a_ref.shape[2]
    x = _dot(g_ref[...], a_ref[...].reshape(2 * n2, c).astype(BF))
    xr, xi = x[:n2], x[n2:]
    kr, ki = kf_ref[0], kf_ref[1]
    y = jnp.concatenate([xr * kr - xi * ki, xr * ki + xi * kr], axis=0).astype(BF)
    o_ref[...] = _dot(gi_ref[...], y).reshape(2, n2, c)


def _filt_mid_kernel(g_ref, a_ref, s_ref, o_ref):
    n2 = a_ref.shape[1]
    c = a_ref.shape[2] // 2
    x = _dot(g_ref[...], a_ref[...].reshape(2 * n2, 2 * c).astype(BF))
    s = s_ref[...]
    inv = 1.0 / (s[:, :c] + s[:, c:])
    o_ref[0] = (x[:n2, :c] + x[:n2, c:]) * inv
    o_ref[1] = (x[n2:, :c] - x[n2:, c:]) * inv


def _gate_inv_kernel(m_ref, d_ref, z_ref, gate_ref, sk_ref, o_ref, *, scale):
    conv = _dot(m_ref[...], d_ref[...].astype(BF)) * scale
    o_ref[...] = gate_ref[...] * (conv + z_ref[...] * sk_ref[...])


def _hyena(uc, filt, filt_sum, skip):
    _, B, L, C = uc.shape
    assert B % 2 == 0
    P = B // 2
    N = 2 * L
    N2 = FFT_N2
    N1, f1r, f1i, gr, gi = _dft_tables(N)
    h = N1 // 2
    cols = N2 * C
    m1 = jnp.concatenate([jnp.concatenate([f1r[:, :h], -f1i[:, :h]], 1),
                          jnp.concatenate([f1i[:, :h], f1r[:, :h]], 1)], 0).astype(BF)
    m1_real = jnp.concatenate([f1r[:, :h], f1i[:, :h]], 0).astype(BF)
    m1_inv = jnp.concatenate([jnp.concatenate([f1r[:h], f1i[:h]], 1),
                              jnp.concatenate([-f1i[:h], f1r[:h]], 1)], 0).astype(BF)
    g_fwd = _blk(gr, gi).astype(BF)
    g_inv = _blk(jnp.swapaxes(gr, 1, 2), -jnp.swapaxes(gi, 1, 2)).astype(BF)

    tc = min(cols, 4096)
    CF = filt.shape[1]
    fcols = N2 * CF
    tcf = min(fcols, 8192)
    fa = _mm(m1_real, filt.reshape(h, fcols), jax.ShapeDtypeStruct((2 * N1, fcols), F32), (fcols // tcf,),
             pl.BlockSpec((2 * N1, h), lambda j: (0, 0)), pl.BlockSpec((h, tcf), lambda j: (0, j)),
             pl.BlockSpec((2 * N1, tcf), lambda j: (0, j)), None, name="filt_s1")
    kf = pl.pallas_call(
        _filt_mid_kernel, out_shape=jax.ShapeDtypeStruct((N1, 2, N2, CF // 2), F32), grid=(N1,),
        in_specs=[pl.BlockSpec((None, 2 * N2, 2 * N2), lambda k: (k, 0, 0)),
                  pl.BlockSpec((2, None, N2, CF), lambda k: (0, k, 0, 0)),
                  pl.BlockSpec((1, CF), lambda k: (0, 0))],
        out_specs=pl.BlockSpec((None, 2, N2, CF // 2), lambda k: (k, 0, 0, 0)),
        compiler_params=_cp(("parallel",)), name="filt_mid",
    )(g_fwd, fa.reshape(2, N1, N2, CF), filt_sum)

    sk = jnp.tile(skip.astype(F32), (1, N2))
    z = uc[0].reshape(P, N1, cols)
    for n in range(B_ORDER):
        gate = uc[1 + n].reshape(P, N1, cols)
        a = _mm(m1, z, jax.ShapeDtypeStruct((P, 2 * N1, cols), F32), (P, cols // tc),
                pl.BlockSpec((2 * N1, N1), lambda p, j: (0, 0)), pl.BlockSpec((None, N1, tc), lambda p, j: (p, 0, j)),
                pl.BlockSpec((None, 2 * N1, tc), lambda p, j: (p, 0, j)), None, name="hy_s1")
        d = pl.pallas_call(
            _mid_kernel, out_shape=jax.ShapeDtypeStruct((P, 2, N1, N2, C), F32), grid=(N1, P),
            in_specs=[pl.BlockSpec((None, 2 * N2, 2 * N2), lambda k, p: (k, 0, 0)),
                      pl.BlockSpec((None, 2 * N2, 2 * N2), lambda k, p: (k, 0, 0)),
                      pl.BlockSpec((None, 2, N2, C), lambda k, p: (k, 0, 0, n)),
                      pl.BlockSpec((None, 2, None, N2, C), lambda k, p: (p, 0, k, 0, 0))],
            out_specs=pl.BlockSpec((None, 2, None, N2, C), lambda k, p: (p, 0, k, 0, 0)),
            compiler_params=_cp(("parallel", "parallel")), name="hy_mid",
        )(g_fwd, g_inv, kf, a.reshape(P, 2, N1, N2, C))
        z = pl.pallas_call(
            functools.partial(_gate_inv_kernel, scale=1.0 / N),
            out_shape=jax.ShapeDtypeStruct((P, N1, cols), F32), grid=(P, cols // tc),
            in_specs=[pl.BlockSpec((N1, 2 * N1), lambda p, j: (0, 0)),
                      pl.BlockSpec((None, 2 * N1, tc), lambda p, j: (p, 0, j)),
                      pl.BlockSpec((None, N1, tc), lambda p, j: (p, 0, j)),
                      pl.BlockSpec((None, N1, tc), lambda p, j: (p, 0, j)),
                      pl.BlockSpec((1, tc), lambda p, j: (0, j))],
            out_specs=pl.BlockSpec((None, N1, tc), lambda p, j: (p, 0, j)),
            compiler_params=_cp(("parallel", "parallel")), name="hy_inv_gate",
        )(m1_inv, d.reshape(P, 2 * N1, cols), z, gate, sk[n:n + 1])
    return z.reshape(B, L, C)


def _fnet_tables(L, D):
    gw = D // C_GROUPS
    c = jnp.arange(gw)
    cr, ci = _cis(c[:, None] * c[None, :], gw)
    eye = jnp.eye(C_GROUPS, dtype=F32)
    wch = jnp.concatenate([jnp.kron(eye, cr), jnp.kron(eye, ci)], 1).astype(BF)
    N1, f1r, f1i, gr, gi = _dft_tables(L)
    m1 = _blk(f1r, f1i).astype(BF)
    g_re = jnp.concatenate([gr, -gi], -1).astype(BF)
    return N1, wch, m1, g_re


def _fourier_mix(x, B, L):
    T, D = x.shape
    N2 = FFT_N2
    N1, wch, m1, g_re = _fnet_tables(L, D)
    tm = min(512, T)
    y = _mm(x, wch, jax.ShapeDtypeStruct((2, T, D), F32), (T // tm, 2),
            pl.BlockSpec((tm, D), lambda i, p: (i, 0)), pl.BlockSpec((D, D), lambda i, p: (0, p)),
            pl.BlockSpec((None, tm, D), lambda i, p: (p, i, 0)), None, name="fnet_ch")
    cols = N2 * D
    tc = min(cols, 8192)
    a = _mm(jnp.stack([m1[:, :N1], m1[:, N1:]]), y.reshape(2, B, N1, cols), jax.ShapeDtypeStruct((B, 2 * N1, cols), F32), (B, cols // tc, 2),
            pl.BlockSpec((None, 2 * N1, N1), lambda b, j, p: (p, 0, 0)),
            pl.BlockSpec((None, None, N1, tc), lambda b, j, p: (p, b, 0, j)),
            pl.BlockSpec((None, 2 * N1, tc), lambda b, j, p: (b, 0, j)), (2 * N1, tc), nk=2, kaxis=2, name="fnet_s1")
    gw = D // C_GROUPS
    f = _mm(g_re, a.reshape(B, 2, N1, N2, D), jax.ShapeDtypeStruct((B, N2, N1 * D), F32), (B, N1, 2),
            pl.BlockSpec((None, N2, N2), lambda b, k, p: (k, 0, p)),
            pl.BlockSpec((None, None, None, N2, D), lambda b, k, p: (b, p, k, 0, 0)),
            pl.BlockSpec((None, N2, D), lambda b, k, p: (b, 0, k)), (N2, D), nk=2, kaxis=2,
            scale=1.0 / math.sqrt(L * gw), name="fnet_s2")
    return f.reshape(T, D)


def _router_kernel(w_ref, x_ref, o_ref):
    logits = _dot_nt(w_ref[...], x_ref[...].astype(BF))
    m = jnp.max(logits, axis=0, keepdims=True)
    p = jnp.exp(logits - m)
    o_ref[...] = p / jnp.sum(p, axis=0, keepdims=True)


def _router(x, w_router_t, tm=1024):
    T, D = x.shape
    E = w_router_t.shape[0]
    tm = min(tm, T)
    return pl.pallas_call(
        _router_kernel, out_shape=jax.ShapeDtypeStruct((E, T), F32), grid=(T // tm,),
        in_specs=[pl.BlockSpec((E, D), lambda i: (0, 0)), pl.BlockSpec((tm, D), lambda i: (i, 0))],
        out_specs=pl.BlockSpec((E, tm), lambda i: (0, i)), compiler_params=_cp(("parallel",)), name="router")(
            w_router_t, x)


def _prefix_counts(mask_f, upper, lower):
    rowcs = _dot(mask_f.astype(BF), upper)
    tot = jnp.broadcast_to(rowcs[:, LANES - 1:LANES], mask_f.shape)
    offs = _dot(lower, tot.astype(BF))
    return offs + rowcs - mask_f, offs


def _select_kernel(a_ref, rank_ref, off_ref, *, cap):
    v = a_ref[...]
    R = v.shape[0]
    bits = pltpu.bitcast(v, I32)

    def step(i, thr):
        cand = thr | (jnp.int32(1) << (30 - i))
        cnt = jnp.sum((bits >= cand).astype(F32))
        return jnp.where(cnt >= cap, cand, thr)

    thr = lax.fori_loop(0, 31, step, jnp.int32(0))
    gt = bits > thr
    eq = bits == thr
    need = cap - jnp.sum(gt.astype(F32))
    li = lax.broadcasted_iota(I32, (LANES, LANES), 0)
    lj = lax.broadcasted_iota(I32, (LANES, LANES), 1)
    upper = (li <= lj).astype(BF)
    ri = lax.broadcasted_iota(I32, (R, R), 0)
    rj = lax.broadcasted_iota(I32, (R, R), 1)
    lower = (rj < ri).astype(BF)
    eq_rank, _ = _prefix_counts(eq.astype(F32), upper, lower)
    sel = gt | (eq & (eq_rank < need))
    rank, offs = _prefix_counts(sel.astype(F32), upper, lower)
    rank_ref[...] = jnp.where(sel, rank.astype(I32), -1)
    off_ref[...] = offs.astype(I32)


def _select(aff, cap):
    E, T = aff.shape
    R = T // LANES
    spec = pl.BlockSpec((None, R, LANES), lambda e: (e, 0, 0))
    return pl.pallas_call(
        functools.partial(_select_kernel, cap=cap),
        out_shape=(jax.ShapeDtypeStruct((E, R, LANES), I32), jax.ShapeDtypeStruct((E, R, LANES), I32)),
        grid=(E,), in_specs=[spec], out_specs=(spec, spec), compiler_params=_cp(("parallel",)), name="select")(
            aff.reshape(E, R, LANES))


def _compact_kernel(off_ref, rank_ref, idx_ref, *, R):
    e = pl.program_id(0)
    idx_ref[...] = jnp.zeros_like(idx_ref)
    sub = lax.broadcasted_iota(I32, (8, LANES), 0)
    lane = lax.broadcasted_iota(I32, (8, LANES), 1)
    lhs = jnp.where(sub == 0, lane, jnp.where(sub == 1, 1, 0)).astype(BF)
    slot = lax.broadcasted_iota(I32, (2 * LANES, LANES), 0)

    def chunk(r, c):
        a = off_ref[e, r] // LANES
        local = rank_ref[pl.ds(r, 1), :] - a * LANES
        onehot = (slot == local).astype(BF)
        res = _dot_nt(lhs, onehot)
        val = (res[0:1] + res[1:2] * lax.convert_element_type(r * LANES, F32)).astype(I32)
        idx_ref[pl.ds(a, 1), :] += val[:, :LANES]
        idx_ref[pl.ds(a + 1, 1), :] += val[:, LANES:]
        return c

    lax.fori_loop(0, R, chunk, 0, unroll=8)


def _compact(rank, rowoff, cap):
    E, R, _ = rank.shape
    nrow = cap // LANES + 2
    return pl.pallas_call(
        functools.partial(_compact_kernel, R=R),
        out_shape=jax.ShapeDtypeStruct((E, nrow, LANES), I32),
        grid_spec=pltpu.PrefetchScalarGridSpec(
            num_scalar_prefetch=1, grid=(E,),
            in_specs=[pl.BlockSpec((None, R, LANES), lambda e, off: (e, 0, 0))],
            out_specs=pl.BlockSpec((None, nrow, LANES), lambda e, off: (e, 0, 0))),
        compiler_params=_cp(("arbitrary",)), name="compact")(rowoff, rank)


def _ffn_kernel(idx_ref, idxn_ref, x_hbm, wg_ref, wu_ref, wd_ref, o_ref, xbuf, xb, acc, sem, *, tm, nf, ntiles):
    f = pl.program_id(2)
    tile = pl.program_id(0) * pl.num_programs(1) + pl.program_id(1)
    slot = tile % 2
    part = tm // nf

    def row_copy(t, r, slot_):
        return pltpu.make_async_copy(x_hbm.at[pl.ds(t, 1)], xbuf.at[slot_, pl.ds(r, 1)], sem.at[slot_])

    def wait_tile(slot_):
        pltpu.make_async_copy(x_hbm.at[pl.ds(0, tm)], xbuf.at[slot_], sem.at[slot_]).wait()

    @pl.when((tile == 0) & (f == 0))
    def _():
        def issue(r, c):
            row_copy(idx_ref[0, 0, r], r, 0).start()
            return c

        lax.fori_loop(0, tm, issue, 0)

    @pl.when(f == 0)
    def _():
        wait_tile(slot)
        xb[...] = xbuf[slot].astype(BF)
        acc[...] = jnp.zeros_like(acc)

    base = f * part
    for j in range(part):
        row_copy(idxn_ref[0, 0, base + j], base + j, 1 - slot).start()

    x = xb[...]
    g = _dot(x, wg_ref[...])
    u = _dot(x, wu_ref[...])
    hcur = (g * jax.nn.sigmoid(g)) * u
    acc[...] += _dot(hcur.astype(BF), wd_ref[...])

    @pl.when(f == nf - 1)
    def _():
        o_ref[...] = acc[...].astype(o_ref.dtype)

    @pl.when((tile == ntiles - 1) & (f == nf - 1))
    def _():
        wait_tile(1 - slot)


def _ffn(x, idx, wg, wu, wd, layer, cap, tm=512, fc=1408):
    T, D = x.shape
    E = wg.shape[1]
    FF = wg.shape[3]
    tm = min(tm, cap)
    nt = cap // tm
    nf = FF // fc
    idx3 = idx[:, :cap // LANES, :].reshape(E * nt, 1, tm)
    last = E * nt - 1
    return pl.pallas_call(
        functools.partial(_ffn_kernel, tm=tm, nf=nf, ntiles=E * nt),
        out_shape=jax.ShapeDtypeStruct((E, cap, D), BF), grid=(E, nt, nf),
        in_specs=[pl.BlockSpec((1, 1, tm), lambda e, i, f: (e * nt + i, 0, 0), memory_space=pltpu.SMEM),
                  pl.BlockSpec((1, 1, tm), lambda e, i, f: (jnp.minimum(e * nt + i + 1, last), 0, 0),
                               memory_space=pltpu.SMEM),
                  pl.BlockSpec(memory_space=pl.ANY),
                  pl.BlockSpec((None, None, D, fc), lambda e, i, f: (layer, e, 0, f)),
                  pl.BlockSpec((None, None, D, fc), lambda e, i, f: (layer, e, 0, f)),
                  pl.BlockSpec((None, None, fc, D), lambda e, i, f: (layer, e, f, 0))],
        out_specs=pl.BlockSpec((None, tm, D), lambda e, i, f: (e, i, 0)),
        scratch_shapes=[pltpu.VMEM((2, tm, D), F32), pltpu.VMEM((tm, D), BF), pltpu.VMEM((tm, D), F32),
                        pltpu.SemaphoreType.DMA((2,))],
        compiler_params=_cp(("arbitrary", "arbitrary", "arbitrary")), name="expert_ffn")(idx3, idx3, x, wg, wu, wd)


COMBINE_WIN = 128


def _combine_kernel(off_ref, rank_ref, aff_ref, x_ref, g_ref, b_ref, out_hbm, o_ref, wins, winx, ysc, sem, semx,
                    *, tt, E, cap, rpt, ntile, alpha):
    W = COMBINE_WIN
    ti = pl.program_id(0)
    slot = ti % 2

    def start0(e, ti_):
        r0 = off_ref[e, ti_ * rpt]
        return pl.multiple_of(jnp.minimum((r0 // 16) * 16, cap - W), 16)

    def win_copy(e, ti_, slot_):
        return pltpu.make_async_copy(out_hbm.at[e, pl.ds(start0(e, ti_), W)], wins.at[slot_, pl.ds(e * W, W)],
                                     sem.at[slot_])

    @pl.when(ti == 0)
    def _():
        for e in range(E):
            win_copy(e, 0, 0).start()

    @pl.when(ti + 1 < ntile)
    def _():
        for e in range(E):
            win_copy(e, ti + 1, 1 - slot).start()

    rank = rank_ref[...]
    aff = aff_ref[...]
    lane = lax.broadcasted_iota(I32, (tt, W), 1)
    hi, lo = [], []
    for e in range(E):
        r_e = rank[:, e:e + 1]
        g_e = aff[:, e:e + 1]
        g_hi = g_e.astype(BF).astype(F32)
        match = (lane == r_e - start0(e, ti)) & (r_e >= 0)
        hi.append(jnp.where(match, g_hi, 0.0).astype(BF))
        lo.append(jnp.where(match, g_e - g_hi, 0.0).astype(BF))
    p = jnp.concatenate([jnp.concatenate(hi, axis=1), jnp.concatenate(lo, axis=1)], axis=0)
    for e in range(E):
        win_copy(e, ti, slot).wait()
    y2 = _dot(p, wins[slot])
    ysc[...] = alpha * x_ref[...] + (y2[:tt] + y2[tt:])

    def extra(e, c):
        s0 = start0(e, ti)
        r_end = off_ref[e, (ti + 1) * rpt]

        @pl.when(r_end > s0 + W)
        def _():
            lane_e = lax.broadcasted_iota(I32, (tt, E), 1)
            r_e = jnp.sum(jnp.where(lane_e == e, rank, 0).astype(F32), axis=1, keepdims=True).astype(I32)
            g_e = jnp.sum(jnp.where(lane_e == e, aff, 0.0), axis=1, keepdims=True)
            for w in range(1, tt // W + 1):
                lo_w = s0 + w * W

                @pl.when(r_end > lo_w)
                def _():
                    sw = pl.multiple_of(jnp.minimum(lo_w, cap - W), 16)
                    cp = pltpu.make_async_copy(out_hbm.at[e, pl.ds(sw, W)], winx, semx)
                    cp.start()
                    cp.wait()
                    onehot = jnp.where((lane == r_e - sw) & (r_e >= lo_w), 1.0, 0.0).astype(BF)
                    ysc[...] += g_e * _dot(onehot, winx[...])

        return c

    lax.fori_loop(0, E, extra, 0)
    o_ref[...] = _ln_rows(ysc[...], g_ref[...], b_ref[...])


def _combine_ln(out, rank_t, aff_t, rowoff, x, g, b, cap, alpha):
    T, D = x.shape
    E = out.shape[0]
    W = COMBINE_WIN
    assert cap >= W
    tt = min(256, cap // 2)
    rpt = tt // LANES
    ntile = T // tt
    kern = functools.partial(_combine_kernel, tt=tt, E=E, cap=cap, rpt=rpt, ntile=ntile, alpha=alpha)
    return pl.pallas_call(
        kern, out_shape=jax.ShapeDtypeStruct((T, D), F32),
        grid_spec=pltpu.PrefetchScalarGridSpec(
            num_scalar_prefetch=1, grid=(ntile,),
            in_specs=[pl.BlockSpec((tt, E), lambda i, off: (i, 0)),
                      pl.BlockSpec((tt, E), lambda i, off: (i, 0)),
                      pl.BlockSpec((tt, D), lambda i, off: (i, 0)),
                      pl.BlockSpec((1, D), lambda i, off: (0, 0)),
                      pl.BlockSpec((1, D), lambda i, off: (0, 0)),
                      pl.BlockSpec(memory_space=pl.ANY)],
            out_specs=pl.BlockSpec((tt, D), lambda i, off: (i, 0)),
            scratch_shapes=[pltpu.VMEM((2, E * W, D), BF), pltpu.VMEM((W, D), BF), pltpu.VMEM((tt, D), F32),
                            pltpu.SemaphoreType.DMA((2,)), pltpu.SemaphoreType.DMA]),
        compiler_params=_cp(("arbitrary",)), name="combine_ln",
    )(rowoff, rank_t, aff_t, x, g.reshape(1, D), b.reshape(1, D), out)


def _expert_choice_ln(x, w_router_t, wg, wu, wd, layer, g, b, alpha):
    T, D = x.shape
    E = N_EXPERTS
    cap = max(1, EC_FACTOR * T // E)
    aff = _router(x, w_router_t)
    rank, rowoff = _select(aff, cap)
    rowoff = jnp.concatenate([rowoff[:, :, 0], jnp.full((E, 1), cap, I32)], axis=1)
    idx = _compact(rank, rowoff, cap)
    out = _ffn(x, idx, wg, wu, wd, layer, cap)
    return _combine_ln(out, rank.reshape(E, T).T, aff.T, rowoff, x, g, b, cap, alpha)


def _trunk(x, p, depth):
    B, L, D = x.shape
    T = B * L
    alpha = (2 * depth) ** 0.25
    x = x.reshape(T, D)
    filt_cache = {}
    for layer in range(depth):
        j = layer // 2
        if layer % 2 == 0:
            hproj = _xw(x, p["ab_w_in"][j], name="ab_in").reshape(B, L, -1)
            lambda_init = 0.8 - 0.6 * math.exp(-0.3 * layer)
            lf = p["diff_lambda"][j].astype(F32)
            lam_full = jnp.exp(jnp.sum(lf[0] * lf[1])) - jnp.exp(jnp.sum(lf[2] * lf[3])) + lambda_init
            a_out = _diff_attention(hproj, lam_full, p["diff_subln_g"][j], p["rel_bias"], lambda_init, T=512)
            uc = _short_conv(hproj, p["hy_conv_w"][j], p["hy_conv_b"][j])
            filt, filt_sum = _filter_mlp(L, p["hy_f_w1"][j], p["hy_f_b1"][j], p["hy_f_w2"][j], p["hy_f_b2"][j],
                                         p["hy_f_w3"][j], p["hy_f_freq"][j], p["hy_decay"][j])
            b_out = _hyena(uc, filt, filt_sum, p["hy_skip"][j])
            w_out = p["ab_w_out"][j]
            pairs = [(a_out.reshape(T, D_A), w_out[:D_A]), (b_out.reshape(T, D_B), w_out[D_A:])]
        else:
            pairs = [(_fourier_mix(x, B, L), p["c_w_out"][j])]
        x = _mm_res_ln(pairs, x, p["ln_g"][layer, 0], p["ln_b"][layer, 0], alpha)
        x = _expert_choice_ln(x, p["ec_router_t"][layer], p["ec_w_gate"], p["ec_w_up"], p["ec_w_down"], layer,
                              p["ln_g"][layer, 1], p["ln_b"][layer, 1], alpha)
    return x.reshape(B, L, D)


def kernel(x_prompt, x_sample, rel_bias, ab_w_in, ab_w_out, diff_lambda, diff_subln_g, hy_conv_w, hy_conv_b,
           hy_f_w1, hy_f_b1, hy_f_w2, hy_f_b2, hy_f_w3, hy_f_freq, hy_decay, hy_skip,
           c_w_out, ec_router, ec_w_gate, ec_w_up, ec_w_down, ln_g, ln_b):
    depth = ec_router.shape[0]
    p = dict(
        rel_bias=rel_bias, ab_w_in=ab_w_in.astype(BF), ab_w_out=ab_w_out.astype(BF), diff_lambda=diff_lambda,
        diff_subln_g=diff_subln_g, hy_conv_w=hy_conv_w, hy_conv_b=hy_conv_b, hy_f_w1=hy_f_w1, hy_f_b1=hy_f_b1,
        hy_f_w2=hy_f_w2, hy_f_b2=hy_f_b2, hy_f_w3=hy_f_w3, hy_f_freq=hy_f_freq, hy_decay=hy_decay, hy_skip=hy_skip,
        c_w_out=c_w_out.astype(BF), ec_router_t=jnp.swapaxes(ec_router, 1, 2).astype(BF),
        ec_w_gate=ec_w_gate.astype(BF), ec_w_up=ec_w_up.astype(BF), ec_w_down=ec_w_down.astype(BF),
        ln_g=ln_g.astype(F32), ln_b=ln_b.astype(F32))
    return _trunk(x_prompt, p, depth), _trunk(x_sample, p, depth)
```

```python
import functools
import math

import jax
import jax.numpy as jnp
from jax import lax
from jax.experimental import pallas as pl
from jax.experimental.pallas import tpu as pltpu

BF = jnp.bfloat16
F32 = jnp.float32
I32 = jnp.int32

A_HEADS = 4
A_DV = 128
A_DK = 64
D_A = 512
D_B = 512
B_ORDER = 2
REL_BUCKETS = 32
REL_MAX_DIST = 128
FILTER_BANDS = 16
N_EXPERTS = 16
EC_FACTOR = 2
LN_EPS = 1e-5
C_GROUPS = 4
LANES = 128
FFT_N2 = 128
VMEM_LIMIT = 52 * 1024 * 1024


def _cp(sem, vmem=VMEM_LIMIT):
    return pltpu.CompilerParams(dimension_semantics=sem, vmem_limit_bytes=vmem)


def _dot(a, b):
    return jnp.dot(a, b, preferred_element_type=F32)


def _dot_nt(a, b):
    return lax.dot_general(a, b, (((1,), (1,)), ((), ())), preferred_element_type=F32)


def _mm_kernel(a_ref, b_ref, o_ref, *scratch, nk, kaxis, scale):
    prod = _dot(a_ref[...].astype(BF), b_ref[...].astype(BF))
    if nk == 1:
        o_ref[...] = (prod * scale).astype(o_ref.dtype)
        return
    acc = scratch[0]
    k = pl.program_id(kaxis)

    @pl.when(k == 0)
    def _():
        acc[...] = jnp.zeros_like(acc)

    acc[...] += prod

    @pl.when(k == nk - 1)
    def _():
        o_ref[...] = (acc[...] * scale).astype(o_ref.dtype)


def _mm(a, b, out_shape, grid, a_spec, b_spec, o_spec, mn, *, nk=1, kaxis=None, scale=1.0, name):
    sem = tuple("arbitrary" if i == kaxis else "parallel" for i in range(len(grid)))
    return pl.pallas_call(
        functools.partial(_mm_kernel, nk=nk, kaxis=kaxis, scale=scale),
        out_shape=out_shape, grid=grid, in_specs=[a_spec, b_spec], out_specs=o_spec,
        scratch_shapes=[pltpu.VMEM(mn, F32)] if nk > 1 else [],
        compiler_params=_cp(sem), name=name)(a, b)


def _xw(x, w, tm=512, name="xw"):
    M, K = x.shape
    N = w.shape[1]
    tm = min(tm, M)
    return _mm(x, w, jax.ShapeDtypeStruct((M, N), F32), (M // tm,),
               pl.BlockSpec((tm, K), lambda i: (i, 0)), pl.BlockSpec((K, N), lambda i: (0, 0)),
               pl.BlockSpec((tm, N), lambda i: (i, 0)), (tm, N), name=name)


def _ln_rows(y, g, b):
    mu = jnp.mean(y, -1, keepdims=True)
    d = y - mu
    var = jnp.mean(d * d, -1, keepdims=True)
    return d * lax.rsqrt(var + LN_EPS) * g + b


def _mm_res_ln_kernel(*refs, npairs, alpha):
    x_ref, g_ref, b_ref = refs[2 * npairs:2 * npairs + 3]
    o_ref = refs[2 * npairs + 3]
    y = alpha * x_ref[...]
    for i in range(npairs):
        y = y + _dot(refs[2 * i][...].astype(BF), refs[2 * i + 1][...])
    o_ref[...] = _ln_rows(y, g_ref[...], b_ref[...])


def _mm_res_ln(pairs, x, g, b, alpha, tm=512, name="mm_res_ln"):
    M, D = x.shape
    tm = min(tm, M)
    specs, args = [], []
    for a, w in pairs:
        specs += [pl.BlockSpec((tm, a.shape[1]), lambda i: (i, 0)), pl.BlockSpec(w.shape, lambda i: (0, 0))]
        args += [a, w]
    specs += [pl.BlockSpec((tm, D), lambda i: (i, 0)), pl.BlockSpec((1, D), lambda i: (0, 0)),
              pl.BlockSpec((1, D), lambda i: (0, 0))]
    args += [x, g.reshape(1, D), b.reshape(1, D)]
    return pl.pallas_call(
        functools.partial(_mm_res_ln_kernel, npairs=len(pairs), alpha=alpha),
        out_shape=jax.ShapeDtypeStruct((M, D), F32), grid=(M // tm,), in_specs=specs,
        out_specs=pl.BlockSpec((tm, D), lambda i: (i, 0)), compiler_params=_cp(("parallel",)), name=name)(*args)


def _rel_bucket(rel):
    nb = REL_BUCKETS // 2
    max_exact = nb // 2
    n = jnp.abs(rel)
    large = max_exact + (jnp.log(jnp.maximum(n, 1).astype(F32) / max_exact)
                         / math.log(REL_MAX_DIST / max_exact) * (nb - max_exact)).astype(I32)
    large = jnp.minimum(large, nb - 1)
    return jnp.where(rel > 0, nb, 0) + jnp.where(n < max_exact, n, large)


LOG2E = 1.4426950408889634


def _bias_tiles(rel_table, T):
    assert T >= REL_MAX_DIST
    i = jnp.arange(T)
    rel = (jnp.arange(-2, 3)[:, None, None] * T + i[None, :, None]) - i[None, None, :]
    bucket = _rel_bucket(rel)[None]
    tab = LOG2E * rel_table.astype(F32).T[:, :, None, None, None]
    out = jnp.zeros((rel_table.shape[1],) + rel.shape, F32)
    for b in range(REL_BUCKETS):
        out = jnp.where(bucket == b, tab[:, b], out)
    return out


ATTN_MAX_DRIFT = 40.0
ATTN_CHUNK_GROUP = 3


def _attn_kernel(lam_ref, q_ref, k_ref, v_ref, bias_ref, g_ref, o_ref, kb, vt, m_s, l_s, a_s, *, T, nk, out_scale):
    qi = pl.program_id(2)
    tq = q_ref.shape[0]

    @pl.when(qi == 0)
    def _():
        def prep(c, carry):
            st = pl.multiple_of(c * T, T)
            kb[c] = k_ref[pl.ds(st, T), :].astype(BF)
            vt[c] = v_ref[pl.ds(st, T), :].T.astype(BF)
            return carry

        lax.fori_loop(0, nk, prep, 0)

    q = q_ref[...] * (A_DK ** -0.5 * LOG2E)
    lane = lax.broadcasted_iota(I32, q.shape, 1)
    qs = (jnp.where(lane < A_DK, q, 0.0).astype(BF), jnp.where(lane >= A_DK, q, 0.0).astype(BF))

    def scores(kj, mi):
        return _dot_nt(kb[kj], qs[mi]) + bias_ref[jnp.clip(kj - qi, -2, 2) + 2]

    def rescaling_chunk(kj, first):
        for mi in range(2):
            s = scores(kj, mi)
            m_new = jnp.max(s, axis=0, keepdims=True)
            if not first:
                m_new = jnp.maximum(m_s[mi], m_new)
            p = jnp.exp2(s - m_new)
            lsum = jnp.sum(p, axis=0, keepdims=True)
            pv = _dot(vt[kj], p.astype(BF))
            if first:
                l_s[mi] = lsum
                a_s[mi] = pv
            else:
                alpha = jnp.exp2(m_s[mi] - m_new)
                l_s[mi] = alpha * l_s[mi] + lsum
                a_s[mi] = alpha * a_s[mi] + pv
            m_s[mi] = m_new

    rescaling_chunk(0, True)

    def lazy_chunks(kjs):
        lsum, pv, drift = [None, None], [None, None], None
        for kj in kjs:
            for mi in range(2):
                s = scores(kj, mi)
                m_ref = m_s[mi]
                p = jnp.exp2(s - m_ref)
                ls = jnp.sum(p, axis=0, keepdims=True)
                acc = _dot(vt[kj], p.astype(BF))
                lsum[mi] = ls if lsum[mi] is None else lsum[mi] + ls
                pv[mi] = acc if pv[mi] is None else pv[mi] + acc
                d = jnp.max(s, axis=0, keepdims=True) - m_ref
                drift = d if drift is None else jnp.maximum(drift, d)
        over = jnp.max(drift) > ATTN_MAX_DRIFT

        @pl.when(jnp.logical_not(over))
        def _():
            for mi in range(2):
                l_s[mi] += lsum[mi]
                a_s[mi] += pv[mi]

        @pl.when(over)
        def _():
            for kj in kjs:
                rescaling_chunk(kj, False)

    group = ATTN_CHUNK_GROUP

    def grouped(i, c):
        lazy_chunks([1 + group * i + u for u in range(group)])
        return c

    ngroups = (nk - 1) // group
    lax.fori_loop(0, ngroups, grouped, 0)
    if (nk - 1) % group:
        lazy_chunks(list(range(1 + group * ngroups, nk)))
    o = a_s[0] / l_s[0] - lam_ref[0] * (a_s[1] / l_s[1])
    o = o * lax.rsqrt(jnp.mean(o * o, 0, keepdims=True) + LN_EPS) * g_ref[...] * out_scale
    o_ref[...] = o.T


def _diff_attention(hproj, lam_full, sub_g, rel_table, lambda_init, T):
    B, L, _ = hproj.shape
    T = min(T, L)
    bias = _bias_tiles(rel_table, T)
    H = A_HEADS
    return pl.pallas_call(
        functools.partial(_attn_kernel, T=T, nk=L // T, out_scale=1.0 - lambda_init),
        out_shape=jax.ShapeDtypeStruct((B, L, D_A), F32), grid=(B, H, L // T),
        in_specs=[pl.BlockSpec(memory_space=pltpu.SMEM),
                  pl.BlockSpec((None, T, LANES), lambda b, h, i: (b, i, h)),
                  pl.BlockSpec((None, L, LANES), lambda b, h, i: (b, 0, H + h)),
                  pl.BlockSpec((None, L, LANES), lambda b, h, i: (b, 0, 2 * H + h)),
                  pl.BlockSpec((None, 5, T, T), lambda b, h, i: (h, 0, 0, 0)),
                  pl.BlockSpec((A_DV, 1), lambda b, h, i: (0, 0))],
        out_specs=pl.BlockSpec((None, T, LANES), lambda b, h, i: (b, i, h)),
        scratch_shapes=[pltpu.VMEM((L // T, T, LANES), BF), pltpu.VMEM((L // T, LANES, T), BF),
                        pltpu.VMEM((2, 1, T), F32), pltpu.VMEM((2, 1, T), F32), pltpu.VMEM((2, A_DV, T), F32)],
        compiler_params=_cp(("parallel", "parallel", "arbitrary")), name="diff_attn",
    )(lam_full.reshape(1), hproj, hproj, hproj, bias, sub_g.astype(F32).reshape(A_DV, 1))


def _short_conv_kernel(u_ref, w_ref, b_ref, o_ref):
    u = u_ref[...]
    L = u.shape[0]
    row = lax.broadcasted_iota(I32, u.shape, 0)
    up = jnp.where(row == 0, 0.0, pltpu.roll(u, 1, 0))
    dn = jnp.where(row == L - 1, 0.0, pltpu.roll(u, L - 1, 0))
    w = w_ref[...]
    o_ref[...] = up * w[0:1] + u * w[1:2] + dn * w[2:3] + b_ref[...]


def _short_conv(hproj, conv_w, conv_b):
    B, L, _ = hproj.shape
    nblk = 3 * D_B // LANES
    off = 3 * D_A // LANES
    per = D_B // LANES
    return pl.pallas_call(
        _short_conv_kernel, out_shape=jax.ShapeDtypeStruct((3, B, L, D_B), F32), grid=(B, nblk),
        in_specs=[pl.BlockSpec((None, L, LANES), lambda b, j: (b, 0, off + j)),
                  pl.BlockSpec((3, LANES), lambda b, j: (0, j)),
                  pl.BlockSpec((1, LANES), lambda b, j: (0, j))],
        out_specs=pl.BlockSpec((None, None, L, LANES), lambda b, j: (j // per, b, 0, j % per)),
        compiler_params=_cp(("parallel", "parallel")), name="short_conv",
    )(hproj, conv_w.astype(F32), conv_b.astype(F32).reshape(1, -1))


def _filter_mlp_kernel(z_ref, w1_ref, b1_ref, w2_ref, b2_ref, w3_ref, fq_ref, dec_ref, h_ref, s_ref):
    i = pl.program_id(0)
    z = z_ref[...]
    fq = fq_ref[...]
    h = jnp.sin(fq[0:1] * (_dot(z.astype(BF), w1_ref[...]) + b1_ref[...]))
    h = jnp.sin(fq[1:2] * (_dot(h.astype(BF), w2_ref[...]) + b2_ref[...]))
    t = z[:, 0:1]
    h = _dot(h.astype(BF), w3_ref[...]) * jnp.exp(-t * jnp.abs(dec_ref[...]))

    @pl.when(i == 0)
    def _():
        s_ref[...] = jnp.zeros_like(s_ref)

    s_ref[...] += jnp.sum(jnp.abs(h), axis=0, keepdims=True)
    row = lax.broadcasted_iota(I32, h.shape, 0)
    col = lax.broadcasted_iota(I32, h.shape, 1)
    half = h.shape[1] // 2
    h_ref[...] = jnp.where((row + i * h.shape[0] == 0) & (col >= half), 0.0, h)


def _filter_mlp(L, w1, b1, w2, b2, w3, freq, decay, tl=512):
    t = jnp.linspace(0.0, 1.0, L, dtype=F32)[:, None]
    wpos = 2.0 * math.pi * jnp.arange(L, dtype=F32)[:, None] / L
    fr = jnp.linspace(1e-4, FILTER_BANDS - 1, FILTER_BANDS, dtype=F32)[None, :]
    z = jnp.concatenate([t, jnp.cos(fr * wpos), -jnp.sin(fr * wpos)], -1)
    emb, width = w1.shape
    ch = w3.shape[1]
    z = jnp.pad(z, ((0, 0), (0, LANES - emb)))
    padw = LANES - width
    w1p = jnp.pad(w1, ((0, LANES - emb), (0, padw))).astype(BF)
    w2p = jnp.pad(w2, ((0, padw), (0, padw))).astype(BF)
    w3p = jnp.pad(w3, ((0, padw), (0, 0))).astype(BF)
    b1p = jnp.pad(b1.astype(F32), (0, padw)).reshape(1, LANES)
    b2p = jnp.pad(b2.astype(F32), (0, padw)).reshape(1, LANES)
    fqp = jnp.pad(freq.astype(F32), ((0, 0), (0, padw)))
    tl = min(tl, L)
    full = lambda shp: pl.BlockSpec(shp, lambda i: (0, 0))
    return pl.pallas_call(
        _filter_mlp_kernel,
        out_shape=(jax.ShapeDtypeStruct((L, ch), F32), jax.ShapeDtypeStruct((1, ch), F32)), grid=(L // tl,),
        in_specs=[pl.BlockSpec((tl, LANES), lambda i: (i, 0)), full((LANES, LANES)), full((1, LANES)),
                  full((LANES, LANES)), full((1, LANES)), full((LANES, ch)), full((2, LANES)), full((1, ch))],
        out_specs=(pl.BlockSpec((tl, ch), lambda i: (i, 0)), full((1, ch))),
        compiler_params=_cp(("arbitrary",)), name="filter_mlp",
    )(z, w1p, b1p, w2p, b2p, w3p, fqp, decay.astype(F32).reshape(1, ch))


def _cis(idx, n):
    ang = (2.0 * math.pi / n) * (idx % n).astype(F32)
    return jnp.cos(ang), -jnp.sin(ang)


def _blk(re, im):
    return jnp.concatenate([jnp.concatenate([re, -im], -1), jnp.concatenate([im, re], -1)], -2)


def _dft_tables(N):
    N2 = FFT_N2
    N1 = N // N2
    a = jnp.arange(N1)
    f1r, f1i = _cis(a[:, None] * a[None, :], N1)
    k = a[:, None, None] + N1 * jnp.arange(N2)[None, :, None]
    gr, gi = _cis(k * jnp.arange(N2)[None, None, :], N)
    return N1, f1r, f1i, gr, gi


def _mid_kernel(g_ref, gi_ref, kf_ref, a_ref, o_ref):
    n2, c = a_ref.shape[1], a_ref.shape[2]
    x = _dot(g_ref[...], a_ref[...].reshape(2 * n2, c).astype(BF))
    xr, xi = x[:n2], x[n2:]
    kr, ki = kf_ref[0], kf_ref[1]
    y = jnp.concatenate([xr * kr - xi * ki, xr * ki + xi * kr], axis=0).astype(BF)
    o_ref[...] = _dot(gi_ref[...], y).reshape(2, n2, c).astype(o_ref.dtype)


def _filt_mid_kernel(g_ref, a_ref, s_ref, o_ref):
    n2 = a_ref.shape[1]
    c = a_ref.shape[2] // 2
    x = _dot(g_ref[...], a_ref[...].reshape(2 * n2, 2 * c).astype(BF))
    s = s_ref[...]
    inv = 1.0 / (s[:, :c] + s[:, c:])
    o_ref[0] = (x[:n2, :c] + x[:n2, c:]) * inv
    o_ref[1] = (x[n2:, :c] - x[n2:, c:]) * inv


def _gate_inv_kernel(m_ref, d_ref, z_ref, gate_ref, sk_ref, o_ref, *, scale):
    conv = _dot(m_ref[...], d_ref[...].astype(BF)) * scale
    o_ref[...] = gate_ref[...] * (conv + z_ref[...] * sk_ref[...])


def _hyena(uc, filt, filt_sum, skip):
    _, B, L, C = uc.shape
    assert B % 2 == 0
    P = B // 2
    N = 2 * L
    N2 = FFT_N2
    N1, f1r, f1i, gr, gi = _dft_tables(N)
    h = N1 // 2
    cols = N2 * C
    m1 = jnp.concatenate([jnp.concatenate([f1r[:, :h], -f1i[:, :h]], 1),
                          jnp.concatenate([f1i[:, :h], f1r[:, :h]], 1)], 0).astype(BF)
    m1_real = jnp.concatenate([f1r[:, :h], f1i[:, :h]], 0).astype(BF)
    m1_inv = jnp.concatenate([jnp.concatenate([f1r[:h], f1i[:h]], 1),
                              jnp.concatenate([-f1i[:h], f1r[:h]], 1)], 0).astype(BF)
    g_fwd = _blk(gr, gi).astype(BF)
    g_inv = _blk(jnp.swapaxes(gr, 1, 2), -jnp.swapaxes(gi, 1, 2)).astype(BF)

    tc = min(cols, 4096)
    CF = filt.shape[1]
    fcols = N2 * CF
    tcf = min(fcols, 8192)
    fa = _mm(m1_real, filt.reshape(h, fcols), jax.ShapeDtypeStruct((2 * N1, fcols), BF), (fcols // tcf,),
             pl.BlockSpec((2 * N1, h), lambda j: (0, 0)), pl.BlockSpec((h, tcf), lambda j: (0, j)),
             pl.BlockSpec((2 * N1, tcf), lambda j: (0, j)), None, name="filt_s1")
    kf = pl.pallas_call(
        _filt_mid_kernel, out_shape=jax.ShapeDtypeStruct((N1, 2, N2, CF // 2), F32), grid=(N1,),
        in_specs=[pl.BlockSpec((None, 2 * N2, 2 * N2), lambda k: (k, 0, 0)),
                  pl.BlockSpec((2, None, N2, CF), lambda k: (0, k, 0, 0)),
                  pl.BlockSpec((1, CF), lambda k: (0, 0))],
        out_specs=pl.BlockSpec((None, 2, N2, CF // 2), lambda k: (k, 0, 0, 0)),
        compiler_params=_cp(("parallel",)), name="filt_mid",
    )(g_fwd, fa.reshape(2, N1, N2, CF), filt_sum)

    sk = jnp.tile(skip.astype(F32), (1, N2))
    z = uc[0].reshape(P, N1, cols)
    for n in range(B_ORDER):
        gate = uc[1 + n].reshape(P, N1, cols)
        a = _mm(m1, z, jax.ShapeDtypeStruct((P, 2 * N1, cols), BF), (P, cols // tc),
                pl.BlockSpec((2 * N1, N1), lambda p, j: (0, 0)), pl.BlockSpec((None, N1, tc), lambda p, j: (p, 0, j)),
                pl.BlockSpec((None, 2 * N1, tc), lambda p, j: (p, 0, j)), None, name="hy_s1")
        d = pl.pallas_call(
            _mid_kernel, out_shape=jax.ShapeDtypeStruct((P, 2, N1, N2, C), BF), grid=(N1, P),
            in_specs=[pl.BlockSpec((None, 2 * N2, 2 * N2), lambda k, p: (k, 0, 0)),
                      pl.BlockSpec((None, 2 * N2, 2 * N2), lambda k, p: (k, 0, 0)),
                      pl.BlockSpec((None, 2, N2, C), lambda k, p: (k, 0, 0, n)),
                      pl.BlockSpec((None, 2, None, N2, C), lambda k, p: (p, 0, k, 0, 0))],
            out_specs=pl.BlockSpec((None, 2, None, N2, C), lambda k, p: (p, 0, k, 0, 0)),
            compiler_params=_cp(("parallel", "parallel")), name="hy_mid",
        )(g_fwd, g_inv, kf, a.reshape(P, 2, N1, N2, C))
        z = pl.pallas_call(
            functools.partial(_gate_inv_kernel, scale=1.0 / N),
            out_shape=jax.ShapeDtypeStruct((P, N1, cols), F32), grid=(P, cols // tc),
            in_specs=[pl.BlockSpec((N1, 2 * N1), lambda p, j: (0, 0)),
                      pl.BlockSpec((None, 2 * N1, tc), lambda p, j: (p, 0, j)),
                      pl.BlockSpec((None, N1, tc), lambda p, j: (p, 0, j)),
                      pl.BlockSpec((None, N1, tc), lambda p, j: (p, 0, j)),
                      pl.BlockSpec((1, tc), lambda p, j: (0, j))],
            out_specs=pl.BlockSpec((None, N1, tc), lambda p, j: (p, 0, j)),
            compiler_params=_cp(("parallel", "parallel")), name="hy_inv_gate",
        )(m1_inv, d.reshape(P, 2 * N1, cols), z, gate, sk[n:n + 1])
    return z.reshape(B, L, C)


def _fnet_tables(L, D):
    gw = D // C_GROUPS
    c = jnp.arange(gw)
    cr, ci = _cis(c[:, None] * c[None, :], gw)
    eye = jnp.eye(C_GROUPS, dtype=F32)
    wch = jnp.concatenate([jnp.kron(eye, cr), jnp.kron(eye, ci)], 1).astype(BF)
    N1, f1r, f1i, gr, gi = _dft_tables(L)
    m1 = _blk(f1r, f1i).astype(BF)
    g_re = jnp.concatenate([gr, -gi], -1).astype(BF)
    return N1, wch, m1, g_re


def _fourier_mix(x, B, L):
    T, D = x.shape
    N2 = FFT_N2
    N1, wch, m1, g_re = _fnet_tables(L, D)
    tm = min(512, T)
    y = _mm(x, wch, jax.ShapeDtypeStruct((2, T, D), BF), (T // tm, 2),
            pl.BlockSpec((tm, D), lambda i, p: (i, 0)), pl.BlockSpec((D, D), lambda i, p: (0, p)),
            pl.BlockSpec((None, tm, D), lambda i, p: (p, i, 0)), None, name="fnet_ch")
    cols = N2 * D
    tc = min(cols, 8192)
    a = _mm(jnp.stack([m1[:, :N1], m1[:, N1:]]), y.reshape(2, B, N1, cols), jax.ShapeDtypeStruct((B, 2 * N1, cols), BF), (B, cols // tc, 2),
            pl.BlockSpec((None, 2 * N1, N1), lambda b, j, p: (p, 0, 0)),
            pl.BlockSpec((None, None, N1, tc), lambda b, j, p: (p, b, 0, j)),
            pl.BlockSpec((None, 2 * N1, tc), lambda b, j, p: (b, 0, j)), (2 * N1, tc), nk=2, kaxis=2, name="fnet_s1")
    gw = D // C_GROUPS
    f = _mm(g_re, a.reshape(B, 2, N1, N2, D), jax.ShapeDtypeStruct((B, N2, N1 * D), F32), (B, N1, 2),
            pl.BlockSpec((None, N2, N2), lambda b, k, p: (k, 0, p)),
            pl.BlockSpec((None, None, None, N2, D), lambda b, k, p: (b, p, k, 0, 0)),
            pl.BlockSpec((None, N2, D), lambda b, k, p: (b, 0, k)), (N2, D), nk=2, kaxis=2,
            scale=1.0 / math.sqrt(L * gw), name="fnet_s2")
    return f.reshape(T, D)


def _router_kernel(w_ref, x_ref, o_ref):
    logits = _dot_nt(w_ref[...], x_ref[...].astype(BF))
    m = jnp.max(logits, axis=0, keepdims=True)
    p = jnp.exp(logits - m)
    o_ref[...] = p / jnp.sum(p, axis=0, keepdims=True)


def _router(x, w_router_t, tm=1024):
    T, D = x.shape
    E = w_router_t.shape[0]
    tm = min(tm, T)
    return pl.pallas_call(
        _router_kernel, out_shape=jax.ShapeDtypeStruct((E, T), F32), grid=(T // tm,),
        in_specs=[pl.BlockSpec((E, D), lambda i: (0, 0)), pl.BlockSpec((tm, D), lambda i: (i, 0))],
        out_specs=pl.BlockSpec((E, tm), lambda i: (0, i)), compiler_params=_cp(("parallel",)), name="router")(
            w_router_t, x)


def _prefix_counts(mask_f, upper, lower):
    rowcs = _dot(mask_f.astype(BF), upper)
    tot = jnp.broadcast_to(rowcs[:, LANES - 1:LANES], mask_f.shape)
    offs = _dot(lower, tot.astype(BF))
    return offs + rowcs - mask_f, offs


def _select_kernel(a_ref, rank_ref, off_ref, *, cap):
    v = a_ref[...]
    R = v.shape[0]
    bits = pltpu.bitcast(v, I32)

    def step(i, thr):
        cand = thr | (jnp.int32(1) << (30 - i))
        cnt = jnp.sum((bits >= cand).astype(F32))
        return jnp.where(cnt >= cap, cand, thr)

    thr = lax.fori_loop(0, 31, step, jnp.int32(0))
    gt = bits > thr
    eq = bits == thr
    need = cap - jnp.sum(gt.astype(F32))
    li = lax.broadcasted_iota(I32, (LANES, LANES), 0)
    lj = lax.broadcasted_iota(I32, (LANES, LANES), 1)
    upper = (li <= lj).astype(BF)
    ri = lax.broadcasted_iota(I32, (R, R), 0)
    rj = lax.broadcasted_iota(I32, (R, R), 1)
    lower = (rj < ri).astype(BF)
    eq_rank, _ = _prefix_counts(eq.astype(F32), upper, lower)
    sel = gt | (eq & (eq_rank < need))
    rank, offs = _prefix_counts(sel.astype(F32), upper, lower)
    rank_ref[...] = jnp.where(sel, rank.astype(I32), -1)
    off_ref[...] = offs.astype(I32)


def _select(aff, cap):
    E, T = aff.shape
    R = T // LANES
    spec = pl.BlockSpec((None, R, LANES), lambda e: (e, 0, 0))
    return pl.pallas_call(
        functools.partial(_select_kernel, cap=cap),
        out_shape=(jax.ShapeDtypeStruct((E, R, LANES), I32), jax.ShapeDtypeStruct((E, R, LANES), I32)),
        grid=(E,), in_specs=[spec], out_specs=(spec, spec), compiler_params=_cp(("parallel",)), name="select")(
            aff.reshape(E, R, LANES))


def _compact_kernel(off_ref, rank_ref, idx_ref, *, R):
    e = pl.program_id(0)
    idx_ref[...] = jnp.zeros_like(idx_ref)
    sub = lax.broadcasted_iota(I32, (8, LANES), 0)
    lane = lax.broadcasted_iota(I32, (8, LANES), 1)
    lhs = jnp.where(sub == 0, lane, jnp.where(sub == 1, 1, 0)).astype(BF)
    slot = lax.broadcasted_iota(I32, (2 * LANES, LANES), 0)

    def chunk(r, c):
        a = off_ref[e, r] // LANES
        local = rank_ref[pl.ds(r, 1), :] - a * LANES
        onehot = (slot == local).astype(BF)
        res = _dot_nt(lhs, onehot)
        val = (res[0:1] + res[1:2] * lax.convert_element_type(r * LANES, F32)).astype(I32)
        idx_ref[pl.ds(a, 1), :] += val[:, :LANES]
        idx_ref[pl.ds(a + 1, 1), :] += val[:, LANES:]
        return c

    lax.fori_loop(0, R, chunk, 0, unroll=8)


def _compact(rank, rowoff, cap):
    E, R, _ = rank.shape
    nrow = cap // LANES + 2
    return pl.pallas_call(
        functools.partial(_compact_kernel, R=R),
        out_shape=jax.ShapeDtypeStruct((E, nrow, LANES), I32),
        grid_spec=pltpu.PrefetchScalarGridSpec(
            num_scalar_prefetch=1, grid=(E,),
            in_specs=[pl.BlockSpec((None, R, LANES), lambda e, off: (e, 0, 0))],
            out_specs=pl.BlockSpec((None, nrow, LANES), lambda e, off: (e, 0, 0))),
        compiler_params=_cp(("arbitrary",)), name="compact")(rowoff, rank)


def _ffn_kernel(idx_ref, idxn_ref, x_hbm, wg_ref, wu_ref, wd_ref, o_ref, xbuf, xb, acc, sem, *, tm, nf, ntiles):
    f = pl.program_id(2)
    tile = pl.program_id(0) * pl.num_programs(1) + pl.program_id(1)
    slot = tile % 2
    part = tm // nf

    def row_copy(t, r, slot_):
        return pltpu.make_async_copy(x_hbm.at[pl.ds(t, 1)], xbuf.at[slot_, pl.ds(r, 1)], sem.at[slot_])

    def wait_tile(slot_):
        pltpu.make_async_copy(x_hbm.at[pl.ds(0, tm)], xbuf.at[slot_], sem.at[slot_]).wait()

    @pl.when((tile == 0) & (f == 0))
    def _():
        def issue(r, c):
            row_copy(idx_ref[0, 0, r], r, 0).start()
            return c

        lax.fori_loop(0, tm, issue, 0)

    @pl.when(f == 0)
    def _():
        wait_tile(slot)
        xb[...] = xbuf[slot].astype(BF)
        acc[...] = jnp.zeros_like(acc)

    base = f * part
    for j in range(part):
        row_copy(idxn_ref[0, 0, base + j], base + j, 1 - slot).start()

    x = xb[...]
    g = _dot(x, wg_ref[...])
    u = _dot(x, wu_ref[...])
    hcur = (g * jax.nn.sigmoid(g)) * u
    acc[...] += _dot(hcur.astype(BF), wd_ref[...])

    @pl.when(f == nf - 1)
    def _():
        o_ref[...] = acc[...].astype(o_ref.dtype)

    @pl.when((tile == ntiles - 1) & (f == nf - 1))
    def _():
        wait_tile(1 - slot)


def _ffn(x, idx, wg, wu, wd, layer, cap, tm=512, fc=1408):
    T, D = x.shape
    E = wg.shape[1]
    FF = wg.shape[3]
    tm = min(tm, cap)
    nt = cap // tm
    nf = FF // fc
    idx3 = idx[:, :cap // LANES, :].reshape(E * nt, 1, tm)
    last = E * nt - 1
    return pl.pallas_call(
        functools.partial(_ffn_kernel, tm=tm, nf=nf, ntiles=E * nt),
        out_shape=jax.ShapeDtypeStruct((E, cap, D), BF), grid=(E, nt, nf),
        in_specs=[pl.BlockSpec((1, 1, tm), lambda e, i, f: (e * nt + i, 0, 0), memory_space=pltpu.SMEM),
                  pl.BlockSpec((1, 1, tm), lambda e, i, f: (jnp.minimum(e * nt + i + 1, last), 0, 0),
                               memory_space=pltpu.SMEM),
                  pl.BlockSpec(memory_space=pl.ANY),
                  pl.BlockSpec((None, None, D, fc), lambda e, i, f: (layer, e, 0, f)),
                  pl.BlockSpec((None, None, D, fc), lambda e, i, f: (layer, e, 0, f)),
                  pl.BlockSpec((None, None, fc, D), lambda e, i, f: (layer, e, f, 0))],
        out_specs=pl.BlockSpec((None, tm, D), lambda e, i, f: (e, i, 0)),
        scratch_shapes=[pltpu.VMEM((2, tm, D), F32), pltpu.VMEM((tm, D), BF), pltpu.VMEM((tm, D), F32),
                        pltpu.SemaphoreType.DMA((2,))],
        compiler_params=_cp(("arbitrary", "arbitrary", "arbitrary")), name="expert_ffn")(idx3, idx3, x, wg, wu, wd)


COMBINE_WIN = 64


def _combine_kernel(off_ref, rank_ref, aff_ref, x_ref, g_ref, b_ref, out_hbm, o_ref, wins, winx, ysc, sem, semx,
                    *, tt, E, cap, rpt, ntile, alpha):
    W = COMBINE_WIN
    ti = pl.program_id(0)
    slot = ti % 2

    def start0(e, ti_):
        r0 = off_ref[e, ti_ * rpt]
        return pl.multiple_of(jnp.minimum((r0 // 16) * 16, cap - W), 16)

    def win_copy(e, ti_, slot_):
        return pltpu.make_async_copy(out_hbm.at[e, pl.ds(start0(e, ti_), W)], wins.at[slot_, pl.ds(e * W, W)],
                                     sem.at[slot_])

    @pl.when(ti == 0)
    def _():
        for e in range(E):
            win_copy(e, 0, 0).start()

    @pl.when(ti + 1 < ntile)
    def _():
        for e in range(E):
            win_copy(e, ti + 1, 1 - slot).start()

    sub = lax.broadcasted_iota(I32, (W, tt), 0)

    def gated_onehot(r_row, g_row, first_row, min_rank):
        g_hi = g_row.astype(BF).astype(F32)
        match = (sub == r_row - first_row) & (r_row >= min_rank)
        return jnp.concatenate([jnp.where(match, g_hi, 0.0), jnp.where(match, g_row - g_hi, 0.0)], axis=1).astype(BF)

    def scatter_rows(p_t, rows):
        y2 = lax.dot_general(p_t, rows, (((0,), (0,)), ((), ())), preferred_element_type=F32)
        return y2[:tt] + y2[tt:]

    p_t = jnp.concatenate([gated_onehot(rank_ref[e:e + 1, :], aff_ref[e:e + 1, :], start0(e, ti), 0)
                           for e in range(E)], axis=0)
    for e in range(E):
        win_copy(e, ti, slot).wait()
    ysc[...] = alpha * x_ref[...] + scatter_rows(p_t, wins[slot])

    def extra(e, c):
        s0 = start0(e, ti)
        r_end = off_ref[e, (ti + 1) * rpt]

        @pl.when(r_end > s0 + W)
        def _():
            r_row = rank_ref[pl.ds(e, 1), :]
            g_row = aff_ref[pl.ds(e, 1), :]
            for w in range(1, tt // W + 1):
                lo_w = s0 + w * W

                @pl.when(r_end > lo_w)
                def _():
                    sw = pl.multiple_of(jnp.minimum(lo_w, cap - W), 16)
                    cp = pltpu.make_async_copy(out_hbm.at[e, pl.ds(sw, W)], winx, semx)
                    cp.start()
                    cp.wait()
                    ysc[...] += scatter_rows(gated_onehot(r_row, g_row, sw, lo_w), winx[...])

        return c

    lax.fori_loop(0, E, extra, 0)
    o_ref[...] = _ln_rows(ysc[...], g_ref[...], b_ref[...])


def _combine_ln(out, rank, aff, rowoff, x, g, b, cap, alpha):
    T, D = x.shape
    E = out.shape[0]
    W = COMBINE_WIN
    assert cap >= W
    tt = min(256, cap // 2)
    rpt = tt // LANES
    ntile = T // tt
    kern = functools.partial(_combine_kernel, tt=tt, E=E, cap=cap, rpt=rpt, ntile=ntile, alpha=alpha)
    return pl.pallas_call(
        kern, out_shape=jax.ShapeDtypeStruct((T, D), F32),
        grid_spec=pltpu.PrefetchScalarGridSpec(
            num_scalar_prefetch=1, grid=(ntile,),
            in_specs=[pl.BlockSpec((E, tt), lambda i, off: (0, i)),
                      pl.BlockSpec((E, tt), lambda i, off: (0, i)),
                      pl.BlockSpec((tt, D), lambda i, off: (i, 0)),
                      pl.BlockSpec((1, D), lambda i, off: (0, 0)),
                      pl.BlockSpec((1, D), lambda i, off: (0, 0)),
                      pl.BlockSpec(memory_space=pl.ANY)],
            out_specs=pl.BlockSpec((tt, D), lambda i, off: (i, 0)),
            scratch_shapes=[pltpu.VMEM((2, E * W, D), BF), pltpu.VMEM((W, D), BF), pltpu.VMEM((tt, D), F32),
                            pltpu.SemaphoreType.DMA((2,)), pltpu.SemaphoreType.DMA]),
        compiler_params=_cp(("arbitrary",)), name="combine_ln",
    )(rowoff, rank, aff, x, g.reshape(1, D), b.reshape(1, D), out)


def _expert_choice_ln(x, w_router_t, wg, wu, wd, layer, g, b, alpha):
    T, D = x.shape
    E = N_EXPERTS
    cap = max(1, EC_FACTOR * T // E)
    aff = _router(x, w_router_t)
    rank, rowoff = _select(aff, cap)
    rowoff = jnp.concatenate([rowoff[:, :, 0], jnp.full((E, 1), cap, I32)], axis=1)
    idx = _compact(rank, rowoff, cap)
    out = _ffn(x, idx, wg, wu, wd, layer, cap)
    return _combine_ln(out, rank.reshape(E, T), aff, rowoff, x, g, b, cap, alpha)


def _trunk(x, p, depth):
    B, L, D = x.shape
    T = B * L
    alpha = (2 * depth) ** 0.25
    x = x.reshape(T, D)
    filt_cache = {}
    for layer in range(depth):
        j = layer // 2
        if layer % 2 == 0:
            hproj = _xw(x, p["ab_w_in"][j], name="ab_in").reshape(B, L, -1)
            lambda_init = 0.8 - 0.6 * math.exp(-0.3 * layer)
            lf = p["diff_lambda"][j].astype(F32)
            lam_full = jnp.exp(jnp.sum(lf[0] * lf[1])) - jnp.exp(jnp.sum(lf[2] * lf[3])) + lambda_init
            a_out = _diff_attention(hproj, lam_full, p["diff_subln_g"][j], p["rel_bias"], lambda_init, T=512)
            uc = _short_conv(hproj, p["hy_conv_w"][j], p["hy_conv_b"][j])
            filt, filt_sum = _filter_mlp(L, p["hy_f_w1"][j], p["hy_f_b1"][j], p["hy_f_w2"][j], p["hy_f_b2"][j],
                                         p["hy_f_w3"][j], p["hy_f_freq"][j], p["hy_decay"][j])
            b_out = _hyena(uc, filt, filt_sum, p["hy_skip"][j])
            w_out = p["ab_w_out"][j]
            pairs = [(a_out.reshape(T, D_A), w_out[:D_A]), (b_out.reshape(T, D_B), w_out[D_A:])]
        else:
            pairs = [(_fourier_mix(x, B, L), p["c_w_out"][j])]
        x = _mm_res_ln(pairs, x, p["ln_g"][layer, 0], p["ln_b"][layer, 0], alpha)
        x = _expert_choice_ln(x, p["ec_router_t"][layer], p["ec_w_gate"], p["ec_w_up"], p["ec_w_down"], layer,
                              p["ln_g"][layer, 1], p["ln_b"][layer, 1], alpha)
    return x.reshape(B, L, D)


def kernel(x_prompt, x_sample, rel_bias, ab_w_in, ab_w_out, diff_lambda, diff_subln_g, hy_conv_w, hy_conv_b,
           hy_f_w1, hy_f_b1, hy_f_w2, hy_f_b2, hy_f_w3, hy_f_freq, hy_decay, hy_skip,
           c_w_out, ec_router, ec_w_gate, ec_w_up, ec_w_down, ln_g, ln_b):
    depth = ec_router.shape[0]
    p = dict(
        rel_bias=rel_bias, ab_w_in=ab_w_in.astype(BF), ab_w_out=ab_w_out.astype(BF), diff_lambda=diff_lambda,
        diff_subln_g=diff_subln_g, hy_conv_w=hy_conv_w, hy_conv_b=hy_conv_b, hy_f_w1=hy_f_w1, hy_f_b1=hy_f_b1,
        hy_f_w2=hy_f_w2, hy_f_b2=hy_f_b2, hy_f_w3=hy_f_w3, hy_f_freq=hy_f_freq, hy_decay=hy_decay, hy_skip=hy_skip,
        c_w_out=c_w_out.astype(BF), ec_router_t=jnp.swapaxes(ec_router, 1, 2).astype(BF),
        ec_w_gate=ec_w_gate.astype(BF), ec_w_up=ec_w_up.astype(BF), ec_w_down=ec_w_down.astype(BF),
        ln_g=ln_g.astype(F32), ln_b=ln_b.astype(F32))
    return _trunk(x_prompt, p, depth), _trunk(x_sample, p, depth)
```

```python
import functools
import math

import jax
import jax.numpy as jnp
from jax import lax
from jax.experimental import pallas as pl
from jax.experimental.pallas import tpu as pltpu

BF = jnp.bfloat16
F32 = jnp.float32
I32 = jnp.int32

A_HEADS = 4
A_DV = 128
A_DK = 64
D_A = 512
D_B = 512
B_ORDER = 2
REL_BUCKETS = 32
REL_MAX_DIST = 128
FILTER_BANDS = 16
N_EXPERTS = 16
EC_FACTOR = 2
LN_EPS = 1e-5
C_GROUPS = 4
LANES = 128
FFT_N2 = 128
VMEM_LIMIT = 52 * 1024 * 1024


def _cp(sem, vmem=VMEM_LIMIT):
    return pltpu.CompilerParams(dimension_semantics=sem, vmem_limit_bytes=vmem)


def _dot(a, b):
    return jnp.dot(a, b, preferred_element_type=F32)


def _dot_nt(a, b):
    return lax.dot_general(a, b, (((1,), (1,)), ((), ())), preferred_element_type=F32)


def _mm_kernel(a_ref, b_ref, o_ref, *scratch, nk, kaxis, scale):
    prod = _dot(a_ref[...].astype(BF), b_ref[...].astype(BF))
    if nk == 1:
        o_ref[...] = (prod * scale).astype(o_ref.dtype)
        return
    acc = scratch[0]
    k = pl.program_id(kaxis)

    @pl.when(k == 0)
    def _():
        acc[...] = jnp.zeros_like(acc)

    acc[...] += prod

    @pl.when(k == nk - 1)
    def _():
        o_ref[...] = (acc[...] * scale).astype(o_ref.dtype)


def _mm(a, b, out_shape, grid, a_spec, b_spec, o_spec, mn, *, nk=1, kaxis=None, scale=1.0, name):
    sem = tuple("arbitrary" if i == kaxis else "parallel" for i in range(len(grid)))
    return pl.pallas_call(
        functools.partial(_mm_kernel, nk=nk, kaxis=kaxis, scale=scale),
        out_shape=out_shape, grid=grid, in_specs=[a_spec, b_spec], out_specs=o_spec,
        scratch_shapes=[pltpu.VMEM(mn, F32)] if nk > 1 else [],
        compiler_params=_cp(sem), name=name)(a, b)


def _xw(x, w, tm=512, name="xw"):
    M, K = x.shape
    N = w.shape[1]
    tm = min(tm, M)
    return _mm(x, w, jax.ShapeDtypeStruct((M, N), F32), (M // tm,),
               pl.BlockSpec((tm, K), lambda i: (i, 0)), pl.BlockSpec((K, N), lambda i: (0, 0)),
               pl.BlockSpec((tm, N), lambda i: (i, 0)), (tm, N), name=name)


def _ln_rows(y, g, b):
    mu = jnp.mean(y, -1, keepdims=True)
    d = y - mu
    var = jnp.mean(d * d, -1, keepdims=True)
    return d * lax.rsqrt(var + LN_EPS) * g + b


def _mm_res_ln_kernel(*refs, npairs, alpha):
    x_ref, g_ref, b_ref = refs[2 * npairs:2 * npairs + 3]
    o_ref = refs[2 * npairs + 3]
    y = alpha * x_ref[...]
    for i in range(npairs):
        y = y + _dot(refs[2 * i][...].astype(BF), refs[2 * i + 1][...])
    o_ref[...] = _ln_rows(y, g_ref[...], b_ref[...])


def _mm_res_ln(pairs, x, g, b, alpha, tm=512, name="mm_res_ln"):
    M, D = x.shape
    tm = min(tm, M)
    specs, args = [], []
    for a, w in pairs:
        specs += [pl.BlockSpec((tm, a.shape[1]), lambda i: (i, 0)), pl.BlockSpec(w.shape, lambda i: (0, 0))]
        args += [a, w]
    specs += [pl.BlockSpec((tm, D), lambda i: (i, 0)), pl.BlockSpec((1, D), lambda i: (0, 0)),
              pl.BlockSpec((1, D), lambda i: (0, 0))]
    args += [x, g.reshape(1, D), b.reshape(1, D)]
    return pl.pallas_call(
        functools.partial(_mm_res_ln_kernel, npairs=len(pairs), alpha=alpha),
        out_shape=jax.ShapeDtypeStruct((M, D), F32), grid=(M // tm,), in_specs=specs,
        out_specs=pl.BlockSpec((tm, D), lambda i: (i, 0)), compiler_params=_cp(("parallel",)), name=name)(*args)


def _rel_bucket(rel):
    nb = REL_BUCKETS // 2
    max_exact = nb // 2
    n = jnp.abs(rel)
    large = max_exact + (jnp.log(jnp.maximum(n, 1).astype(F32) / max_exact)
                         / math.log(REL_MAX_DIST / max_exact) * (nb - max_exact)).astype(I32)
    large = jnp.minimum(large, nb - 1)
    return jnp.where(rel > 0, nb, 0) + jnp.where(n < max_exact, n, large)


LOG2E = 1.4426950408889634


def _bias_tiles(rel_table, T):
    assert T >= REL_MAX_DIST
    i = jnp.arange(T)
    rel = (jnp.arange(-2, 3)[:, None, None] * T + i[None, :, None]) - i[None, None, :]
    bucket = _rel_bucket(rel)[None]
    tab = LOG2E * rel_table.astype(F32).T[:, :, None, None, None]
    out = jnp.zeros((rel_table.shape[1],) + rel.shape, F32)
    for b in range(REL_BUCKETS):
        out = jnp.where(bucket == b, tab[:, b], out)
    return out


ATTN_MAX_DRIFT = 40.0
ATTN_CHUNK_GROUP = 3


def _attn_kernel(lam_ref, q_ref, k_ref, v_ref, bias_ref, g_ref, o_ref, kb, vt, m_s, l_s, a_s, *, T, nk, out_scale):
    qi = pl.program_id(2)
    tq = q_ref.shape[0]

    @pl.when(qi == 0)
    def _():
        def prep(c, carry):
            st = pl.multiple_of(c * T, T)
            kb[c] = k_ref[pl.ds(st, T), :].astype(BF)
            vt[c] = v_ref[pl.ds(st, T), :].T.astype(BF)
            return carry

        lax.fori_loop(0, nk, prep, 0)

    q = q_ref[...] * (A_DK ** -0.5 * LOG2E)
    lane = lax.broadcasted_iota(I32, q.shape, 1)
    qs = (jnp.where(lane < A_DK, q, 0.0).astype(BF), jnp.where(lane >= A_DK, q, 0.0).astype(BF))

    def scores(kj, mi):
        return _dot_nt(kb[kj], qs[mi]) + bias_ref[jnp.clip(kj - qi, -2, 2) + 2]

    def rescaling_chunk(kj, first):
        for mi in range(2):
            s = scores(kj, mi)
            m_new = jnp.max(s, axis=0, keepdims=True)
            if not first:
                m_new = jnp.maximum(m_s[mi], m_new)
            p = jnp.exp2(s - m_new)
            lsum = jnp.sum(p, axis=0, keepdims=True)
            pv = _dot(vt[kj], p.astype(BF))
            if first:
                l_s[mi] = lsum
                a_s[mi] = pv
            else:
                alpha = jnp.exp2(m_s[mi] - m_new)
                l_s[mi] = alpha * l_s[mi] + lsum
                a_s[mi] = alpha * a_s[mi] + pv
            m_s[mi] = m_new

    rescaling_chunk(0, True)

    def lazy_chunks(kjs):
        lsum, pv, drift = [None, None], [None, None], None
        for kj in kjs:
            for mi in range(2):
                s = scores(kj, mi)
                m_ref = m_s[mi]
                p = jnp.exp2(s - m_ref)
                ls = jnp.sum(p, axis=0, keepdims=True)
                acc = _dot(vt[kj], p.astype(BF))
                lsum[mi] = ls if lsum[mi] is None else lsum[mi] + ls
                pv[mi] = acc if pv[mi] is None else pv[mi] + acc
                d = jnp.max(s, axis=0, keepdims=True) - m_ref
                drift = d if drift is None else jnp.maximum(drift, d)
        over = jnp.max(drift) > ATTN_MAX_DRIFT

        @pl.when(jnp.logical_not(over))
        def _():
            for mi in range(2):
                l_s[mi] += lsum[mi]
                a_s[mi] += pv[mi]

        @pl.when(over)
        def _():
            for kj in kjs:
                rescaling_chunk(kj, False)

    group = ATTN_CHUNK_GROUP

    def grouped(i, c):
        lazy_chunks([1 + group * i + u for u in range(group)])
        return c

    ngroups = (nk - 1) // group
    lax.fori_loop(0, ngroups, grouped, 0)
    if (nk - 1) % group:
        lazy_chunks(list(range(1 + group * ngroups, nk)))
    o = a_s[0] / l_s[0] - lam_ref[0] * (a_s[1] / l_s[1])
    o = o * lax.rsqrt(jnp.mean(o * o, 0, keepdims=True) + LN_EPS) * g_ref[...] * out_scale
    o_ref[...] = o.T


def _diff_attention(hproj, lam_full, sub_g, rel_table, lambda_init, T):
    B, L, _ = hproj.shape
    T = min(T, L)
    bias = _bias_tiles(rel_table, T)
    H = A_HEADS
    return pl.pallas_call(
        functools.partial(_attn_kernel, T=T, nk=L // T, out_scale=1.0 - lambda_init),
        out_shape=jax.ShapeDtypeStruct((B, L, D_A), F32), grid=(B, H, L // T),
        in_specs=[pl.BlockSpec(memory_space=pltpu.SMEM),
                  pl.BlockSpec((None, T, LANES), lambda b, h, i: (b, i, h)),
                  pl.BlockSpec((None, L, LANES), lambda b, h, i: (b, 0, H + h)),
                  pl.BlockSpec((None, L, LANES), lambda b, h, i: (b, 0, 2 * H + h)),
                  pl.BlockSpec((None, 5, T, T), lambda b, h, i: (h, 0, 0, 0)),
                  pl.BlockSpec((A_DV, 1), lambda b, h, i: (0, 0))],
        out_specs=pl.BlockSpec((None, T, LANES), lambda b, h, i: (b, i, h)),
        scratch_shapes=[pltpu.VMEM((L // T, T, LANES), BF), pltpu.VMEM((L // T, LANES, T), BF),
                        pltpu.VMEM((2, 1, T), F32), pltpu.VMEM((2, 1, T), F32), pltpu.VMEM((2, A_DV, T), F32)],
        compiler_params=_cp(("parallel", "parallel", "arbitrary")), name="diff_attn",
    )(lam_full.reshape(1), hproj, hproj, hproj, bias, sub_g.astype(F32).reshape(A_DV, 1))


def _short_conv_kernel(u_ref, w_ref, b_ref, o_ref):
    u = u_ref[...]
    L = u.shape[0]
    row = lax.broadcasted_iota(I32, u.shape, 0)
    up = jnp.where(row == 0, 0.0, pltpu.roll(u, 1, 0))
    dn = jnp.where(row == L - 1, 0.0, pltpu.roll(u, L - 1, 0))
    w = w_ref[...]
    o_ref[...] = up * w[0:1] + u * w[1:2] + dn * w[2:3] + b_ref[...]


def _short_conv(hproj, conv_w, conv_b):
    B, L, _ = hproj.shape
    nblk = 3 * D_B // LANES
    off = 3 * D_A // LANES
    per = D_B // LANES
    return pl.pallas_call(
        _short_conv_kernel, out_shape=jax.ShapeDtypeStruct((3, B, L, D_B), F32), grid=(B, nblk),
        in_specs=[pl.BlockSpec((None, L, LANES), lambda b, j: (b, 0, off + j)),
                  pl.BlockSpec((3, LANES), lambda b, j: (0, j)),
                  pl.BlockSpec((1, LANES), lambda b, j: (0, j))],
        out_specs=pl.BlockSpec((None, None, L, LANES), lambda b, j: (j // per, b, 0, j % per)),
        compiler_params=_cp(("parallel", "parallel")), name="short_conv",
    )(hproj, conv_w.astype(F32), conv_b.astype(F32).reshape(1, -1))


def _filter_mlp_kernel(z_ref, w1_ref, b1_ref, w2_ref, b2_ref, w3_ref, fq_ref, dec_ref, h_ref, s_ref):
    i = pl.program_id(0)
    z = z_ref[...]
    fq = fq_ref[...]
    h = jnp.sin(fq[0:1] * (_dot(z.astype(BF), w1_ref[...]) + b1_ref[...]))
    h = jnp.sin(fq[1:2] * (_dot(h.astype(BF), w2_ref[...]) + b2_ref[...]))
    t = z[:, 0:1]
    h = _dot(h.astype(BF), w3_ref[...]) * jnp.exp(-t * jnp.abs(dec_ref[...]))

    @pl.when(i == 0)
    def _():
        s_ref[...] = jnp.zeros_like(s_ref)

    s_ref[...] += jnp.sum(jnp.abs(h), axis=0, keepdims=True)
    row = lax.broadcasted_iota(I32, h.shape, 0)
    col = lax.broadcasted_iota(I32, h.shape, 1)
    half = h.shape[1] // 2
    h_ref[...] = jnp.where((row + i * h.shape[0] == 0) & (col >= half), 0.0, h)


def _filter_mlp(L, w1, b1, w2, b2, w3, freq, decay, tl=512):
    t = jnp.linspace(0.0, 1.0, L, dtype=F32)[:, None]
    wpos = 2.0 * math.pi * jnp.arange(L, dtype=F32)[:, None] / L
    fr = jnp.linspace(1e-4, FILTER_BANDS - 1, FILTER_BANDS, dtype=F32)[None, :]
    z = jnp.concatenate([t, jnp.cos(fr * wpos), -jnp.sin(fr * wpos)], -1)
    emb, width = w1.shape
    ch = w3.shape[1]
    z = jnp.pad(z, ((0, 0), (0, LANES - emb)))
    padw = LANES - width
    w1p = jnp.pad(w1, ((0, LANES - emb), (0, padw))).astype(BF)
    w2p = jnp.pad(w2, ((0, padw), (0, padw))).astype(BF)
    w3p = jnp.pad(w3, ((0, padw), (0, 0))).astype(BF)
    b1p = jnp.pad(b1.astype(F32), (0, padw)).reshape(1, LANES)
    b2p = jnp.pad(b2.astype(F32), (0, padw)).reshape(1, LANES)
    fqp = jnp.pad(freq.astype(F32), ((0, 0), (0, padw)))
    tl = min(tl, L)
    full = lambda shp: pl.BlockSpec(shp, lambda i: (0, 0))
    return pl.pallas_call(
        _filter_mlp_kernel,
        out_shape=(jax.ShapeDtypeStruct((L, ch), F32), jax.ShapeDtypeStruct((1, ch), F32)), grid=(L // tl,),
        in_specs=[pl.BlockSpec((tl, LANES), lambda i: (i, 0)), full((LANES, LANES)), full((1, LANES)),
                  full((LANES, LANES)), full((1, LANES)), full((LANES, ch)), full((2, LANES)), full((1, ch))],
        out_specs=(pl.BlockSpec((tl, ch), lambda i: (i, 0)), full((1, ch))),
        compiler_params=_cp(("arbitrary",)), name="filter_mlp",
    )(z, w1p, b1p, w2p, b2p, w3p, fqp, decay.astype(F32).reshape(1, ch))


def _cis(idx, n):
    ang = (2.0 * math.pi / n) * (idx % n).astype(F32)
    return jnp.cos(ang), -jnp.sin(ang)


def _blk(re, im):
    return jnp.concatenate([jnp.concatenate([re, -im], -1), jnp.concatenate([im, re], -1)], -2)


def _dft_tables(N):
    N2 = FFT_N2
    N1 = N // N2
    a = jnp.arange(N1)
    f1r, f1i = _cis(a[:, None] * a[None, :], N1)
    k = a[:, None, None] + N1 * jnp.arange(N2)[None, :, None]
    gr, gi = _cis(k * jnp.arange(N2)[None, None, :], N)
    return N1, f1r, f1i, gr, gi


def _mid_kernel(g_ref, gi_ref, kf_ref, a_ref, o_ref):
    n2, c = a_ref.shape[2], a_ref.shape[3]
    kr, ki = kf_ref[0], kf_ref[1]
    for p in range(a_ref.shape[0]):
        x = _dot(g_ref[...], a_ref[p].reshape(2 * n2, c).astype(BF))
        xr, xi = x[:n2], x[n2:]
        y = jnp.concatenate([xr * kr - xi * ki, xr * ki + xi * kr], axis=0).astype(BF)
        o_ref[p] = _dot(gi_ref[...], y).reshape(2, n2, c).astype(o_ref.dtype)


def _filt_mid_kernel(g_ref, a_ref, s_ref, o_ref):
    n2 = a_ref.shape[1]
    c = a_ref.shape[2] // 2
    x = _dot(g_ref[...], a_ref[...].reshape(2 * n2, 2 * c).astype(BF))
    s = s_ref[...]
    inv = 1.0 / (s[:, :c] + s[:, c:])
    o_ref[0] = (x[:n2, :c] + x[:n2, c:]) * inv
    o_ref[1] = (x[n2:, :c] - x[n2:, c:]) * inv


def _gate_inv_kernel(m_ref, d_ref, z_ref, gate_ref, sk_ref, o_ref, *, scale):
    conv = _dot(m_ref[...], d_ref[...].astype(BF)) * scale
    o_ref[...] = gate_ref[...] * (conv + z_ref[...] * sk_ref[...])


def _hyena(uc, filt, filt_sum, skip):
    _, B, L, C = uc.shape
    assert B % 2 == 0
    P = B // 2
    N = 2 * L
    N2 = FFT_N2
    N1, f1r, f1i, gr, gi = _dft_tables(N)
    h = N1 // 2
    cols = N2 * C
    m1 = jnp.concatenate([jnp.concatenate([f1r[:, :h], -f1i[:, :h]], 1),
                          jnp.concatenate([f1i[:, :h], f1r[:, :h]], 1)], 0).astype(BF)
    m1_real = jnp.concatenate([f1r[:, :h], f1i[:, :h]], 0).astype(BF)
    m1_inv = jnp.concatenate([jnp.concatenate([f1r[:h], f1i[:h]], 1),
                              jnp.concatenate([-f1i[:h], f1r[:h]], 1)], 0).astype(BF)
    g_fwd = _blk(gr, gi).astype(BF)
    g_inv = _blk(jnp.swapaxes(gr, 1, 2), -jnp.swapaxes(gi, 1, 2)).astype(BF)

    tc = min(cols, 4096)
    CF = filt.shape[1]
    fcols = N2 * CF
    tcf = min(fcols, 8192)
    fa = _mm(m1_real, filt.reshape(h, fcols), jax.ShapeDtypeStruct((2 * N1, fcols), BF), (fcols // tcf,),
             pl.BlockSpec((2 * N1, h), lambda j: (0, 0)), pl.BlockSpec((h, tcf), lambda j: (0, j)),
             pl.BlockSpec((2 * N1, tcf), lambda j: (0, j)), None, name="filt_s1")
    kf = pl.pallas_call(
        _filt_mid_kernel, out_shape=jax.ShapeDtypeStruct((N1, 2, N2, CF // 2), F32), grid=(N1,),
        in_specs=[pl.BlockSpec((None, 2 * N2, 2 * N2), lambda k: (k, 0, 0)),
                  pl.BlockSpec((2, None, N2, CF), lambda k: (0, k, 0, 0)),
                  pl.BlockSpec((1, CF), lambda k: (0, 0))],
        out_specs=pl.BlockSpec((None, 2, N2, CF // 2), lambda k: (k, 0, 0, 0)),
        compiler_params=_cp(("parallel",)), name="filt_mid",
    )(g_fwd, fa.reshape(2, N1, N2, CF), filt_sum)

    sk = jnp.tile(skip.astype(F32), (1, N2))
    z = uc[0].reshape(P, N1, cols)
    for n in range(B_ORDER):
        gate = uc[1 + n].reshape(P, N1, cols)
        a = _mm(m1, z, jax.ShapeDtypeStruct((P, 2 * N1, cols), BF), (P, cols // tc),
                pl.BlockSpec((2 * N1, N1), lambda p, j: (0, 0)), pl.BlockSpec((None, N1, tc), lambda p, j: (p, 0, j)),
                pl.BlockSpec((None, 2 * N1, tc), lambda p, j: (p, 0, j)), None, name="hy_s1")
        d = pl.pallas_call(
            _mid_kernel, out_shape=jax.ShapeDtypeStruct((P, 2, N1, N2, C), BF), grid=(N1,),
            in_specs=[pl.BlockSpec((None, 2 * N2, 2 * N2), lambda k: (k, 0, 0)),
                      pl.BlockSpec((None, 2 * N2, 2 * N2), lambda k: (k, 0, 0)),
                      pl.BlockSpec((None, 2, N2, C), lambda k: (k, 0, 0, n)),
                      pl.BlockSpec((P, 2, None, N2, C), lambda k: (0, 0, k, 0, 0))],
            out_specs=pl.BlockSpec((P, 2, None, N2, C), lambda k: (0, 0, k, 0, 0)),
            compiler_params=_cp(("parallel",)), name="hy_mid",
        )(g_fwd, g_inv, kf, a.reshape(P, 2, N1, N2, C))
        z = pl.pallas_call(
            functools.partial(_gate_inv_kernel, scale=1.0 / N),
            out_shape=jax.ShapeDtypeStruct((P, N1, cols), F32), grid=(P, cols // tc),
            in_specs=[pl.BlockSpec((N1, 2 * N1), lambda p, j: (0, 0)),
                      pl.BlockSpec((None, 2 * N1, tc), lambda p, j: (p, 0, j)),
                      pl.BlockSpec((None, N1, tc), lambda p, j: (p, 0, j)),
                      pl.BlockSpec((None, N1, tc), lambda p, j: (p, 0, j)),
                      pl.BlockSpec((1, tc), lambda p, j: (0, j))],
            out_specs=pl.BlockSpec((None, N1, tc), lambda p, j: (p, 0, j)),
            compiler_params=_cp(("parallel", "parallel")), name="hy_inv_gate",
        )(m1_inv, d.reshape(P, 2 * N1, cols), z, gate, sk[n:n + 1])
    return z.reshape(B, L, C)


def _fnet_tables(L, D):
    gw = D // C_GROUPS
    c = jnp.arange(gw)
    cr, ci = _cis(c[:, None] * c[None, :], gw)
    wch = jnp.concatenate([cr, ci], 1).astype(BF)
    N1, f1r, f1i, gr, gi = _dft_tables(L)
    m1 = _blk(f1r, f1i).astype(BF)
    g_re = jnp.concatenate([gr, -gi], -1).astype(BF)
    return N1, wch, m1, g_re


def _fnet_ch_kernel(x_ref, w_ref, o_ref):
    gw = x_ref.shape[1]
    res = _dot(x_ref[...].astype(BF), w_ref[...])
    o_ref[0] = res[:, :gw].astype(o_ref.dtype)
    o_ref[1] = res[:, gw:].astype(o_ref.dtype)


def _fnet_s2_kernel(g_ref, a_ref, o_ref, *, scale):
    kb, n2 = g_ref.shape[0], g_ref.shape[1]
    d = a_ref.shape[3]
    for j in range(kb):
        res = _dot(g_ref[j], a_ref[:, j].reshape(2 * n2, d))
        o_ref[:, j * d:(j + 1) * d] = res * scale


def _fourier_mix(x, B, L):
    T, D = x.shape
    N2 = FFT_N2
    N1, wch, m1, g_re = _fnet_tables(L, D)
    gw = D // C_GROUPS
    tm = min(1024, T)
    y = pl.pallas_call(
        _fnet_ch_kernel, out_shape=jax.ShapeDtypeStruct((2, T, D), BF), grid=(T // tm, C_GROUPS),
        in_specs=[pl.BlockSpec((tm, gw), lambda i, c: (i, c)), pl.BlockSpec((gw, 2 * gw), lambda i, c: (0, 0))],
        out_specs=pl.BlockSpec((2, tm, gw), lambda i, c: (0, i, c)),
        compiler_params=_cp(("parallel", "parallel")), name="fnet_ch")(x, wch)
    cols = N2 * D
    tc = min(cols, 8192)
    a = _mm(jnp.stack([m1[:, :N1], m1[:, N1:]]), y.reshape(2, B, N1, cols), jax.ShapeDtypeStruct((B, 2 * N1, cols), BF), (B, cols // tc, 2),
            pl.BlockSpec((None, 2 * N1, N1), lambda b, j, p: (p, 0, 0)),
            pl.BlockSpec((None, None, N1, tc), lambda b, j, p: (p, b, 0, j)),
            pl.BlockSpec((None, 2 * N1, tc), lambda b, j, p: (b, 0, j)), (2 * N1, tc), nk=2, kaxis=2, name="fnet_s1")
    kb = min(8, N1)
    f = pl.pallas_call(
        functools.partial(_fnet_s2_kernel, scale=1.0 / math.sqrt(L * gw)),
        out_shape=jax.ShapeDtypeStruct((B, N2, N1 * D), F32), grid=(B, N1 // kb),
        in_specs=[pl.BlockSpec((kb, N2, 2 * N2), lambda b, k: (k, 0, 0)),
                  pl.BlockSpec((None, 2, kb, N2, D), lambda b, k: (b, 0, k, 0, 0))],
        out_specs=pl.BlockSpec((None, N2, kb * D), lambda b, k: (b, 0, k)),
        compiler_params=_cp(("parallel", "parallel")), name="fnet_s2")(g_re, a.reshape(B, 2, N1, N2, D))
    return f.reshape(T, D)


def _router_kernel(w_ref, x_ref, o_ref):
    logits = _dot_nt(w_ref[...], x_ref[...].astype(BF))
    m = jnp.max(logits, axis=0, keepdims=True)
    p = jnp.exp(logits - m)
    o_ref[...] = p / jnp.sum(p, axis=0, keepdims=True)


def _router(x, w_router_t, tm=1024):
    T, D = x.shape
    E = w_router_t.shape[0]
    tm = min(tm, T)
    return pl.pallas_call(
        _router_kernel, out_shape=jax.ShapeDtypeStruct((E, T), F32), grid=(T // tm,),
        in_specs=[pl.BlockSpec((E, D), lambda i: (0, 0)), pl.BlockSpec((tm, D), lambda i: (i, 0))],
        out_specs=pl.BlockSpec((E, tm), lambda i: (0, i)), compiler_params=_cp(("parallel",)), name="router")(
            w_router_t, x)


def _prefix_counts(mask_f, upper, lower):
    rowcs = _dot(mask_f.astype(BF), upper)
    tot = jnp.broadcast_to(rowcs[:, LANES - 1:LANES], mask_f.shape)
    offs = _dot(lower, tot.astype(BF))
    return offs + rowcs - mask_f, offs


def _select_kernel(a_ref, rank_ref, off_ref, *, cap):
    v = a_ref[...]
    R = v.shape[0]
    bits = pltpu.bitcast(v, I32)

    def step(i, thr):
        cand = thr | (jnp.int32(1) << (30 - i))
        cnt = jnp.sum((bits >= cand).astype(F32))
        return jnp.where(cnt >= cap, cand, thr)

    thr = lax.fori_loop(0, 31, step, jnp.int32(0))
    gt = bits > thr
    eq = bits == thr
    need = cap - jnp.sum(gt.astype(F32))
    li = lax.broadcasted_iota(I32, (LANES, LANES), 0)
    lj = lax.broadcasted_iota(I32, (LANES, LANES), 1)
    upper = (li <= lj).astype(BF)
    ri = lax.broadcasted_iota(I32, (R, R), 0)
    rj = lax.broadcasted_iota(I32, (R, R), 1)
    lower = (rj < ri).astype(BF)
    eq_rank, _ = _prefix_counts(eq.astype(F32), upper, lower)
    sel = gt | (eq & (eq_rank < need))
    rank, offs = _prefix_counts(sel.astype(F32), upper, lower)
    rank_ref[...] = jnp.where(sel, rank.astype(I32), -1)
    off_ref[...] = offs.astype(I32)


def _select(aff, cap):
    E, T = aff.shape
    R = T // LANES
    spec = pl.BlockSpec((None, R, LANES), lambda e: (e, 0, 0))
    return pl.pallas_call(
        functools.partial(_select_kernel, cap=cap),
        out_shape=(jax.ShapeDtypeStruct((E, R, LANES), I32), jax.ShapeDtypeStruct((E, R, LANES), I32)),
        grid=(E,), in_specs=[spec], out_specs=(spec, spec), compiler_params=_cp(("parallel",)), name="select")(
            aff.reshape(E, R, LANES))


def _compact_kernel(off_ref, rank_ref, idx_ref, *, R):
    e = pl.program_id(0)
    idx_ref[...] = jnp.zeros_like(idx_ref)
    sub = lax.broadcasted_iota(I32, (8, LANES), 0)
    lane = lax.broadcasted_iota(I32, (8, LANES), 1)
    lhs = jnp.where(sub == 0, lane, jnp.where(sub == 1, 1, 0)).astype(BF)
    slot = lax.broadcasted_iota(I32, (2 * LANES, LANES), 0)

    def chunk(r, c):
        a = off_ref[e, r] // LANES
        local = rank_ref[pl.ds(r, 1), :] - a * LANES
        onehot = (slot == local).astype(BF)
        res = _dot_nt(lhs, onehot)
        val = (res[0:1] + res[1:2] * lax.convert_element_type(r * LANES, F32)).astype(I32)
        idx_ref[pl.ds(a, 1), :] += val[:, :LANES]
        idx_ref[pl.ds(a + 1, 1), :] += val[:, LANES:]
        return c

    lax.fori_loop(0, R, chunk, 0, unroll=8)


def _compact(rank, rowoff, cap):
    E, R, _ = rank.shape
    nrow = cap // LANES + 2
    return pl.pallas_call(
        functools.partial(_compact_kernel, R=R),
        out_shape=jax.ShapeDtypeStruct((E, nrow, LANES), I32),
        grid_spec=pltpu.PrefetchScalarGridSpec(
            num_scalar_prefetch=1, grid=(E,),
            in_specs=[pl.BlockSpec((None, R, LANES), lambda e, off: (e, 0, 0))],
            out_specs=pl.BlockSpec((None, nrow, LANES), lambda e, off: (e, 0, 0))),
        compiler_params=_cp(("arbitrary",)), name="compact")(rowoff, rank)


def _ffn_kernel(idx_ref, idxn_ref, x_hbm, wg_ref, wu_ref, wd_ref, o_ref, xbuf, xb, acc, sem, *, tm, nf, ntiles):
    f = pl.program_id(2)
    tile = pl.program_id(0) * pl.num_programs(1) + pl.program_id(1)
    slot = tile % 2
    part = tm // nf

    def row_copy(t, r, slot_):
        return pltpu.make_async_copy(x_hbm.at[pl.ds(t, 1)], xbuf.at[slot_, pl.ds(r, 1)], sem.at[slot_])

    def wait_tile(slot_):
        pltpu.make_async_copy(x_hbm.at[pl.ds(0, tm)], xbuf.at[slot_], sem.at[slot_]).wait()

    @pl.when((tile == 0) & (f == 0))
    def _():
        def issue(r, c):
            row_copy(idx_ref[0, 0, r], r, 0).start()
            return c

        lax.fori_loop(0, tm, issue, 0)

    @pl.when(f == 0)
    def _():
        wait_tile(slot)
        xb[...] = xbuf[slot].astype(BF)
        if nf > 1:
            acc[...] = jnp.zeros_like(acc)

    base = f * part
    for j in range(part):
        row_copy(idxn_ref[0, 0, base + j], base + j, 1 - slot).start()

    x = xb[...]
    g = _dot(x, wg_ref[...])
    u = _dot(x, wu_ref[...])
    hcur = (g * jax.nn.sigmoid(g)) * u
    down = _dot(hcur.astype(BF), wd_ref[...])
    if nf == 1:
        o_ref[...] = down.astype(o_ref.dtype)
    else:
        acc[...] += down

        @pl.when(f == nf - 1)
        def _():
            o_ref[...] = acc[...].astype(o_ref.dtype)

    @pl.when((tile == ntiles - 1) & (f == nf - 1))
    def _():
        wait_tile(1 - slot)


def _ffn(x, idx, wg, wu, wd, layer, cap, tm=512):
    T, D = x.shape
    E = wg.shape[1]
    FF = wg.shape[3]
    tm = min(tm, cap)
    nt = cap // tm
    fc, nf = FF, 1
    resident = pl.Buffered(1)
    idx3 = idx[:, :cap // LANES, :].reshape(E * nt, 1, tm)
    last = E * nt - 1
    return pl.pallas_call(
        functools.partial(_ffn_kernel, tm=tm, nf=nf, ntiles=E * nt),
        out_shape=jax.ShapeDtypeStruct((E, cap, D), BF), grid=(E, nt, nf),
        in_specs=[pl.BlockSpec((1, 1, tm), lambda e, i, f: (e * nt + i, 0, 0), memory_space=pltpu.SMEM),
                  pl.BlockSpec((1, 1, tm), lambda e, i, f: (jnp.minimum(e * nt + i + 1, last), 0, 0),
                               memory_space=pltpu.SMEM),
                  pl.BlockSpec(memory_space=pl.ANY),
                  pl.BlockSpec((None, None, D, fc), lambda e, i, f: (layer, e, 0, f), pipeline_mode=resident),
                  pl.BlockSpec((None, None, D, fc), lambda e, i, f: (layer, e, 0, f), pipeline_mode=resident),
                  pl.BlockSpec((None, None, fc, D), lambda e, i, f: (layer, e, f, 0), pipeline_mode=resident)],
        out_specs=pl.BlockSpec((None, tm, D), lambda e, i, f: (e, i, 0)),
        scratch_shapes=[pltpu.VMEM((2, tm, D), F32), pltpu.VMEM((tm, D), BF),
                        pltpu.VMEM((tm, D) if nf > 1 else (8, LANES), F32), pltpu.SemaphoreType.DMA((2,))],
        compiler_params=_cp(("arbitrary", "arbitrary", "arbitrary")), name="expert_ffn")(idx3, idx3, x, wg, wu, wd)


COMBINE_WIN = 64


def _combine_kernel(off_ref, rank_ref, aff_ref, x_ref, g_ref, b_ref, out_hbm, o_ref, wins, winx, ysc, sem, semx,
                    *, tt, E, cap, rpt, ntile, alpha):
    W = COMBINE_WIN
    ti = pl.program_id(0)
    slot = ti % 2

    def start0(e, ti_):
        r0 = off_ref[e, ti_ * rpt]
        return pl.multiple_of(jnp.minimum((r0 // 16) * 16, cap - W), 16)

    def win_copy(e, ti_, slot_):
        return pltpu.make_async_copy(out_hbm.at[e, pl.ds(start0(e, ti_), W)], wins.at[slot_, pl.ds(e * W, W)],
                                     sem.at[slot_])

    @pl.when(ti == 0)
    def _():
        for e in range(E):
            win_copy(e, 0, 0).start()

    @pl.when(ti + 1 < ntile)
    def _():
        for e in range(E):
            win_copy(e, ti + 1, 1 - slot).start()

    sub = lax.broadcasted_iota(I32, (W, tt), 0)

    def gated_onehot(r_row, g_row, first_row, min_rank):
        g_hi = g_row.astype(BF).astype(F32)
        match = (sub == r_row - first_row) & (r_row >= min_rank)
        return jnp.concatenate([jnp.where(match, g_hi, 0.0), jnp.where(match, g_row - g_hi, 0.0)], axis=1).astype(BF)

    def scatter_rows(p_t, rows):
        y2 = lax.dot_general(p_t, rows, (((0,), (0,)), ((), ())), preferred_element_type=F32)
        return y2[:tt] + y2[tt:]

    p_t = jnp.concatenate([gated_onehot(rank_ref[e:e + 1, :], aff_ref[e:e + 1, :], start0(e, ti), 0)
                           for e in range(E)], axis=0)
    for e in range(E):
        win_copy(e, ti, slot).wait()
    ysc[...] = alpha * x_ref[...] + scatter_rows(p_t, wins[slot])

    def extra(e, c):
        s0 = start0(e, ti)
        r_end = off_ref[e, (ti + 1) * rpt]

        @pl.when(r_end > s0 + W)
        def _():
            r_row = rank_ref[pl.ds(e, 1), :]
            g_row = aff_ref[pl.ds(e, 1), :]
            for w in range(1, tt // W + 1):
                lo_w = s0 + w * W

                @pl.when(r_end > lo_w)
                def _():
                    sw = pl.multiple_of(jnp.minimum(lo_w, cap - W), 16)
                    cp = pltpu.make_async_copy(out_hbm.at[e, pl.ds(sw, W)], winx, semx)
                    cp.start()
                    cp.wait()
                    ysc[...] += scatter_rows(gated_onehot(r_row, g_row, sw, lo_w), winx[...])

        return c

    lax.fori_loop(0, E, extra, 0)
    o_ref[...] = _ln_rows(ysc[...], g_ref[...], b_ref[...])


def _combine_ln(out, rank, aff, rowoff, x, g, b, cap, alpha):
    T, D = x.shape
    E = out.shape[0]
    W = COMBINE_WIN
    assert cap >= W
    tt = min(256, cap // 2)
    rpt = tt // LANES
    ntile = T // tt
    kern = functools.partial(_combine_kernel, tt=tt, E=E, cap=cap, rpt=rpt, ntile=ntile, alpha=alpha)
    return pl.pallas_call(
        kern, out_shape=jax.ShapeDtypeStruct((T, D), F32),
        grid_spec=pltpu.PrefetchScalarGridSpec(
            num_scalar_prefetch=1, grid=(ntile,),
            in_specs=[pl.BlockSpec((E, tt), lambda i, off: (0, i)),
                      pl.BlockSpec((E, tt), lambda i, off: (0, i)),
                      pl.BlockSpec((tt, D), lambda i, off: (i, 0)),
                      pl.BlockSpec((1, D), lambda i, off: (0, 0)),
                      pl.BlockSpec((1, D), lambda i, off: (0, 0)),
                      pl.BlockSpec(memory_space=pl.ANY)],
            out_specs=pl.BlockSpec((tt, D), lambda i, off: (i, 0)),
            scratch_shapes=[pltpu.VMEM((2, E * W, D), BF), pltpu.VMEM((W, D), BF), pltpu.VMEM((tt, D), F32),
                            pltpu.SemaphoreType.DMA((2,)), pltpu.SemaphoreType.DMA]),
        compiler_params=_cp(("arbitrary",)), name="combine_ln",
    )(rowoff, rank, aff, x, g.reshape(1, D), b.reshape(1, D), out)


def _expert_choice_ln(x, w_router_t, wg, wu, wd, layer, g, b, alpha):
    T, D = x.shape
    E = N_EXPERTS
    cap = max(1, EC_FACTOR * T // E)
    aff = _router(x, w_router_t)
    rank, rowoff = _select(aff, cap)
    rowoff = jnp.concatenate([rowoff[:, :, 0], jnp.full((E, 1), cap, I32)], axis=1)
    idx = _compact(rank, rowoff, cap)
    out = _ffn(x, idx, wg, wu, wd, layer, cap)
    return _combine_ln(out, rank.reshape(E, T), aff, rowoff, x, g, b, cap, alpha)


def _trunk(x, p, depth):
    B, L, D = x.shape
    T = B * L
    alpha = (2 * depth) ** 0.25
    x = x.reshape(T, D)
    filt_cache = {}
    for layer in range(depth):
        j = layer // 2
        if layer % 2 == 0:
            hproj = _xw(x, p["ab_w_in"][j], name="ab_in").reshape(B, L, -1)
            lambda_init = 0.8 - 0.6 * math.exp(-0.3 * layer)
            lf = p["diff_lambda"][j].astype(F32)
            lam_full = jnp.exp(jnp.sum(lf[0] * lf[1])) - jnp.exp(jnp.sum(lf[2] * lf[3])) + lambda_init
            a_out = _diff_attention(hproj, lam_full, p["diff_subln_g"][j], p["rel_bias"], lambda_init, T=512)
            uc = _short_conv(hproj, p["hy_conv_w"][j], p["hy_conv_b"][j])
            filt, filt_sum = _filter_mlp(L, p["hy_f_w1"][j], p["hy_f_b1"][j], p["hy_f_w2"][j], p["hy_f_b2"][j],
                                         p["hy_f_w3"][j], p["hy_f_freq"][j], p["hy_decay"][j])
            b_out = _hyena(uc, filt, filt_sum, p["hy_skip"][j])
            w_out = p["ab_w_out"][j]
            pairs = [(a_out.reshape(T, D_A), w_out[:D_A]), (b_out.reshape(T, D_B), w_out[D_A:])]
        else:
            pairs = [(_fourier_mix(x, B, L), p["c_w_out"][j])]
        x = _mm_res_ln(pairs, x, p["ln_g"][layer, 0], p["ln_b"][layer, 0], alpha)
        x = _expert_choice_ln(x, p["ec_router_t"][layer], p["ec_w_gate"], p["ec_w_up"], p["ec_w_down"], layer,
                              p["ln_g"][layer, 1], p["ln_b"][layer, 1], alpha)
    return x.reshape(B, L, D)


def kernel(x_prompt, x_sample, rel_bias, ab_w_in, ab_w_out, diff_lambda, diff_subln_g, hy_conv_w, hy_conv_b,
           hy_f_w1, hy_f_b1, hy_f_w2, hy_f_b2, hy_f_w3, hy_f_freq, hy_decay, hy_skip,
           c_w_out, ec_router, ec_w_gate, ec_w_up, ec_w_down, ln_g, ln_b):
    depth = ec_router.shape[0]
    p = dict(
        rel_bias=rel_bias, ab_w_in=ab_w_in.astype(BF), ab_w_out=ab_w_out.astype(BF), diff_lambda=diff_lambda,
        diff_subln_g=diff_subln_g, hy_conv_w=hy_conv_w, hy_conv_b=hy_conv_b, hy_f_w1=hy_f_w1, hy_f_b1=hy_f_b1,
        hy_f_w2=hy_f_w2, hy_f_b2=hy_f_b2, hy_f_w3=hy_f_w3, hy_f_freq=hy_f_freq, hy_decay=hy_decay, hy_skip=hy_skip,
        c_w_out=c_w_out.astype(BF), ec_router_t=jnp.swapaxes(ec_router, 1, 2).astype(BF),
        ec_w_gate=ec_w_gate.astype(BF), ec_w_up=ec_w_up.astype(BF), ec_w_down=ec_w_down.astype(BF),
        ln_g=ln_g.astype(F32), ln_b=ln_b.astype(F32))
    return _trunk(x_prompt, p, depth), _trunk(x_sample, p, depth)
```

```python
import functools
import math

import jax
import jax.numpy as jnp
from jax import lax
from jax.experimental import pallas as pl
from jax.experimental.pallas import tpu as pltpu

BF = jnp.bfloat16
F32 = jnp.float32
I32 = jnp.int32

A_HEADS = 4
A_DV = 128
A_DK = 64
D_A = 512
D_B = 512
B_ORDER = 2
REL_BUCKETS = 32
REL_MAX_DIST = 128
FILTER_BANDS = 16
N_EXPERTS = 16
EC_FACTOR = 2
LN_EPS = 1e-5
C_GROUPS = 4
LANES = 128
BF16_SUBLANES = 16
FFT_N2 = 128
VMEM_LIMIT = 52 * 1024 * 1024


def _cp(sem, vmem=VMEM_LIMIT):
    return pltpu.CompilerParams(dimension_semantics=sem, vmem_limit_bytes=vmem)


def _dot(a, b):
    return jnp.dot(a, b, preferred_element_type=F32)


def _dot_nt(a, b):
    return lax.dot_general(a, b, (((1,), (1,)), ((), ())), preferred_element_type=F32)


def _mm_kernel(a_ref, b_ref, o_ref, *scratch, nk, kaxis, scale):
    prod = _dot(a_ref[...].astype(BF), b_ref[...].astype(BF))
    if nk == 1:
        o_ref[...] = (prod * scale).astype(o_ref.dtype)
        return
    acc = scratch[0]
    k = pl.program_id(kaxis)

    @pl.when(k == 0)
    def _():
        acc[...] = jnp.zeros_like(acc)

    acc[...] += prod

    @pl.when(k == nk - 1)
    def _():
        o_ref[...] = (acc[...] * scale).astype(o_ref.dtype)


def _mm(a, b, out_shape, grid, a_spec, b_spec, o_spec, mn, *, nk=1, kaxis=None, scale=1.0, name):
    sem = tuple("arbitrary" if i == kaxis else "parallel" for i in range(len(grid)))
    return pl.pallas_call(
        functools.partial(_mm_kernel, nk=nk, kaxis=kaxis, scale=scale),
        out_shape=out_shape, grid=grid, in_specs=[a_spec, b_spec], out_specs=o_spec,
        scratch_shapes=[pltpu.VMEM(mn, F32)] if nk > 1 else [],
        compiler_params=_cp(sem), name=name)(a, b)


def _xw(x, w, tm=512, name="xw"):
    M, K = x.shape
    N = w.shape[1]
    tm = min(tm, M)
    return _mm(x, w, jax.ShapeDtypeStruct((M, N), F32), (M // tm,),
               pl.BlockSpec((tm, K), lambda i: (i, 0)), pl.BlockSpec((K, N), lambda i: (0, 0)),
               pl.BlockSpec((tm, N), lambda i: (i, 0)), (tm, N), name=name)


def _ln_rows(y, g, b):
    mu = jnp.mean(y, -1, keepdims=True)
    d = y - mu
    var = jnp.mean(d * d, -1, keepdims=True)
    return d * lax.rsqrt(var + LN_EPS) * g + b


def _mm_res_ln_kernel(*refs, npairs, alpha):
    x_ref, g_ref, b_ref = refs[2 * npairs:2 * npairs + 3]
    o_ref = refs[2 * npairs + 3]
    y = alpha * x_ref[...]
    for i in range(npairs):
        y = y + _dot(refs[2 * i][...].astype(BF), refs[2 * i + 1][...])
    o_ref[...] = _ln_rows(y, g_ref[...], b_ref[...])


def _mm_res_ln(pairs, x, g, b, alpha, tm=512, name="mm_res_ln"):
    M, D = x.shape
    tm = min(tm, M)
    specs, args = [], []
    for a, w in pairs:
        specs += [pl.BlockSpec((tm, a.shape[1]), lambda i: (i, 0)), pl.BlockSpec(w.shape, lambda i: (0, 0))]
        args += [a, w]
    specs += [pl.BlockSpec((tm, D), lambda i: (i, 0)), pl.BlockSpec((1, D), lambda i: (0, 0)),
              pl.BlockSpec((1, D), lambda i: (0, 0))]
    args += [x, g.reshape(1, D), b.reshape(1, D)]
    return pl.pallas_call(
        functools.partial(_mm_res_ln_kernel, npairs=len(pairs), alpha=alpha),
        out_shape=jax.ShapeDtypeStruct((M, D), F32), grid=(M // tm,), in_specs=specs,
        out_specs=pl.BlockSpec((tm, D), lambda i: (i, 0)), compiler_params=_cp(("parallel",)), name=name)(*args)


def _rel_bucket(rel):
    nb = REL_BUCKETS // 2
    max_exact = nb // 2
    n = jnp.abs(rel)
    large = max_exact + (jnp.log(jnp.maximum(n, 1).astype(F32) / max_exact)
                         / math.log(REL_MAX_DIST / max_exact) * (nb - max_exact)).astype(I32)
    large = jnp.minimum(large, nb - 1)
    return jnp.where(rel > 0, nb, 0) + jnp.where(n < max_exact, n, large)


LOG2E = 1.4426950408889634


def _bias_tiles(rel_table, T):
    assert T >= REL_MAX_DIST
    i = jnp.arange(T)
    rel = (jnp.arange(-2, 3)[:, None, None] * T + i[None, :, None]) - i[None, None, :]
    bucket = _rel_bucket(rel)[None]
    tab = LOG2E * rel_table.astype(F32).T[:, :, None, None, None]
    out = jnp.zeros((rel_table.shape[1],) + rel.shape, F32)
    for b in range(REL_BUCKETS):
        out = jnp.where(bucket == b, tab[:, b], out)
    return out


ATTN_MAX_DRIFT = 40.0
ATTN_CHUNK_GROUP = 3
ATTN_TILE = 512


def _attn_kernel(lam_ref, q_ref, k_ref, v_ref, bias_ref, g_ref, o_ref, kb, vt, m_s, a_s, *, T, nk, out_scale):
    qi = pl.program_id(2)
    dv = A_DV

    @pl.when(qi == 0)
    def _():
        def prep(c, carry):
            st = pl.multiple_of(c * T, T)
            kb[c] = k_ref[pl.ds(st, T), :].astype(BF)
            vt[c, :dv] = v_ref[pl.ds(st, T), :].T.astype(BF)
            vt[c, dv:] = jnp.ones((vt.shape[1] - dv, T), BF)
            return carry

        lax.fori_loop(0, nk, prep, 0)

    q = q_ref[...] * (A_DK ** -0.5 * LOG2E)
    lane = lax.broadcasted_iota(I32, q.shape, 1)
    qs = (jnp.where(lane < A_DK, q, 0.0).astype(BF), jnp.where(lane >= A_DK, q, 0.0).astype(BF))

    def scores(kj, mi):
        return _dot_nt(kb[kj], qs[mi]) + bias_ref[jnp.clip(kj - qi, -2, 2) + 2]

    def rescaling_chunk(kj, first):
        for mi in range(2):
            s = scores(kj, mi)
            m_new = jnp.max(s, axis=0, keepdims=True)
            if not first:
                m_new = jnp.maximum(m_s[mi], m_new)
            pv = _dot(vt[kj], jnp.exp2(s - m_new).astype(BF))
            if first:
                a_s[mi] = pv
            else:
                a_s[mi] = jnp.exp2(m_s[mi] - m_new) * a_s[mi] + pv
            m_s[mi] = m_new

    rescaling_chunk(0, True)

    def lazy_chunks(kjs, near):
        pv, drift = [None, None], None
        if not near:
            far_bias = jnp.where(kjs[-1] < qi, bias_ref[0, 0:1, :], bias_ref[4, 0:1, :])
        for kj in kjs:
            for mi in range(2):
                if near:
                    s, m_ref = scores(kj, mi), m_s[mi]
                else:
                    s, m_ref = _dot_nt(kb[kj], qs[mi]), m_s[mi] - far_bias
                acc = _dot(vt[kj], jnp.exp2(s - m_ref).astype(BF))
                pv[mi] = acc if pv[mi] is None else pv[mi] + acc
                d = jnp.max(s, axis=0, keepdims=True) - m_ref
                drift = d if drift is None else jnp.maximum(drift, d)
        over = jnp.max(drift) > ATTN_MAX_DRIFT

        @pl.when(jnp.logical_not(over))
        def _():
            for mi in range(2):
                a_s[mi] += pv[mi]

        @pl.when(over)
        def _():
            for kj in kjs:
                rescaling_chunk(kj, False)

    def near_or_far(kjs):
        is_near = (kjs[0] - 1 <= qi) & (qi <= kjs[-1] + 1)

        @pl.when(is_near)
        def _():
            lazy_chunks(kjs, True)

        @pl.when(jnp.logical_not(is_near))
        def _():
            lazy_chunks(kjs, False)

    group = ATTN_CHUNK_GROUP

    def grouped(i, c):
        near_or_far([1 + group * i + u for u in range(group)])
        return c

    ngroups = (nk - 1) // group
    lax.fori_loop(0, ngroups, grouped, 0)
    if (nk - 1) % group:
        near_or_far(list(range(1 + group * ngroups, nk)))
    a0, a1 = a_s[0], a_s[1]
    o = a0[:dv] / a0[dv:dv + 1] - lam_ref[0] * (a1[:dv] / a1[dv:dv + 1])
    o = o * lax.rsqrt(jnp.mean(o * o, 0, keepdims=True) + LN_EPS) * g_ref[...] * out_scale
    o_ref[...] = o.T


def _diff_attention(hproj, lam_full, sub_g, bias, lambda_init):
    B, L, _ = hproj.shape
    T = bias.shape[-1]
    H = A_HEADS
    return pl.pallas_call(
        functools.partial(_attn_kernel, T=T, nk=L // T, out_scale=1.0 - lambda_init),
        out_shape=jax.ShapeDtypeStruct((B, L, D_A), F32), grid=(B, H, L // T),
        in_specs=[pl.BlockSpec(memory_space=pltpu.SMEM),
                  pl.BlockSpec((None, T, LANES), lambda b, h, i: (b, i, h)),
                  pl.BlockSpec((None, L, LANES), lambda b, h, i: (b, 0, H + h)),
                  pl.BlockSpec((None, L, LANES), lambda b, h, i: (b, 0, 2 * H + h)),
                  pl.BlockSpec((None, 5, T, T), lambda b, h, i: (h, 0, 0, 0)),
                  pl.BlockSpec((A_DV, 1), lambda b, h, i: (0, 0))],
        out_specs=pl.BlockSpec((None, T, LANES), lambda b, h, i: (b, i, h)),
        scratch_shapes=[pltpu.VMEM((L // T, T, LANES), BF), pltpu.VMEM((L // T, A_DV + BF16_SUBLANES, T), BF),
                        pltpu.VMEM((2, 1, T), F32), pltpu.VMEM((2, A_DV + BF16_SUBLANES, T), F32)],
        compiler_params=_cp(("parallel", "parallel", "arbitrary")), name="diff_attn",
    )(lam_full.reshape(1), hproj, hproj, hproj, bias, sub_g.astype(F32).reshape(A_DV, 1))


def _short_conv_kernel(u_ref, w_ref, b_ref, o_ref):
    u = u_ref[...]
    L = u.shape[0]
    row = lax.broadcasted_iota(I32, u.shape, 0)
    up = jnp.where(row == 0, 0.0, pltpu.roll(u, 1, 0))
    dn = jnp.where(row == L - 1, 0.0, pltpu.roll(u, L - 1, 0))
    w = w_ref[...]
    o_ref[...] = up * w[0:1] + u * w[1:2] + dn * w[2:3] + b_ref[...]


def _short_conv(hproj, conv_w, conv_b):
    B, L, _ = hproj.shape
    nblk = 3 * D_B // LANES
    off = 3 * D_A // LANES
    per = D_B // LANES
    return pl.pallas_call(
        _short_conv_kernel, out_shape=jax.ShapeDtypeStruct((3, B, L, D_B), F32), grid=(B, nblk),
        in_specs=[pl.BlockSpec((None, L, LANES), lambda b, j: (b, 0, off + j)),
                  pl.BlockSpec((3, LANES), lambda b, j: (0, j)),
                  pl.BlockSpec((1, LANES), lambda b, j: (0, j))],
        out_specs=pl.BlockSpec((None, None, L, LANES), lambda b, j: (j // per, b, 0, j % per)),
        compiler_params=_cp(("parallel", "parallel")), name="short_conv",
    )(hproj, conv_w.astype(F32), conv_b.astype(F32).reshape(1, -1))


def _filter_mlp_kernel(z_ref, w1_ref, b1_ref, w2_ref, b2_ref, w3_ref, fq_ref, dec_ref, h_ref, s_ref):
    i = pl.program_id(0)
    z = z_ref[...]
    fq = fq_ref[...]
    h = jnp.sin(fq[0:1] * (_dot(z.astype(BF), w1_ref[...]) + b1_ref[...]))
    h = jnp.sin(fq[1:2] * (_dot(h.astype(BF), w2_ref[...]) + b2_ref[...]))
    t = z[:, 0:1]
    h = _dot(h.astype(BF), w3_ref[...]) * jnp.exp(-t * jnp.abs(dec_ref[...]))

    @pl.when(i == 0)
    def _():
        s_ref[...] = jnp.zeros_like(s_ref)

    s_ref[...] += jnp.sum(jnp.abs(h), axis=0, keepdims=True)
    row = lax.broadcasted_iota(I32, h.shape, 0)
    col = lax.broadcasted_iota(I32, h.shape, 1)
    half = h.shape[1] // 2
    h_ref[...] = jnp.where((row + i * h.shape[0] == 0) & (col >= half), 0.0, h)


def _filter_mlp(L, w1, b1, w2, b2, w3, freq, decay, tl=512):
    t = jnp.linspace(0.0, 1.0, L, dtype=F32)[:, None]
    wpos = 2.0 * math.pi * jnp.arange(L, dtype=F32)[:, None] / L
    fr = jnp.linspace(1e-4, FILTER_BANDS - 1, FILTER_BANDS, dtype=F32)[None, :]
    z = jnp.concatenate([t, jnp.cos(fr * wpos), -jnp.sin(fr * wpos)], -1)
    emb, width = w1.shape
    ch = w3.shape[1]
    z = jnp.pad(z, ((0, 0), (0, LANES - emb)))
    padw = LANES - width
    w1p = jnp.pad(w1, ((0, LANES - emb), (0, padw))).astype(BF)
    w2p = jnp.pad(w2, ((0, padw), (0, padw))).astype(BF)
    w3p = jnp.pad(w3, ((0, padw), (0, 0))).astype(BF)
    b1p = jnp.pad(b1.astype(F32), (0, padw)).reshape(1, LANES)
    b2p = jnp.pad(b2.astype(F32), (0, padw)).reshape(1, LANES)
    fqp = jnp.pad(freq.astype(F32), ((0, 0), (0, padw)))
    tl = min(tl, L)
    full = lambda shp: pl.BlockSpec(shp, lambda i: (0, 0))
    return pl.pallas_call(
        _filter_mlp_kernel,
        out_shape=(jax.ShapeDtypeStruct((L, ch), F32), jax.ShapeDtypeStruct((1, ch), F32)), grid=(L // tl,),
        in_specs=[pl.BlockSpec((tl, LANES), lambda i: (i, 0)), full((LANES, LANES)), full((1, LANES)),
                  full((LANES, LANES)), full((1, LANES)), full((LANES, ch)), full((2, LANES)), full((1, ch))],
        out_specs=(pl.BlockSpec((tl, ch), lambda i: (i, 0)), full((1, ch))),
        compiler_params=_cp(("arbitrary",)), name="filter_mlp",
    )(z, w1p, b1p, w2p, b2p, w3p, fqp, decay.astype(F32).reshape(1, ch))


def _cis(idx, n):
    ang = (2.0 * math.pi / n) * (idx % n).astype(F32)
    return jnp.cos(ang), -jnp.sin(ang)


def _blk(re, im):
    return jnp.concatenate([jnp.concatenate([re, -im], -1), jnp.concatenate([im, re], -1)], -2)


def _dft_tables(N):
    N2 = FFT_N2
    N1 = N // N2
    a = jnp.arange(N1)
    f1r, f1i = _cis(a[:, None] * a[None, :], N1)
    k = a[:, None, None] + N1 * jnp.arange(N2)[None, :, None]
    gr, gi = _cis(k * jnp.arange(N2)[None, None, :], N)
    return N1, f1r, f1i, gr, gi


def _mid_kernel(g_ref, gi_ref, kf_ref, a_ref, o_ref):
    n2, c = a_ref.shape[2], a_ref.shape[3]
    kr, ki = kf_ref[0], kf_ref[1]
    for p in range(a_ref.shape[0]):
        x = _dot(g_ref[...], a_ref[p].reshape(2 * n2, c).astype(BF))
        xr, xi = x[:n2], x[n2:]
        y = jnp.concatenate([xr * kr - xi * ki, xr * ki + xi * kr], axis=0).astype(BF)
        o_ref[p] = _dot(gi_ref[...], y).reshape(2, n2, c).astype(o_ref.dtype)


def _filt_mid_kernel(g_ref, a_ref, s_ref, o_ref):
    n2 = a_ref.shape[1]
    c = a_ref.shape[2] // 2
    x = _dot(g_ref[...], a_ref[...].reshape(2 * n2, 2 * c).astype(BF))
    s = s_ref[...]
    inv = 1.0 / (s[:, :c] + s[:, c:])
    o_ref[0] = (x[:n2, :c] + x[:n2, c:]) * inv
    o_ref[1] = (x[n2:, :c] - x[n2:, c:]) * inv


def _gate_inv_kernel(m_ref, d_ref, z_ref, gate_ref, sk_ref, o_ref, *, scale):
    conv = _dot(m_ref[...], d_ref[...].astype(BF)) * scale
    o_ref[...] = gate_ref[...] * (conv + z_ref[...] * sk_ref[...])


def _hyena(uc, filt, filt_sum, skip):
    _, B, L, C = uc.shape
    assert B % 2 == 0
    P = B // 2
    N = 2 * L
    N2 = FFT_N2
    N1, f1r, f1i, gr, gi = _dft_tables(N)
    h = N1 // 2
    cols = N2 * C
    m1 = jnp.concatenate([jnp.concatenate([f1r[:, :h], -f1i[:, :h]], 1),
                          jnp.concatenate([f1i[:, :h], f1r[:, :h]], 1)], 0).astype(BF)
    m1_real = jnp.concatenate([f1r[:, :h], f1i[:, :h]], 0).astype(BF)
    m1_inv = jnp.concatenate([jnp.concatenate([f1r[:h], f1i[:h]], 1),
                              jnp.concatenate([-f1i[:h], f1r[:h]], 1)], 0).astype(BF)
    g_fwd = _blk(gr, gi).astype(BF)
    g_inv = _blk(jnp.swapaxes(gr, 1, 2), -jnp.swapaxes(gi, 1, 2)).astype(BF)

    tc = min(cols, 4096)
    CF = filt.shape[1]
    fcols = N2 * CF
    tcf = min(fcols, 8192)
    fa = _mm(m1_real, filt.reshape(h, fcols), jax.ShapeDtypeStruct((2 * N1, fcols), BF), (fcols // tcf,),
             pl.BlockSpec((2 * N1, h), lambda j: (0, 0)), pl.BlockSpec((h, tcf), lambda j: (0, j)),
             pl.BlockSpec((2 * N1, tcf), lambda j: (0, j)), None, name="filt_s1")
    kf = pl.pallas_call(
        _filt_mid_kernel, out_shape=jax.ShapeDtypeStruct((N1, 2, N2, CF // 2), F32), grid=(N1,),
        in_specs=[pl.BlockSpec((None, 2 * N2, 2 * N2), lambda k: (k, 0, 0)),
                  pl.BlockSpec((2, None, N2, CF), lambda k: (0, k, 0, 0)),
                  pl.BlockSpec((1, CF), lambda k: (0, 0))],
        out_specs=pl.BlockSpec((None, 2, N2, CF // 2), lambda k: (k, 0, 0, 0)),
        compiler_params=_cp(("parallel",)), name="filt_mid",
    )(g_fwd, fa.reshape(2, N1, N2, CF), filt_sum)

    sk = jnp.tile(skip.astype(F32), (1, N2))
    z = uc[0].reshape(P, N1, cols)
    for n in range(B_ORDER):
        gate = uc[1 + n].reshape(P, N1, cols)
        a = _mm(m1, z, jax.ShapeDtypeStruct((P, 2 * N1, cols), BF), (P, cols // tc),
                pl.BlockSpec((2 * N1, N1), lambda p, j: (0, 0)), pl.BlockSpec((None, N1, tc), lambda p, j: (p, 0, j)),
                pl.BlockSpec((None, 2 * N1, tc), lambda p, j: (p, 0, j)), None, name="hy_s1")
        d = pl.pallas_call(
            _mid_kernel, out_shape=jax.ShapeDtypeStruct((P, 2, N1, N2, C), BF), grid=(N1,),
            in_specs=[pl.BlockSpec((None, 2 * N2, 2 * N2), lambda k: (k, 0, 0)),
                      pl.BlockSpec((None, 2 * N2, 2 * N2), lambda k: (k, 0, 0)),
                      pl.BlockSpec((None, 2, N2, C), lambda k: (k, 0, 0, n)),
                      pl.BlockSpec((P, 2, None, N2, C), lambda k: (0, 0, k, 0, 0))],
            out_specs=pl.BlockSpec((P, 2, None, N2, C), lambda k: (0, 0, k, 0, 0)),
            compiler_params=_cp(("parallel",)), name="hy_mid",
        )(g_fwd, g_inv, kf, a.reshape(P, 2, N1, N2, C))
        z = pl.pallas_call(
            functools.partial(_gate_inv_kernel, scale=1.0 / N),
            out_shape=jax.ShapeDtypeStruct((P, N1, cols), F32), grid=(P, cols // tc),
            in_specs=[pl.BlockSpec((N1, 2 * N1), lambda p, j: (0, 0)),
                      pl.BlockSpec((None, 2 * N1, tc), lambda p, j: (p, 0, j)),
                      pl.BlockSpec((None, N1, tc), lambda p, j: (p, 0, j)),
                      pl.BlockSpec((None, N1, tc), lambda p, j: (p, 0, j)),
                      pl.BlockSpec((1, tc), lambda p, j: (0, j))],
            out_specs=pl.BlockSpec((None, N1, tc), lambda p, j: (p, 0, j)),
            compiler_params=_cp(("parallel", "parallel")), name="hy_inv_gate",
        )(m1_inv, d.reshape(P, 2 * N1, cols), z, gate, sk[n:n + 1])
    return z.reshape(B, L, C)


def _fnet_tables(L, D):
    gw = D // C_GROUPS
    c = jnp.arange(gw)
    cr, ci = _cis(c[:, None] * c[None, :], gw)
    wch = jnp.concatenate([cr, ci], 1).astype(BF)
    N1, f1r, f1i, gr, gi = _dft_tables(L)
    m1 = _blk(f1r, f1i).astype(BF)
    g_re = jnp.concatenate([gr, -gi], -1).astype(BF)
    return N1, wch, m1, g_re


def _fnet_ch_kernel(x_ref, w_ref, o_ref):
    gw = x_ref.shape[1]
    res = _dot(x_ref[...].astype(BF), w_ref[...])
    o_ref[0] = res[:, :gw].astype(o_ref.dtype)
    o_ref[1] = res[:, gw:].astype(o_ref.dtype)


def _fnet_s1_kernel(m_ref, y_ref, o_ref):
    n1, tc = y_ref.shape[1], y_ref.shape[2]
    o_ref[...] = _dot(m_ref[...], y_ref[...].reshape(2 * n1, tc)).astype(o_ref.dtype)


def _fnet_s2_kernel(g_ref, a_ref, o_ref, *, scale):
    kb, n2 = g_ref.shape[0], g_ref.shape[1]
    d = a_ref.shape[3]
    for j in range(kb):
        res = _dot(g_ref[j], a_ref[:, j].reshape(2 * n2, d))
        o_ref[:, j, :] = res * scale


def _fourier_mix(x, B, L):
    T, D = x.shape
    N2 = FFT_N2
    N1, wch, m1, g_re = _fnet_tables(L, D)
    gw = D // C_GROUPS
    tm = min(1024, T)
    y = pl.pallas_call(
        _fnet_ch_kernel, out_shape=jax.ShapeDtypeStruct((2, T, D), BF), grid=(T // tm, C_GROUPS),
        in_specs=[pl.BlockSpec((tm, gw), lambda i, c: (i, c)), pl.BlockSpec((gw, 2 * gw), lambda i, c: (0, 0))],
        out_specs=pl.BlockSpec((2, tm, gw), lambda i, c: (0, i, c)),
        compiler_params=_cp(("parallel", "parallel")), name="fnet_ch")(x, wch)
    cols = N2 * D
    tc = min(cols, 8192)
    a = pl.pallas_call(
        _fnet_s1_kernel, out_shape=jax.ShapeDtypeStruct((B, 2 * N1, cols), BF), grid=(B, cols // tc),
        in_specs=[pl.BlockSpec((2 * N1, 2 * N1), lambda b, j: (0, 0)),
                  pl.BlockSpec((2, None, N1, tc), lambda b, j: (0, b, 0, j))],
        out_specs=pl.BlockSpec((None, 2 * N1, tc), lambda b, j: (b, 0, j)),
        compiler_params=_cp(("parallel", "parallel")), name="fnet_s1")(m1, y.reshape(2, B, N1, cols))
    kb = min(8, N1)
    f = pl.pallas_call(
        functools.partial(_fnet_s2_kernel, scale=1.0 / math.sqrt(L * gw)),
        out_shape=jax.ShapeDtypeStruct((B, N2, N1, D), F32), grid=(B, N1 // kb),
        in_specs=[pl.BlockSpec((kb, N2, 2 * N2), lambda b, k: (k, 0, 0)),
                  pl.BlockSpec((None, 2, kb, N2, D), lambda b, k: (b, 0, k, 0, 0))],
        out_specs=pl.BlockSpec((None, N2, kb, D), lambda b, k: (b, 0, k, 0)),
        compiler_params=_cp(("parallel", "parallel")), name="fnet_s2")(g_re, a.reshape(B, 2, N1, N2, D))
    return f.reshape(T, D)


def _router_kernel(w_ref, x_ref, o_ref):
    logits = _dot_nt(w_ref[...], x_ref[...].astype(BF))
    m = jnp.max(logits, axis=0, keepdims=True)
    p = jnp.exp(logits - m)
    o_ref[...] = p / jnp.sum(p, axis=0, keepdims=True)


def _router(x, w_router_t, tm=1024):
    T, D = x.shape
    E = w_router_t.shape[0]
    tm = min(tm, T)
    return pl.pallas_call(
        _router_kernel, out_shape=jax.ShapeDtypeStruct((E, T), F32), grid=(T // tm,),
        in_specs=[pl.BlockSpec((E, D), lambda i: (0, 0)), pl.BlockSpec((tm, D), lambda i: (i, 0))],
        out_specs=pl.BlockSpec((E, tm), lambda i: (0, i)), compiler_params=_cp(("parallel",)), name="router")(
            w_router_t, x)


def _prefix_counts(mask_f, upper, lower):
    rowcs = _dot(mask_f.astype(BF), upper)
    tot = jnp.broadcast_to(rowcs[:, LANES - 1:LANES], mask_f.shape)
    offs = _dot(lower, tot.astype(BF))
    return offs + rowcs - mask_f, offs


def _select_kernel(a_ref, rank_ref, off_ref, *, cap):
    v = a_ref[...]
    R = v.shape[0]
    bits = pltpu.bitcast(v, I32)

    def step(i, thr):
        cand = thr | (jnp.int32(1) << (30 - i))
        cnt = jnp.sum((bits >= cand).astype(F32))
        return jnp.where(cnt >= cap, cand, thr)

    thr = lax.fori_loop(0, 31, step, jnp.int32(0))
    gt = bits > thr
    eq = bits == thr
    need = cap - jnp.sum(gt.astype(F32))
    li = lax.broadcasted_iota(I32, (LANES, LANES), 0)
    lj = lax.broadcasted_iota(I32, (LANES, LANES), 1)
    upper = (li <= lj).astype(BF)
    ri = lax.broadcasted_iota(I32, (R, R), 0)
    rj = lax.broadcasted_iota(I32, (R, R), 1)
    lower = (rj < ri).astype(BF)
    eq_rank, _ = _prefix_counts(eq.astype(F32), upper, lower)
    sel = gt | (eq & (eq_rank < need))
    rank, offs = _prefix_counts(sel.astype(F32), upper, lower)
    rank_ref[...] = jnp.where(sel, rank.astype(I32), -1)
    off_ref[...] = offs.astype(I32)


def _select(aff, cap):
    E, T = aff.shape
    R = T // LANES
    spec = pl.BlockSpec((None, R, LANES), lambda e: (e, 0, 0))
    return pl.pallas_call(
        functools.partial(_select_kernel, cap=cap),
        out_shape=(jax.ShapeDtypeStruct((E, R, LANES), I32), jax.ShapeDtypeStruct((E, R, LANES), I32)),
        grid=(E,), in_specs=[spec], out_specs=(spec, spec), compiler_params=_cp(("parallel",)), name="select")(
            aff.reshape(E, R, LANES))


def _compact_kernel(off_ref, rank_ref, idx_ref, *, R):
    e = pl.program_id(0)
    idx_ref[...] = jnp.zeros_like(idx_ref)
    sub = lax.broadcasted_iota(I32, (8, LANES), 0)
    lane = lax.broadcasted_iota(I32, (8, LANES), 1)
    lhs = jnp.where(sub == 0, lane, jnp.where(sub == 1, 1, 0)).astype(BF)
    slot = lax.broadcasted_iota(I32, (2 * LANES, LANES), 0)

    def chunk(r, c):
        a = off_ref[e, r] // LANES
        local = rank_ref[pl.ds(r, 1), :] - a * LANES
        onehot = (slot == local).astype(BF)
        res = _dot_nt(lhs, onehot)
        val = (res[0:1] + res[1:2] * lax.convert_element_type(r * LANES, F32)).astype(I32)
        idx_ref[pl.ds(a, 1), :] += val[:, :LANES]
        idx_ref[pl.ds(a + 1, 1), :] += val[:, LANES:]
        return c

    lax.fori_loop(0, R, chunk, 0, unroll=8)


def _compact(rank, rowoff, cap):
    E, R, _ = rank.shape
    nrow = cap // LANES + 2
    return pl.pallas_call(
        functools.partial(_compact_kernel, R=R),
        out_shape=jax.ShapeDtypeStruct((E, nrow, LANES), I32),
        grid_spec=pltpu.PrefetchScalarGridSpec(
            num_scalar_prefetch=1, grid=(E,),
            in_specs=[pl.BlockSpec((None, R, LANES), lambda e, off: (e, 0, 0))],
            out_specs=pl.BlockSpec((None, nrow, LANES), lambda e, off: (e, 0, 0))),
        compiler_params=_cp(("arbitrary",)), name="compact")(rowoff, rank)


def _ffn_kernel(idx_ref, idxn_ref, x_hbm, wg_ref, wu_ref, wd_ref, o_ref, xbuf, xb, acc, sem, *, tm, nf, ntiles):
    f = pl.program_id(2)
    tile = pl.program_id(0) * pl.num_programs(1) + pl.program_id(1)
    slot = tile % 2
    part = tm // nf

    def row_copy(t, r, slot_):
        return pltpu.make_async_copy(x_hbm.at[pl.ds(t, 1)], xbuf.at[slot_, pl.ds(r, 1)], sem.at[slot_])

    def wait_tile(slot_):
        pltpu.make_async_copy(x_hbm.at[pl.ds(0, tm)], xbuf.at[slot_], sem.at[slot_]).wait()

    @pl.when((tile == 0) & (f == 0))
    def _():
        def issue(r, c):
            row_copy(idx_ref[0, 0, r], r, 0).start()
            return c

        lax.fori_loop(0, tm, issue, 0)

    @pl.when(f == 0)
    def _():
        wait_tile(slot)
        xb[...] = xbuf[slot].astype(BF)
        if nf > 1:
            acc[...] = jnp.zeros_like(acc)

    base = f * part
    for j in range(part):
        row_copy(idxn_ref[0, 0, base + j], base + j, 1 - slot).start()

    x = xb[...]
    g = _dot(x, wg_ref[...])
    u = _dot(x, wu_ref[...])
    hcur = (g * jax.nn.sigmoid(g)) * u
    down = _dot(hcur.astype(BF), wd_ref[...])
    if nf == 1:
        o_ref[...] = down.astype(o_ref.dtype)
    else:
        acc[...] += down

        @pl.when(f == nf - 1)
        def _():
            o_ref[...] = acc[...].astype(o_ref.dtype)

    @pl.when((tile == ntiles - 1) & (f == nf - 1))
    def _():
        wait_tile(1 - slot)


def _ffn(x, idx, wg, wu, wd, layer, cap, tm=512):
    T, D = x.shape
    E = wg.shape[1]
    FF = wg.shape[3]
    tm = min(tm, cap)
    nt = cap // tm
    fc, nf = FF, 1
    resident = pl.Buffered(1)
    idx3 = idx[:, :cap // LANES, :].reshape(E * nt, 1, tm)
    last = E * nt - 1
    return pl.pallas_call(
        functools.partial(_ffn_kernel, tm=tm, nf=nf, ntiles=E * nt),
        out_shape=jax.ShapeDtypeStruct((E, cap, D), BF), grid=(E, nt, nf),
        in_specs=[pl.BlockSpec((1, 1, tm), lambda e, i, f: (e * nt + i, 0, 0), memory_space=pltpu.SMEM),
                  pl.BlockSpec((1, 1, tm), lambda e, i, f: (jnp.minimum(e * nt + i + 1, last), 0, 0),
                               memory_space=pltpu.SMEM),
                  pl.BlockSpec(memory_space=pl.ANY),
                  pl.BlockSpec((None, None, D, fc), lambda e, i, f: (layer, e, 0, f), pipeline_mode=resident),
                  pl.BlockSpec((None, None, D, fc), lambda e, i, f: (layer, e, 0, f), pipeline_mode=resident),
                  pl.BlockSpec((None, None, fc, D), lambda e, i, f: (layer, e, f, 0), pipeline_mode=resident)],
        out_specs=pl.BlockSpec((None, tm, D), lambda e, i, f: (e, i, 0)),
        scratch_shapes=[pltpu.VMEM((2, tm, D), F32), pltpu.VMEM((tm, D), BF),
                        pltpu.VMEM((tm, D) if nf > 1 else (8, LANES), F32), pltpu.SemaphoreType.DMA((2,))],
        compiler_params=_cp(("arbitrary", "arbitrary", "arbitrary")), name="expert_ffn")(idx3, idx3, x, wg, wu, wd)


COMBINE_WIN = 64


def _combine_kernel(off_ref, rank_ref, aff_ref, x_ref, g_ref, b_ref, out_hbm, o_ref, wins, winx, ysc, sem, semx,
                    *, tt, E, cap, rpt, ntile, alpha):
    W = COMBINE_WIN
    ti = pl.program_id(0)
    slot = ti % 2

    def start0(e, ti_):
        r0 = off_ref[e, ti_ * rpt]
        return pl.multiple_of(jnp.minimum((r0 // 16) * 16, cap - W), 16)

    def win_copy(e, ti_, slot_):
        return pltpu.make_async_copy(out_hbm.at[e, pl.ds(start0(e, ti_), W)], wins.at[slot_, pl.ds(e * W, W)],
                                     sem.at[slot_])

    @pl.when(ti == 0)
    def _():
        for e in range(E):
            win_copy(e, 0, 0).start()

    @pl.when(ti + 1 < ntile)
    def _():
        for e in range(E):
            win_copy(e, ti + 1, 1 - slot).start()

    sub = lax.broadcasted_iota(I32, (W, tt), 0)

    def gated_onehot(r_row, g_row, first_row, min_rank):
        g_hi = g_row.astype(BF).astype(F32)
        match = (sub == r_row - first_row) & (r_row >= min_rank)
        return jnp.concatenate([jnp.where(match, g_hi, 0.0), jnp.where(match, g_row - g_hi, 0.0)], axis=1).astype(BF)

    def scatter_rows(p_t, rows):
        y2 = lax.dot_general(p_t, rows, (((0,), (0,)), ((), ())), preferred_element_type=F32)
        return y2[:tt] + y2[tt:]

    p_t = jnp.concatenate([gated_onehot(rank_ref[e:e + 1, :], aff_ref[e:e + 1, :], start0(e, ti), 0)
                           for e in range(E)], axis=0)
    for e in range(E):
        win_copy(e, ti, slot).wait()
    ysc[...] = alpha * x_ref[...] + scatter_rows(p_t, wins[slot])

    def extra(e, c):
        s0 = start0(e, ti)
        r_end = off_ref[e, (ti + 1) * rpt]

        @pl.when(r_end > s0 + W)
        def _():
            r_row = rank_ref[pl.ds(e, 1), :]
            g_row = aff_ref[pl.ds(e, 1), :]
            for w in range(1, tt // W + 1):
                lo_w = s0 + w * W

                @pl.when(r_end > lo_w)
                def _():
                    sw = pl.multiple_of(jnp.minimum(lo_w, cap - W), 16)
                    cp = pltpu.make_async_copy(out_hbm.at[e, pl.ds(sw, W)], winx, semx)
                    cp.start()
                    cp.wait()
                    ysc[...] += scatter_rows(gated_onehot(r_row, g_row, sw, lo_w), winx[...])

        return c

    lax.fori_loop(0, E, extra, 0)
    o_ref[...] = _ln_rows(ysc[...], g_ref[...], b_ref[...])


def _combine_ln(out, rank, aff, rowoff, x, g, b, cap, alpha):
    T, D = x.shape
    E = out.shape[0]
    W = COMBINE_WIN
    assert cap >= W
    tt = min(256, cap // 2)
    rpt = tt // LANES
    ntile = T // tt
    kern = functools.partial(_combine_kernel, tt=tt, E=E, cap=cap, rpt=rpt, ntile=ntile, alpha=alpha)
    return pl.pallas_call(
        kern, out_shape=jax.ShapeDtypeStruct((T, D), F32),
        grid_spec=pltpu.PrefetchScalarGridSpec(
            num_scalar_prefetch=1, grid=(ntile,),
            in_specs=[pl.BlockSpec((E, tt), lambda i, off: (0, i)),
                      pl.BlockSpec((E, tt), lambda i, off: (0, i)),
                      pl.BlockSpec((tt, D), lambda i, off: (i, 0)),
                      pl.BlockSpec((1, D), lambda i, off: (0, 0)),
                      pl.BlockSpec((1, D), lambda i, off: (0, 0)),
                      pl.BlockSpec(memory_space=pl.ANY)],
            out_specs=pl.BlockSpec((tt, D), lambda i, off: (i, 0)),
            scratch_shapes=[pltpu.VMEM((2, E * W, D), BF), pltpu.VMEM((W, D), BF), pltpu.VMEM((tt, D), F32),
                            pltpu.SemaphoreType.DMA((2,)), pltpu.SemaphoreType.DMA]),
        compiler_params=_cp(("arbitrary",)), name="combine_ln",
    )(rowoff, rank, aff, x, g.reshape(1, D), b.reshape(1, D), out)


def _expert_choice_ln(x, w_router_t, wg, wu, wd, layer, g, b, alpha):
    T, D = x.shape
    E = N_EXPERTS
    cap = max(1, EC_FACTOR * T // E)
    aff = _router(x, w_router_t)
    rank, rowoff = _select(aff, cap)
    rowoff = jnp.concatenate([rowoff[:, :, 0], jnp.full((E, 1), cap, I32)], axis=1)
    idx = _compact(rank, rowoff, cap)
    out = _ffn(x, idx, wg, wu, wd, layer, cap)
    return _combine_ln(out, rank.reshape(E, T), aff, rowoff, x, g, b, cap, alpha)


def _trunk(x, p, depth):
    B, L, D = x.shape
    T = B * L
    alpha = (2 * depth) ** 0.25
    x = x.reshape(T, D)
    filt_cache = {}
    for layer in range(depth):
        j = layer // 2
        if layer % 2 == 0:
            hproj = _xw(x, p["ab_w_in"][j], name="ab_in").reshape(B, L, -1)
            lambda_init = 0.8 - 0.6 * math.exp(-0.3 * layer)
            lf = p["diff_lambda"][j].astype(F32)
            lam_full = jnp.exp(jnp.sum(lf[0] * lf[1])) - jnp.exp(jnp.sum(lf[2] * lf[3])) + lambda_init
            t_attn = min(ATTN_TILE, L)
            if t_attn not in p["bias_tiles"]:
                p["bias_tiles"][t_attn] = _bias_tiles(p["rel_bias"], t_attn)
            a_out = _diff_attention(hproj, lam_full, p["diff_subln_g"][j], p["bias_tiles"][t_attn], lambda_init)
            uc = _short_conv(hproj, p["hy_conv_w"][j], p["hy_conv_b"][j])
            filt, filt_sum = _filter_mlp(L, p["hy_f_w1"][j], p["hy_f_b1"][j], p["hy_f_w2"][j], p["hy_f_b2"][j],
                                         p["hy_f_w3"][j], p["hy_f_freq"][j], p["hy_decay"][j])
            b_out = _hyena(uc, filt, filt_sum, p["hy_skip"][j])
            w_out = p["ab_w_out"][j]
            pairs = [(a_out.reshape(T, D_A), w_out[:D_A]), (b_out.reshape(T, D_B), w_out[D_A:])]
        else:
            pairs = [(_fourier_mix(x, B, L), p["c_w_out"][j])]
        x = _mm_res_ln(pairs, x, p["ln_g"][layer, 0], p["ln_b"][layer, 0], alpha)
        x = _expert_choice_ln(x, p["ec_router_t"][layer], p["ec_w_gate"], p["ec_w_up"], p["ec_w_down"], layer,
                              p["ln_g"][layer, 1], p["ln_b"][layer, 1], alpha)
    return x.reshape(B, L, D)


def kernel(x_prompt, x_sample, rel_bias, ab_w_in, ab_w_out, diff_lambda, diff_subln_g, hy_conv_w, hy_conv_b,
           hy_f_w1, hy_f_b1, hy_f_w2, hy_f_b2, hy_f_w3, hy_f_freq, hy_decay, hy_skip,
           c_w_out, ec_router, ec_w_gate, ec_w_up, ec_w_down, ln_g, ln_b):
    depth = ec_router.shape[0]
    p = dict(
        rel_bias=rel_bias, ab_w_in=ab_w_in.astype(BF), ab_w_out=ab_w_out.astype(BF), diff_lambda=diff_lambda,
        diff_subln_g=diff_subln_g, hy_conv_w=hy_conv_w, hy_conv_b=hy_conv_b, hy_f_w1=hy_f_w1, hy_f_b1=hy_f_b1,
        hy_f_w2=hy_f_w2, hy_f_b2=hy_f_b2, hy_f_w3=hy_f_w3, hy_f_freq=hy_f_freq, hy_decay=hy_decay, hy_skip=hy_skip,
        c_w_out=c_w_out.astype(BF), ec_router_t=jnp.swapaxes(ec_router, 1, 2).astype(BF),
        ec_w_gate=ec_w_gate.astype(BF), ec_w_up=ec_w_up.astype(BF), ec_w_down=ec_w_down.astype(BF),
        ln_g=ln_g.astype(F32), ln_b=ln_b.astype(F32), bias_tiles={})
    return _trunk(x_prompt, p, depth), _trunk(x_sample, p, depth)
```

```python
import functools
import math

import jax
import jax.numpy as jnp
from jax import lax
from jax.experimental import pallas as pl
from jax.experimental.pallas import tpu as pltpu

BF = jnp.bfloat16
F32 = jnp.float32
I32 = jnp.int32

A_HEADS = 4
A_DV = 128
A_DK = 64
D_A = 512
D_B = 512
B_ORDER = 2
REL_BUCKETS = 32
REL_MAX_DIST = 128
FILTER_BANDS = 16
N_EXPERTS = 16
EC_FACTOR = 2
LN_EPS = 1e-5
C_GROUPS = 4
LANES = 128
BF16_SUBLANES = 16
FFT_N2 = 128
VMEM_LIMIT = 52 * 1024 * 1024


def _cp(sem, vmem=VMEM_LIMIT):
    return pltpu.CompilerParams(dimension_semantics=sem, vmem_limit_bytes=vmem)


def _dot(a, b):
    return jnp.dot(a, b, preferred_element_type=F32)


def _dot_nt(a, b):
    return lax.dot_general(a, b, (((1,), (1,)), ((), ())), preferred_element_type=F32)


def _mm_kernel(a_ref, b_ref, o_ref, *scratch, nk, kaxis, scale):
    prod = _dot(a_ref[...].astype(BF), b_ref[...].astype(BF))
    if nk == 1:
        o_ref[...] = (prod * scale).astype(o_ref.dtype)
        return
    acc = scratch[0]
    k = pl.program_id(kaxis)

    @pl.when(k == 0)
    def _():
        acc[...] = jnp.zeros_like(acc)

    acc[...] += prod

    @pl.when(k == nk - 1)
    def _():
        o_ref[...] = (acc[...] * scale).astype(o_ref.dtype)


def _mm(a, b, out_shape, grid, a_spec, b_spec, o_spec, mn, *, nk=1, kaxis=None, scale=1.0, name):
    sem = tuple("arbitrary" if i == kaxis else "parallel" for i in range(len(grid)))
    return pl.pallas_call(
        functools.partial(_mm_kernel, nk=nk, kaxis=kaxis, scale=scale),
        out_shape=out_shape, grid=grid, in_specs=[a_spec, b_spec], out_specs=o_spec,
        scratch_shapes=[pltpu.VMEM(mn, F32)] if nk > 1 else [],
        compiler_params=_cp(sem), name=name)(a, b)


def _xw(x, w, tm=512, name="xw"):
    M, K = x.shape
    N = w.shape[1]
    tm = min(tm, M)
    return _mm(x, w, jax.ShapeDtypeStruct((M, N), F32), (M // tm,),
               pl.BlockSpec((tm, K), lambda i: (i, 0)), pl.BlockSpec((K, N), lambda i: (0, 0)),
               pl.BlockSpec((tm, N), lambda i: (i, 0)), (tm, N), name=name)


def _ln_rows(y, g, b):
    mu = jnp.mean(y, -1, keepdims=True)
    d = y - mu
    var = jnp.mean(d * d, -1, keepdims=True)
    return d * lax.rsqrt(var + LN_EPS) * g + b


def _mm_res_ln_kernel(*refs, npairs, alpha):
    x_ref, g_ref, b_ref = refs[2 * npairs:2 * npairs + 3]
    o_ref = refs[2 * npairs + 3]
    y = alpha * x_ref[...]
    for i in range(npairs):
        y = y + _dot(refs[2 * i][...].astype(BF), refs[2 * i + 1][...])
    o_ref[...] = _ln_rows(y, g_ref[...], b_ref[...])


def _mm_res_ln(pairs, x, g, b, alpha, tm=512, name="mm_res_ln"):
    M, D = x.shape
    tm = min(tm, M)
    specs, args = [], []
    for a, w in pairs:
        specs += [pl.BlockSpec((tm, a.shape[1]), lambda i: (i, 0)), pl.BlockSpec(w.shape, lambda i: (0, 0))]
        args += [a, w]
    specs += [pl.BlockSpec((tm, D), lambda i: (i, 0)), pl.BlockSpec((1, D), lambda i: (0, 0)),
              pl.BlockSpec((1, D), lambda i: (0, 0))]
    args += [x, g.reshape(1, D), b.reshape(1, D)]
    return pl.pallas_call(
        functools.partial(_mm_res_ln_kernel, npairs=len(pairs), alpha=alpha),
        out_shape=jax.ShapeDtypeStruct((M, D), F32), grid=(M // tm,), in_specs=specs,
        out_specs=pl.BlockSpec((tm, D), lambda i: (i, 0)), compiler_params=_cp(("parallel",)), name=name)(*args)


def _rel_bucket(rel):
    nb = REL_BUCKETS // 2
    max_exact = nb // 2
    n = jnp.abs(rel)
    large = max_exact + (jnp.log(jnp.maximum(n, 1).astype(F32) / max_exact)
                         / math.log(REL_MAX_DIST / max_exact) * (nb - max_exact)).astype(I32)
    large = jnp.minimum(large, nb - 1)
    return jnp.where(rel > 0, nb, 0) + jnp.where(n < max_exact, n, large)


LOG2E = 1.4426950408889634


def _bias_tiles(rel_table, T):
    assert T >= REL_MAX_DIST
    i = jnp.arange(T)
    rel = (jnp.arange(-2, 3)[:, None, None] * T + i[None, :, None]) - i[None, None, :]
    bucket = _rel_bucket(rel)[None]
    tab = LOG2E * rel_table.astype(F32).T[:, :, None, None, None]
    out = jnp.zeros((rel_table.shape[1],) + rel.shape, F32)
    for b in range(REL_BUCKETS):
        out = jnp.where(bucket == b, tab[:, b], out)
    return out


ATTN_MAX_DRIFT = 40.0
ATTN_CHUNK_GROUP = 3
ATTN_TILE = 512


def _attn_kernel(lam_ref, q_ref, k_ref, v_ref, bias_ref, g_ref, o_ref, kb, vt, m_s, a_s, *, T, nk, out_scale):
    qi = pl.program_id(2)
    dv = A_DV

    @pl.when(qi == 0)
    def _():
        def prep(c, carry):
            st = pl.multiple_of(c * T, T)
            kb[c] = k_ref[pl.ds(st, T), :].astype(BF)
            vt[c, :dv] = v_ref[pl.ds(st, T), :].T.astype(BF)
            vt[c, dv:] = jnp.ones((vt.shape[1] - dv, T), BF)
            return carry

        lax.fori_loop(0, nk, prep, 0)

    q = q_ref[...] * (A_DK ** -0.5 * LOG2E)
    lane = lax.broadcasted_iota(I32, q.shape, 1)
    qs = (jnp.where(lane < A_DK, q, 0.0).astype(BF), jnp.where(lane >= A_DK, q, 0.0).astype(BF))

    def scores(kj, mi):
        return _dot_nt(kb[kj], qs[mi]) + bias_ref[jnp.clip(kj - qi, -2, 2) + 2]

    def rescaling_chunk(kj, first):
        for mi in range(2):
            s = scores(kj, mi)
            m_new = jnp.max(s, axis=0, keepdims=True)
            if not first:
                m_new = jnp.maximum(m_s[mi], m_new)
            pv = _dot(vt[kj], jnp.exp2(s - m_new).astype(BF))
            if first:
                a_s[mi] = pv
            else:
                a_s[mi] = jnp.exp2(m_s[mi] - m_new) * a_s[mi] + pv
            m_s[mi] = m_new

    rescaling_chunk(0, True)

    def lazy_chunks(kjs, near):
        pv, drift = [None, None], None
        if not near:
            far_bias = jnp.where(kjs[-1] < qi, bias_ref[0, 0:1, :], bias_ref[4, 0:1, :])
        for kj in kjs:
            for mi in range(2):
                if near:
                    s, m_ref = scores(kj, mi), m_s[mi]
                else:
                    s, m_ref = _dot_nt(kb[kj], qs[mi]), m_s[mi] - far_bias
                acc = _dot(vt[kj], jnp.exp2(s - m_ref).astype(BF))
                pv[mi] = acc if pv[mi] is None else pv[mi] + acc
                d = jnp.max(s, axis=0, keepdims=True) - m_ref
                drift = d if drift is None else jnp.maximum(drift, d)
        over = jnp.max(drift) > ATTN_MAX_DRIFT

        @pl.when(jnp.logical_not(over))
        def _():
            for mi in range(2):
                a_s[mi] += pv[mi]

        @pl.when(over)
        def _():
            for kj in kjs:
                rescaling_chunk(kj, False)

    def near_or_far(kjs):
        is_near = (kjs[0] - 1 <= qi) & (qi <= kjs[-1] + 1)

        @pl.when(is_near)
        def _():
            lazy_chunks(kjs, True)

        @pl.when(jnp.logical_not(is_near))
        def _():
            lazy_chunks(kjs, False)

    group = ATTN_CHUNK_GROUP

    def grouped(i, c):
        near_or_far([1 + group * i + u for u in range(group)])
        return c

    ngroups = (nk - 1) // group
    lax.fori_loop(0, ngroups, grouped, 0)
    if (nk - 1) % group:
        near_or_far(list(range(1 + group * ngroups, nk)))
    a0, a1 = a_s[0], a_s[1]
    o = a0[:dv] / a0[dv:dv + 1] - lam_ref[0] * (a1[:dv] / a1[dv:dv + 1])
    o = o * lax.rsqrt(jnp.mean(o * o, 0, keepdims=True) + LN_EPS) * g_ref[...] * out_scale
    o_ref[...] = o.T


def _diff_attention(hproj, lam_full, sub_g, bias, lambda_init):
    B, L, _ = hproj.shape
    T = bias.shape[-1]
    H = A_HEADS
    return pl.pallas_call(
        functools.partial(_attn_kernel, T=T, nk=L // T, out_scale=1.0 - lambda_init),
        out_shape=jax.ShapeDtypeStruct((B, L, D_A), F32), grid=(B, H, L // T),
        in_specs=[pl.BlockSpec(memory_space=pltpu.SMEM),
                  pl.BlockSpec((None, T, LANES), lambda b, h, i: (b, i, h)),
                  pl.BlockSpec((None, L, LANES), lambda b, h, i: (b, 0, H + h)),
                  pl.BlockSpec((None, L, LANES), lambda b, h, i: (b, 0, 2 * H + h)),
                  pl.BlockSpec((None, 5, T, T), lambda b, h, i: (h, 0, 0, 0)),
                  pl.BlockSpec((A_DV, 1), lambda b, h, i: (0, 0))],
        out_specs=pl.BlockSpec((None, T, LANES), lambda b, h, i: (b, i, h)),
        scratch_shapes=[pltpu.VMEM((L // T, T, LANES), BF), pltpu.VMEM((L // T, A_DV + BF16_SUBLANES, T), BF),
                        pltpu.VMEM((2, 1, T), F32), pltpu.VMEM((2, A_DV + BF16_SUBLANES, T), F32)],
        compiler_params=_cp(("parallel", "parallel", "arbitrary")), name="diff_attn",
    )(lam_full.reshape(1), hproj, hproj, hproj, bias, sub_g.astype(F32).reshape(A_DV, 1))


def _short_conv_kernel(u_ref, w_ref, b_ref, o_ref):
    u = u_ref[...]
    L = u.shape[0]
    row = lax.broadcasted_iota(I32, u.shape, 0)
    up = jnp.where(row == 0, 0.0, pltpu.roll(u, 1, 0))
    dn = jnp.where(row == L - 1, 0.0, pltpu.roll(u, L - 1, 0))
    w = w_ref[...]
    o_ref[...] = up * w[0:1] + u * w[1:2] + dn * w[2:3] + b_ref[...]


def _short_conv(hproj, conv_w, conv_b):
    B, L, _ = hproj.shape
    nblk = 3 * D_B // LANES
    off = 3 * D_A // LANES
    per = D_B // LANES
    return pl.pallas_call(
        _short_conv_kernel, out_shape=jax.ShapeDtypeStruct((3, B, L, D_B), F32), grid=(B, nblk),
        in_specs=[pl.BlockSpec((None, L, LANES), lambda b, j: (b, 0, off + j)),
                  pl.BlockSpec((3, LANES), lambda b, j: (0, j)),
                  pl.BlockSpec((1, LANES), lambda b, j: (0, j))],
        out_specs=pl.BlockSpec((None, None, L, LANES), lambda b, j: (j // per, b, 0, j % per)),
        compiler_params=_cp(("parallel", "parallel")), name="short_conv",
    )(hproj, conv_w.astype(F32), conv_b.astype(F32).reshape(1, -1))


def _filter_mlp_kernel(z_ref, w1_ref, b1_ref, w2_ref, b2_ref, w3_ref, fq_ref, dec_ref, h_ref, s_ref):
    i = pl.program_id(0)
    z = z_ref[...]
    fq = fq_ref[...]
    h = jnp.sin(fq[0:1] * (_dot(z.astype(BF), w1_ref[...]) + b1_ref[...]))
    h = jnp.sin(fq[1:2] * (_dot(h.astype(BF), w2_ref[...]) + b2_ref[...]))
    t = z[:, 0:1]
    h = _dot(h.astype(BF), w3_ref[...]) * jnp.exp(-t * jnp.abs(dec_ref[...]))

    @pl.when(i == 0)
    def _():
        s_ref[...] = jnp.zeros_like(s_ref)

    s_ref[...] += jnp.sum(jnp.abs(h), axis=0, keepdims=True)
    row = lax.broadcasted_iota(I32, h.shape, 0)
    col = lax.broadcasted_iota(I32, h.shape, 1)
    half = h.shape[1] // 2
    h_ref[...] = jnp.where((row + i * h.shape[0] == 0) & (col >= half), 0.0, h)


def _filter_mlp(L, w1, b1, w2, b2, w3, freq, decay, tl=512):
    t = jnp.linspace(0.0, 1.0, L, dtype=F32)[:, None]
    wpos = 2.0 * math.pi * jnp.arange(L, dtype=F32)[:, None] / L
    fr = jnp.linspace(1e-4, FILTER_BANDS - 1, FILTER_BANDS, dtype=F32)[None, :]
    z = jnp.concatenate([t, jnp.cos(fr * wpos), -jnp.sin(fr * wpos)], -1)
    emb, width = w1.shape
    ch = w3.shape[1]
    z = jnp.pad(z, ((0, 0), (0, LANES - emb)))
    padw = LANES - width
    w1p = jnp.pad(w1, ((0, LANES - emb), (0, padw))).astype(BF)
    w2p = jnp.pad(w2, ((0, padw), (0, padw))).astype(BF)
    w3p = jnp.pad(w3, ((0, padw), (0, 0))).astype(BF)
    b1p = jnp.pad(b1.astype(F32), (0, padw)).reshape(1, LANES)
    b2p = jnp.pad(b2.astype(F32), (0, padw)).reshape(1, LANES)
    fqp = jnp.pad(freq.astype(F32), ((0, 0), (0, padw)))
    tl = min(tl, L)
    full = lambda shp: pl.BlockSpec(shp, lambda i: (0, 0))
    return pl.pallas_call(
        _filter_mlp_kernel,
        out_shape=(jax.ShapeDtypeStruct((L, ch), F32), jax.ShapeDtypeStruct((1, ch), F32)), grid=(L // tl,),
        in_specs=[pl.BlockSpec((tl, LANES), lambda i: (i, 0)), full((LANES, LANES)), full((1, LANES)),
                  full((LANES, LANES)), full((1, LANES)), full((LANES, ch)), full((2, LANES)), full((1, ch))],
        out_specs=(pl.BlockSpec((tl, ch), lambda i: (i, 0)), full((1, ch))),
        compiler_params=_cp(("arbitrary",)), name="filter_mlp",
    )(z, w1p, b1p, w2p, b2p, w3p, fqp, decay.astype(F32).reshape(1, ch))


def _cis(idx, n):
    ang = (2.0 * math.pi / n) * (idx % n).astype(F32)
    return jnp.cos(ang), -jnp.sin(ang)


def _blk(re, im):
    return jnp.concatenate([jnp.concatenate([re, -im], -1), jnp.concatenate([im, re], -1)], -2)


def _dft_tables(N):
    N2 = FFT_N2
    N1 = N // N2
    a = jnp.arange(N1)
    b = jnp.arange(N2)
    f1r, f1i = _cis(a[:, None] * a[None, :], N1)
    tr, ti = _cis(a[:, None] * b[None, :], N)
    f2r, f2i = _cis(b[:, None] * b[None, :], N2)
    gr = tr[:, None, :] * f2r[None] - ti[:, None, :] * f2i[None]
    gi = tr[:, None, :] * f2i[None] + ti[:, None, :] * f2r[None]
    return N1, f1r, f1i, gr, gi


def _mid_kernel(g_ref, gi_ref, kf_ref, a_ref, o_ref):
    n2, c = a_ref.shape[2], a_ref.shape[3]
    kr, ki = kf_ref[0], kf_ref[1]
    for p in range(a_ref.shape[0]):
        x = _dot(g_ref[...], a_ref[p].reshape(2 * n2, c).astype(BF))
        xr, xi = x[:n2], x[n2:]
        y = jnp.concatenate([xr * kr - xi * ki, xr * ki + xi * kr], axis=0).astype(BF)
        o_ref[p] = _dot(gi_ref[...], y).reshape(2, n2, c).astype(o_ref.dtype)


def _filt_mid_kernel(g_ref, a_ref, s_ref, o_ref):
    n2 = a_ref.shape[1]
    c = a_ref.shape[2] // 2
    x = _dot(g_ref[...], a_ref[...].reshape(2 * n2, 2 * c).astype(BF))
    s = s_ref[...]
    inv = 1.0 / (s[:, :c] + s[:, c:])
    o_ref[0] = (x[:n2, :c] + x[:n2, c:]) * inv
    o_ref[1] = (x[n2:, :c] - x[n2:, c:]) * inv


def _hy_s1_kernel(m_ref, z_ref, o_ref):
    c = z_ref.shape[2]
    for t in range(z_ref.shape[1]):
        o_ref[:, t * c:(t + 1) * c] = _dot(m_ref[...], z_ref[:, t, :].astype(BF)).astype(o_ref.dtype)


def _gate_inv_kernel(m_ref, d_ref, z_ref, gate_ref, sk_ref, o_ref, *, scale):
    conv = _dot(m_ref[...], d_ref[...].astype(BF)) * scale
    c = z_ref.shape[2]
    for t in range(z_ref.shape[1]):
        o_ref[:, t, :] = gate_ref[:, t, :] * (conv[:, t * c:(t + 1) * c] + z_ref[:, t, :] * sk_ref[...])


def _hyena(uc, filt, filt_sum, skip):
    _, B, L, C = uc.shape
    assert B % 2 == 0
    P = B // 2
    N = 2 * L
    N2 = FFT_N2
    N1, f1r, f1i, gr, gi = _dft_tables(N)
    h = N1 // 2
    cols = N2 * C
    m1 = jnp.concatenate([jnp.concatenate([f1r[:, :h], -f1i[:, :h]], 1),
                          jnp.concatenate([f1i[:, :h], f1r[:, :h]], 1)], 0).astype(BF)
    m1_real = jnp.concatenate([f1r[:, :h], f1i[:, :h]], 0).astype(BF)
    m1_inv = jnp.concatenate([jnp.concatenate([f1r[:h], f1i[:h]], 1),
                              jnp.concatenate([-f1i[:h], f1r[:h]], 1)], 0).astype(BF)
    g_fwd = _blk(gr, gi).astype(BF)
    g_inv = _blk(jnp.swapaxes(gr, 1, 2), -jnp.swapaxes(gi, 1, 2)).astype(BF)

    tc = min(cols, 4096)
    CF = filt.shape[1]
    fcols = N2 * CF
    tcf = min(fcols, 8192)
    fa = _mm(m1_real, filt.reshape(h, fcols), jax.ShapeDtypeStruct((2 * N1, fcols), BF), (fcols // tcf,),
             pl.BlockSpec((2 * N1, h), lambda j: (0, 0)), pl.BlockSpec((h, tcf), lambda j: (0, j)),
             pl.BlockSpec((2 * N1, tcf), lambda j: (0, j)), None, name="filt_s1")
    kf = pl.pallas_call(
        _filt_mid_kernel, out_shape=jax.ShapeDtypeStruct((N1, 2, N2, CF // 2), F32), grid=(N1,),
        in_specs=[pl.BlockSpec((None, 2 * N2, 2 * N2), lambda k: (k, 0, 0)),
                  pl.BlockSpec((2, None, N2, CF), lambda k: (0, k, 0, 0)),
                  pl.BlockSpec((1, CF), lambda k: (0, 0))],
        out_specs=pl.BlockSpec((None, 2, N2, CF // 2), lambda k: (k, 0, 0, 0)),
        compiler_params=_cp(("parallel",)), name="filt_mid",
    )(g_fwd, fa.reshape(2, N1, N2, CF), filt_sum)

    nb = tc // C
    uc5 = uc.reshape(3, P, N1, N2, C)

    def natural(plane):
        if plane is None:
            return pl.BlockSpec((None, N1, nb, C), lambda p, j: (p, 0, j, 0))
        return pl.BlockSpec((None, None, N1, nb, C), lambda p, j: (plane, p, 0, j, 0))

    z, z_plane = uc5, 0
    for n in range(B_ORDER):
        a = pl.pallas_call(
            _hy_s1_kernel, out_shape=jax.ShapeDtypeStruct((P, 2 * N1, cols), BF), grid=(P, cols // tc),
            in_specs=[pl.BlockSpec((2 * N1, N1), lambda p, j: (0, 0)), natural(z_plane)],
            out_specs=pl.BlockSpec((None, 2 * N1, tc), lambda p, j: (p, 0, j)),
            compiler_params=_cp(("parallel", "parallel")), name="hy_s1")(m1, z)
        d = pl.pallas_call(
            _mid_kernel, out_shape=jax.ShapeDtypeStruct((P, 2, N1, N2, C), BF), grid=(N1,),
            in_specs=[pl.BlockSpec((None, 2 * N2, 2 * N2), lambda k: (k, 0, 0)),
                      pl.BlockSpec((None, 2 * N2, 2 * N2), lambda k: (k, 0, 0)),
                      pl.BlockSpec((None, 2, N2, C), lambda k: (k, 0, 0, n)),
                      pl.BlockSpec((P, 2, None, N2, C), lambda k: (0, 0, k, 0, 0))],
            out_specs=pl.BlockSpec((P, 2, None, N2, C), lambda k: (0, 0, k, 0, 0)),
            compiler_params=_cp(("parallel",)), name="hy_mid",
        )(g_fwd, g_inv, kf, a.reshape(P, 2, N1, N2, C))
        z = pl.pallas_call(
            functools.partial(_gate_inv_kernel, scale=1.0 / N),
            out_shape=jax.ShapeDtypeStruct((P, N1, N2, C), F32), grid=(P, cols // tc),
            in_specs=[pl.BlockSpec((N1, 2 * N1), lambda p, j: (0, 0)),
                      pl.BlockSpec((None, 2 * N1, tc), lambda p, j: (p, 0, j)),
                      natural(z_plane), natural(1 + n),
                      pl.BlockSpec((1, C), lambda p, j: (0, 0))],
            out_specs=natural(None),
            compiler_params=_cp(("parallel", "parallel")), name="hy_inv_gate",
        )(m1_inv, d.reshape(P, 2 * N1, cols), z, uc5, skip.astype(F32)[n:n + 1])
        z_plane = None
    return z.reshape(B, L, C)


def _fnet_tables(L, D):
    gw = D // C_GROUPS
    c = jnp.arange(gw)
    cr, ci = _cis(c[:, None] * c[None, :], gw)
    wch = jnp.concatenate([cr, ci], 1).astype(BF)
    N1, f1r, f1i, gr, gi = _dft_tables(L)
    m1 = _blk(f1r, f1i).astype(BF)
    g_re = jnp.concatenate([gr, -gi], -1).astype(BF)
    return N1, wch, m1, g_re


def _fnet_ch_kernel(x_ref, w_ref, o_ref):
    gw = x_ref.shape[1]
    res = _dot(x_ref[...].astype(BF), w_ref[...])
    o_ref[0] = res[:, :gw].astype(o_ref.dtype)
    o_ref[1] = res[:, gw:].astype(o_ref.dtype)


def _fnet_s1_kernel(m_ref, y_ref, o_ref):
    n1, tc = y_ref.shape[1], y_ref.shape[2]
    o_ref[...] = _dot(m_ref[...], y_ref[...].reshape(2 * n1, tc)).astype(o_ref.dtype)


def _fnet_s2_kernel(g_ref, a_ref, o_ref, *, scale):
    kb, n2 = g_ref.shape[0], g_ref.shape[1]
    d = a_ref.shape[3]
    for j in range(kb):
        res = _dot(g_ref[j], a_ref[:, j].reshape(2 * n2, d))
        o_ref[:, j, :] = res * scale


def _fourier_mix(x, B, L):
    T, D = x.shape
    N2 = FFT_N2
    N1, wch, m1, g_re = _fnet_tables(L, D)
    gw = D // C_GROUPS
    tm = min(1024, T)
    y = pl.pallas_call(
        _fnet_ch_kernel, out_shape=jax.ShapeDtypeStruct((2, T, D), BF), grid=(T // tm, C_GROUPS),
        in_specs=[pl.BlockSpec((tm, gw), lambda i, c: (i, c)), pl.BlockSpec((gw, 2 * gw), lambda i, c: (0, 0))],
        out_specs=pl.BlockSpec((2, tm, gw), lambda i, c: (0, i, c)),
        compiler_params=_cp(("parallel", "parallel")), name="fnet_ch")(x, wch)
    cols = N2 * D
    tc = min(cols, 8192)
    a = pl.pallas_call(
        _fnet_s1_kernel, out_shape=jax.ShapeDtypeStruct((B, 2 * N1, cols), BF), grid=(B, cols // tc),
        in_specs=[pl.BlockSpec((2 * N1, 2 * N1), lambda b, j: (0, 0)),
                  pl.BlockSpec((2, None, N1, tc), lambda b, j: (0, b, 0, j))],
        out_specs=pl.BlockSpec((None, 2 * N1, tc), lambda b, j: (b, 0, j)),
        compiler_params=_cp(("parallel", "parallel")), name="fnet_s1")(m1, y.reshape(2, B, N1, cols))
    kb = min(8, N1)
    f = pl.pallas_call(
        functools.partial(_fnet_s2_kernel, scale=1.0 / math.sqrt(L * gw)),
        out_shape=jax.ShapeDtypeStruct((B, N2, N1, D), F32), grid=(B, N1 // kb),
        in_specs=[pl.BlockSpec((kb, N2, 2 * N2), lambda b, k: (k, 0, 0)),
                  pl.BlockSpec((None, 2, kb, N2, D), lambda b, k: (b, 0, k, 0, 0))],
        out_specs=pl.BlockSpec((None, N2, kb, D), lambda b, k: (b, 0, k, 0)),
        compiler_params=_cp(("parallel", "parallel")), name="fnet_s2")(g_re, a.reshape(B, 2, N1, N2, D))
    return f.reshape(T, D)


def _router_kernel(w_ref, x_ref, o_ref):
    logits = _dot_nt(w_ref[...], x_ref[...].astype(BF))
    m = jnp.max(logits, axis=0, keepdims=True)
    p = jnp.exp(logits - m)
    o_ref[...] = p / jnp.sum(p, axis=0, keepdims=True)


def _router(x, w_router_t, tm=1024):
    T, D = x.shape
    E = w_router_t.shape[0]
    tm = min(tm, T)
    return pl.pallas_call(
        _router_kernel, out_shape=jax.ShapeDtypeStruct((E, T), F32), grid=(T // tm,),
        in_specs=[pl.BlockSpec((E, D), lambda i: (0, 0)), pl.BlockSpec((tm, D), lambda i: (i, 0))],
        out_specs=pl.BlockSpec((E, tm), lambda i: (0, i)), compiler_params=_cp(("parallel",)), name="router")(
            w_router_t, x)


def _prefix_counts(mask_f, upper, lower):
    rowcs = _dot(mask_f.astype(BF), upper)
    tot = jnp.broadcast_to(rowcs[:, LANES - 1:LANES], mask_f.shape)
    offs = _dot(lower, tot.astype(BF))
    return offs + rowcs - mask_f, offs


def _select_kernel(a_ref, rank_ref, off_ref, *, cap):
    v = a_ref[...]
    R = v.shape[0]
    bits = pltpu.bitcast(v, I32)

    def step(i, thr):
        cand = thr | (jnp.int32(1) << (30 - i))
        cnt = jnp.sum((bits >= cand).astype(F32))
        return jnp.where(cnt >= cap, cand, thr)

    thr = lax.fori_loop(0, 31, step, jnp.int32(0))
    gt = bits > thr
    eq = bits == thr
    need = cap - jnp.sum(gt.astype(F32))
    li = lax.broadcasted_iota(I32, (LANES, LANES), 0)
    lj = lax.broadcasted_iota(I32, (LANES, LANES), 1)
    upper = (li <= lj).astype(BF)
    ri = lax.broadcasted_iota(I32, (R, R), 0)
    rj = lax.broadcasted_iota(I32, (R, R), 1)
    lower = (rj < ri).astype(BF)
    eq_rank, _ = _prefix_counts(eq.astype(F32), upper, lower)
    sel = gt | (eq & (eq_rank < need))
    rank, offs = _prefix_counts(sel.astype(F32), upper, lower)
    rank_ref[...] = jnp.where(sel, rank.astype(I32), -1)
    off_ref[...] = offs.astype(I32)


def _select(aff, cap):
    E, T = aff.shape
    R = T // LANES
    spec = pl.BlockSpec((None, R, LANES), lambda e: (e, 0, 0))
    return pl.pallas_call(
        functools.partial(_select_kernel, cap=cap),
        out_shape=(jax.ShapeDtypeStruct((E, R, LANES), I32), jax.ShapeDtypeStruct((E, R, LANES), I32)),
        grid=(E,), in_specs=[spec], out_specs=(spec, spec), compiler_params=_cp(("parallel",)), name="select")(
            aff.reshape(E, R, LANES))


def _compact_kernel(off_ref, rank_ref, idx_ref, *, R):
    e = pl.program_id(0)
    idx_ref[...] = jnp.zeros_like(idx_ref)
    sub = lax.broadcasted_iota(I32, (8, LANES), 0)
    lane = lax.broadcasted_iota(I32, (8, LANES), 1)
    lhs = jnp.where(sub == 0, lane, jnp.where(sub == 1, 1, 0)).astype(BF)
    slot = lax.broadcasted_iota(I32, (2 * LANES, LANES), 0)

    def chunk(r, c):
        a = off_ref[e, r] // LANES
        local = rank_ref[pl.ds(r, 1), :] - a * LANES
        onehot = (slot == local).astype(BF)
        res = _dot_nt(lhs, onehot)
        val = (res[0:1] + res[1:2] * lax.convert_element_type(r * LANES, F32)).astype(I32)
        idx_ref[pl.ds(a, 1), :] += val[:, :LANES]
        idx_ref[pl.ds(a + 1, 1), :] += val[:, LANES:]
        return c

    lax.fori_loop(0, R, chunk, 0, unroll=8)


def _compact(rank, rowoff, cap):
    E, R, _ = rank.shape
    nrow = cap // LANES + 2
    return pl.pallas_call(
        functools.partial(_compact_kernel, R=R),
        out_shape=jax.ShapeDtypeStruct((E, nrow, LANES), I32),
        grid_spec=pltpu.PrefetchScalarGridSpec(
            num_scalar_prefetch=1, grid=(E,),
            in_specs=[pl.BlockSpec((None, R, LANES), lambda e, off: (e, 0, 0))],
            out_specs=pl.BlockSpec((None, nrow, LANES), lambda e, off: (e, 0, 0))),
        compiler_params=_cp(("arbitrary",)), name="compact")(rowoff, rank)


def _ffn_kernel(idx_ref, idxn_ref, x_hbm, wg_ref, wu_ref, wd_ref, o_ref, xbuf, xb, acc, sem, *, tm, nf, ntiles):
    f = pl.program_id(2)
    tile = pl.program_id(0) * pl.num_programs(1) + pl.program_id(1)
    slot = tile % 2
    part = tm // nf

    def row_copy(t, r, slot_):
        return pltpu.make_async_copy(x_hbm.at[pl.ds(t, 1)], xbuf.at[slot_, pl.ds(r, 1)], sem.at[slot_])

    def wait_tile(slot_):
        pltpu.make_async_copy(x_hbm.at[pl.ds(0, tm)], xbuf.at[slot_], sem.at[slot_]).wait()

    @pl.when((tile == 0) & (f == 0))
    def _():
        def issue(r, c):
            row_copy(idx_ref[0, 0, r], r, 0).start()
            return c

        lax.fori_loop(0, tm, issue, 0)

    @pl.when(f == 0)
    def _():
        wait_tile(slot)
        xb[...] = xbuf[slot].astype(BF)
        if nf > 1:
            acc[...] = jnp.zeros_like(acc)

    base = f * part
    for j in range(part):
        row_copy(idxn_ref[0, 0, base + j], base + j, 1 - slot).start()

    x = xb[...]
    g = _dot(x, wg_ref[...])
    u = _dot(x, wu_ref[...])
    hcur = (g * jax.nn.sigmoid(g)) * u
    down = _dot(hcur.astype(BF), wd_ref[...])
    if nf == 1:
        o_ref[...] = down.astype(o_ref.dtype)
    else:
        acc[...] += down

        @pl.when(f == nf - 1)
        def _():
            o_ref[...] = acc[...].astype(o_ref.dtype)

    @pl.when((tile == ntiles - 1) & (f == nf - 1))
    def _():
        wait_tile(1 - slot)


def _ffn(x, idx, wg, wu, wd, layer, cap, tm=512):
    T, D = x.shape
    E = wg.shape[1]
    FF = wg.shape[3]
    tm = min(tm, cap)
    nt = cap // tm
    fc, nf = FF, 1
    resident = pl.Buffered(1)
    idx3 = idx[:, :cap // LANES, :].reshape(E * nt, 1, tm)
    last = E * nt - 1
    return pl.pallas_call(
        functools.partial(_ffn_kernel, tm=tm, nf=nf, ntiles=E * nt),
        out_shape=jax.ShapeDtypeStruct((E, cap, D), BF), grid=(E, nt, nf),
        in_specs=[pl.BlockSpec((1, 1, tm), lambda e, i, f: (e * nt + i, 0, 0), memory_space=pltpu.SMEM),
                  pl.BlockSpec((1, 1, tm), lambda e, i, f: (jnp.minimum(e * nt + i + 1, last), 0, 0),
                               memory_space=pltpu.SMEM),
                  pl.BlockSpec(memory_space=pl.ANY),
                  pl.BlockSpec((None, None, D, fc), lambda e, i, f: (layer, e, 0, f), pipeline_mode=resident),
                  pl.BlockSpec((None, None, D, fc), lambda e, i, f: (layer, e, 0, f), pipeline_mode=resident),
                  pl.BlockSpec((None, None, fc, D), lambda e, i, f: (layer, e, f, 0), pipeline_mode=resident)],
        out_specs=pl.BlockSpec((None, tm, D), lambda e, i, f: (e, i, 0)),
        scratch_shapes=[pltpu.VMEM((2, tm, D), F32), pltpu.VMEM((tm, D), BF),
                        pltpu.VMEM((tm, D) if nf > 1 else (8, LANES), F32), pltpu.SemaphoreType.DMA((2,))],
        compiler_params=_cp(("arbitrary", "arbitrary", "arbitrary")), name="expert_ffn")(idx3, idx3, x, wg, wu, wd)


COMBINE_WIN = 64


def _combine_kernel(off_ref, rank_ref, aff_ref, x_ref, g_ref, b_ref, out_hbm, o_ref, wins, winx, ysc, sem, semx,
                    *, tt, E, cap, rpt, ntile, alpha):
    W = COMBINE_WIN
    ti = pl.program_id(0)
    slot = ti % 2

    def start0(e, ti_):
        r0 = off_ref[e, ti_ * rpt]
        return pl.multiple_of(jnp.minimum((r0 // 16) * 16, cap - W), 16)

    def win_copy(e, ti_, slot_):
        return pltpu.make_async_copy(out_hbm.at[e, pl.ds(start0(e, ti_), W)], wins.at[slot_, pl.ds(e * W, W)],
                                     sem.at[slot_])

    @pl.when(ti == 0)
    def _():
        for e in range(E):
            win_copy(e, 0, 0).start()

    @pl.when(ti + 1 < ntile)
    def _():
        for e in range(E):
            win_copy(e, ti + 1, 1 - slot).start()

    sub = lax.broadcasted_iota(I32, (W, tt), 0)

    def gated_onehot(r_row, g_row, first_row, min_rank):
        g_hi = g_row.astype(BF).astype(F32)
        match = (sub == r_row - first_row) & (r_row >= min_rank)
        return jnp.concatenate([jnp.where(match, g_hi, 0.0), jnp.where(match, g_row - g_hi, 0.0)], axis=1).astype(BF)

    def scatter_rows(p_t, rows):
        y2 = lax.dot_general(p_t, rows, (((0,), (0,)), ((), ())), preferred_element_type=F32)
        return y2[:tt] + y2[tt:]

    p_t = jnp.concatenate([gated_onehot(rank_ref[e:e + 1, :], aff_ref[e:e + 1, :], start0(e, ti), 0)
                           for e in range(E)], axis=0)
    for e in range(E):
        win_copy(e, ti, slot).wait()
    ysc[...] = alpha * x_ref[...] + scatter_rows(p_t, wins[slot])

    def extra(e, c):
        s0 = start0(e, ti)
        r_end = off_ref[e, (ti + 1) * rpt]

        @pl.when(r_end > s0 + W)
        def _():
            r_row = rank_ref[pl.ds(e, 1), :]
            g_row = aff_ref[pl.ds(e, 1), :]
            for w in range(1, tt // W + 1):
                lo_w = s0 + w * W

                @pl.when(r_end > lo_w)
                def _():
                    sw = pl.multiple_of(jnp.minimum(lo_w, cap - W), 16)
                    cp = pltpu.make_async_copy(out_hbm.at[e, pl.ds(sw, W)], winx, semx)
                    cp.start()
                    cp.wait()
                    ysc[...] += scatter_rows(gated_onehot(r_row, g_row, sw, lo_w), winx[...])

        return c

    lax.fori_loop(0, E, extra, 0)
    o_ref[...] = _ln_rows(ysc[...], g_ref[...], b_ref[...])


def _combine_ln(out, rank, aff, rowoff, x, g, b, cap, alpha):
    T, D = x.shape
    E = out.shape[0]
    W = COMBINE_WIN
    assert cap >= W
    tt = min(256, cap // 2)
    rpt = tt // LANES
    ntile = T // tt
    kern = functools.partial(_combine_kernel, tt=tt, E=E, cap=cap, rpt=rpt, ntile=ntile, alpha=alpha)
    return pl.pallas_call(
        kern, out_shape=jax.ShapeDtypeStruct((T, D), F32),
        grid_spec=pltpu.PrefetchScalarGridSpec(
            num_scalar_prefetch=1, grid=(ntile,),
            in_specs=[pl.BlockSpec((E, tt), lambda i, off: (0, i)),
                      pl.BlockSpec((E, tt), lambda i, off: (0, i)),
                      pl.BlockSpec((tt, D), lambda i, off: (i, 0)),
                      pl.BlockSpec((1, D), lambda i, off: (0, 0)),
                      pl.BlockSpec((1, D), lambda i, off: (0, 0)),
                      pl.BlockSpec(memory_space=pl.ANY)],
            out_specs=pl.BlockSpec((tt, D), lambda i, off: (i, 0)),
            scratch_shapes=[pltpu.VMEM((2, E * W, D), BF), pltpu.VMEM((W, D), BF), pltpu.VMEM((tt, D), F32),
                            pltpu.SemaphoreType.DMA((2,)), pltpu.SemaphoreType.DMA]),
        compiler_params=_cp(("arbitrary",)), name="combine_ln",
    )(rowoff, rank, aff, x, g.reshape(1, D), b.reshape(1, D), out)


def _expert_choice_ln(x, w_router_t, wg, wu, wd, layer, g, b, alpha):
    T, D = x.shape
    E = N_EXPERTS
    cap = max(1, EC_FACTOR * T // E)
    aff = _router(x, w_router_t)
    rank, rowoff = _select(aff, cap)
    rowoff = jnp.concatenate([rowoff[:, :, 0], jnp.full((E, 1), cap, I32)], axis=1)
    idx = _compact(rank, rowoff, cap)
    out = _ffn(x, idx, wg, wu, wd, layer, cap)
    return _combine_ln(out, rank.reshape(E, T), aff, rowoff, x, g, b, cap, alpha)


def _trunk(x, p, depth):
    B, L, D = x.shape
    T = B * L
    alpha = (2 * depth) ** 0.25
    x = x.reshape(T, D)
    filt_cache = {}
    for layer in range(depth):
        j = layer // 2
        if layer % 2 == 0:
            hproj = _xw(x, p["ab_w_in"][j], name="ab_in").reshape(B, L, -1)
            lambda_init = 0.8 - 0.6 * math.exp(-0.3 * layer)
            lf = p["diff_lambda"][j].astype(F32)
            lam_full = jnp.exp(jnp.sum(lf[0] * lf[1])) - jnp.exp(jnp.sum(lf[2] * lf[3])) + lambda_init
            t_attn = min(ATTN_TILE, L)
            if t_attn not in p["bias_tiles"]:
                p["bias_tiles"][t_attn] = _bias_tiles(p["rel_bias"], t_attn)
            a_out = _diff_attention(hproj, lam_full, p["diff_subln_g"][j], p["bias_tiles"][t_attn], lambda_init)
            uc = _short_conv(hproj, p["hy_conv_w"][j], p["hy_conv_b"][j])
            filt, filt_sum = _filter_mlp(L, p["hy_f_w1"][j], p["hy_f_b1"][j], p["hy_f_w2"][j], p["hy_f_b2"][j],
                                         p["hy_f_w3"][j], p["hy_f_freq"][j], p["hy_decay"][j])
            b_out = _hyena(uc, filt, filt_sum, p["hy_skip"][j])
            w_out = p["ab_w_out"][j]
            pairs = [(a_out.reshape(T, D_A), w_out[:D_A]), (b_out.reshape(T, D_B), w_out[D_A:])]
        else:
            pairs = [(_fourier_mix(x, B, L), p["c_w_out"][j])]
        x = _mm_res_ln(pairs, x, p["ln_g"][layer, 0], p["ln_b"][layer, 0], alpha)
        x = _expert_choice_ln(x, p["ec_router_t"][layer], p["ec_w_gate"], p["ec_w_up"], p["ec_w_down"], layer,
                              p["ln_g"][layer, 1], p["ln_b"][layer, 1], alpha)
    return x.reshape(B, L, D)


def kernel(x_prompt, x_sample, rel_bias, ab_w_in, ab_w_out, diff_lambda, diff_subln_g, hy_conv_w, hy_conv_b,
           hy_f_w1, hy_f_b1, hy_f_w2, hy_f_b2, hy_f_w3, hy_f_freq, hy_decay, hy_skip,
           c_w_out, ec_router, ec_w_gate, ec_w_up, ec_w_down, ln_g, ln_b):
    depth = ec_router.shape[0]
    p = dict(
        rel_bias=rel_bias, ab_w_in=ab_w_in.astype(BF), ab_w_out=ab_w_out.astype(BF), diff_lambda=diff_lambda,
        diff_subln_g=diff_subln_g, hy_conv_w=hy_conv_w, hy_conv_b=hy_conv_b, hy_f_w1=hy_f_w1, hy_f_b1=hy_f_b1,
        hy_f_w2=hy_f_w2, hy_f_b2=hy_f_b2, hy_f_w3=hy_f_w3, hy_f_freq=hy_f_freq, hy_decay=hy_decay, hy_skip=hy_skip,
        c_w_out=c_w_out.astype(BF), ec_router_t=jnp.swapaxes(ec_router, 1, 2).astype(BF),
        ec_w_gate=ec_w_gate.astype(BF), ec_w_up=ec_w_up.astype(BF), ec_w_down=ec_w_down.astype(BF),
        ln_g=ln_g.astype(F32), ln_b=ln_b.astype(F32), bias_tiles={})
    return _trunk(x_prompt, p, depth), _trunk(x_sample, p, depth)
```

```python
import functools
import math

import jax
import jax.numpy as jnp
from jax import lax
from jax.experimental import pallas as pl
from jax.experimental.pallas import tpu as pltpu

BF = jnp.bfloat16
F32 = jnp.float32
I32 = jnp.int32

A_HEADS = 4
A_DV = 128
A_DK = 64
D_A = 512
D_B = 512
B_ORDER = 2
REL_BUCKETS = 32
REL_MAX_DIST = 128
FILTER_BANDS = 16
N_EXPERTS = 16
EC_FACTOR = 2
LN_EPS = 1e-5
C_GROUPS = 4
LANES = 128
BF16_SUBLANES = 16
FFT_N2 = 128
VMEM_LIMIT = 52 * 1024 * 1024

TOKEN_TILE = 512
WIDE_TOKEN_TILE = 1024
FFN_ROW_TILE = 512
COMBINE_TOKEN_TILE = 256
FILTER_ROW_TILE = 512
HY_COL_TILE = 4096
FNET_COL_TILE = 8192
FNET_FREQ_PER_STEP = 8


def _cp(sem, vmem=VMEM_LIMIT):
    return pltpu.CompilerParams(dimension_semantics=sem, vmem_limit_bytes=vmem)


def _dot(a, b):
    return jnp.dot(a, b, preferred_element_type=F32)


def _dot_nt(a, b):
    return lax.dot_general(a, b, (((1,), (1,)), ((), ())), preferred_element_type=F32)


def _mm_kernel(a_ref, b_ref, o_ref, *scratch, nk, kaxis, scale):
    prod = _dot(a_ref[...].astype(BF), b_ref[...].astype(BF))
    if nk == 1:
        o_ref[...] = (prod * scale).astype(o_ref.dtype)
        return
    acc = scratch[0]
    k = pl.program_id(kaxis)

    @pl.when(k == 0)
    def _():
        acc[...] = jnp.zeros_like(acc)

    acc[...] += prod

    @pl.when(k == nk - 1)
    def _():
        o_ref[...] = (acc[...] * scale).astype(o_ref.dtype)


def _mm(a, b, out_shape, grid, a_spec, b_spec, o_spec, mn, *, nk=1, kaxis=None, scale=1.0, name):
    sem = tuple("arbitrary" if i == kaxis else "parallel" for i in range(len(grid)))
    return pl.pallas_call(
        functools.partial(_mm_kernel, nk=nk, kaxis=kaxis, scale=scale),
        out_shape=out_shape, grid=grid, in_specs=[a_spec, b_spec], out_specs=o_spec,
        scratch_shapes=[pltpu.VMEM(mn, F32)] if nk > 1 else [],
        compiler_params=_cp(sem), name=name)(a, b)


def _xw(x, w, tm=TOKEN_TILE, name="xw"):
    M, K = x.shape
    N = w.shape[1]
    tm = min(tm, M)
    return _mm(x, w, jax.ShapeDtypeStruct((M, N), F32), (M // tm,),
               pl.BlockSpec((tm, K), lambda i: (i, 0)), pl.BlockSpec((K, N), lambda i: (0, 0)),
               pl.BlockSpec((tm, N), lambda i: (i, 0)), (tm, N), name=name)


def _ln_rows(y, g, b):
    mu = jnp.mean(y, -1, keepdims=True)
    d = y - mu
    var = jnp.mean(d * d, -1, keepdims=True)
    return d * lax.rsqrt(var + LN_EPS) * g + b


def _mm_res_ln_kernel(*refs, npairs, alpha):
    x_ref, g_ref, b_ref = refs[2 * npairs:2 * npairs + 3]
    o_ref = refs[2 * npairs + 3]
    y = alpha * x_ref[...]
    for i in range(npairs):
        y = y + _dot(refs[2 * i][...].astype(BF), refs[2 * i + 1][...])
    o_ref[...] = _ln_rows(y, g_ref[...], b_ref[...])


def _mm_res_ln(pairs, x, g, b, alpha, tm=TOKEN_TILE, name="mm_res_ln"):
    M, D = x.shape
    tm = min(tm, M)
    specs, args = [], []
    for a, w in pairs:
        specs += [pl.BlockSpec((tm, a.shape[1]), lambda i: (i, 0)), pl.BlockSpec(w.shape, lambda i: (0, 0))]
        args += [a, w]
    specs += [pl.BlockSpec((tm, D), lambda i: (i, 0)), pl.BlockSpec((1, D), lambda i: (0, 0)),
              pl.BlockSpec((1, D), lambda i: (0, 0))]
    args += [x, g.reshape(1, D), b.reshape(1, D)]
    return pl.pallas_call(
        functools.partial(_mm_res_ln_kernel, npairs=len(pairs), alpha=alpha),
        out_shape=jax.ShapeDtypeStruct((M, D), F32), grid=(M // tm,), in_specs=specs,
        out_specs=pl.BlockSpec((tm, D), lambda i: (i, 0)), compiler_params=_cp(("parallel",)), name=name)(*args)


def _rel_bucket(rel):
    nb = REL_BUCKETS // 2
    max_exact = nb // 2
    n = jnp.abs(rel)
    large = max_exact + (jnp.log(jnp.maximum(n, 1).astype(F32) / max_exact)
                         / math.log(REL_MAX_DIST / max_exact) * (nb - max_exact)).astype(I32)
    large = jnp.minimum(large, nb - 1)
    return jnp.where(rel > 0, nb, 0) + jnp.where(n < max_exact, n, large)


LOG2E = 1.4426950408889634


def _bias_tiles(rel_table, T):
    assert T >= REL_MAX_DIST
    i = jnp.arange(T)
    rel = (jnp.arange(-2, 3)[:, None, None] * T + i[None, :, None]) - i[None, None, :]
    bucket = _rel_bucket(rel)[None]
    tab = LOG2E * rel_table.astype(F32).T[:, :, None, None, None]
    out = jnp.zeros((rel_table.shape[1],) + rel.shape, F32)
    for b in range(REL_BUCKETS):
        out = jnp.where(bucket == b, tab[:, b], out)
    return out


ATTN_MAX_DRIFT = 40.0
ATTN_CHUNK_GROUP = 3
ATTN_TILE = 512


def _attn_kernel(lam_ref, q_ref, k_ref, v_ref, bias_ref, g_ref, o_ref, kb, vt, m_s, a_s, *, T, nk, out_scale):
    qi = pl.program_id(2)
    dv = A_DV

    @pl.when(qi == 0)
    def _():
        def prep(c, carry):
            st = pl.multiple_of(c * T, T)
            kb[c] = k_ref[pl.ds(st, T), :].astype(BF)
            vt[c, :dv] = v_ref[pl.ds(st, T), :].T.astype(BF)
            vt[c, dv:] = jnp.ones((vt.shape[1] - dv, T), BF)
            return carry

        lax.fori_loop(0, nk, prep, 0)

    qt = (q_ref[...] * (A_DK ** -0.5 * LOG2E)).T
    row = lax.broadcasted_iota(I32, qt.shape, 0)
    qs = (jnp.where(row < A_DK, qt, 0.0).astype(BF), jnp.where(row >= A_DK, qt, 0.0).astype(BF))

    def qk(kj, mi):
        return _dot(kb[kj], qs[mi])

    def scores(kj, mi):
        return qk(kj, mi) + bias_ref[jnp.clip(kj - qi, -2, 2) + 2]

    def rescaling_chunk(kj, first):
        for mi in range(2):
            s = scores(kj, mi)
            m_new = jnp.max(s, axis=0, keepdims=True)
            if not first:
                m_new = jnp.maximum(m_s[mi], m_new)
            pv = _dot(vt[kj], jnp.exp2(s - m_new).astype(BF))
            if first:
                a_s[mi] = pv
            else:
                a_s[mi] = jnp.exp2(m_s[mi] - m_new) * a_s[mi] + pv
            m_s[mi] = m_new

    rescaling_chunk(0, True)

    def lazy_chunks(kjs, near):
        pv, drift = [None, None], None
        if not near:
            far_bias = jnp.where(kjs[-1] < qi, bias_ref[0, 0:1, :], bias_ref[4, 0:1, :])
        for kj in kjs:
            for mi in range(2):
                if near:
                    s, m_ref = scores(kj, mi), m_s[mi]
                else:
                    s, m_ref = qk(kj, mi), m_s[mi] - far_bias
                acc = _dot(vt[kj], jnp.exp2(s - m_ref).astype(BF))
                pv[mi] = acc if pv[mi] is None else pv[mi] + acc
                d = jnp.max(s, axis=0, keepdims=True) - m_ref
                drift = d if drift is None else jnp.maximum(drift, d)
        over = jnp.max(drift) > ATTN_MAX_DRIFT

        @pl.when(jnp.logical_not(over))
        def _():
            for mi in range(2):
                a_s[mi] += pv[mi]

        @pl.when(over)
        def _():
            for kj in kjs:
                rescaling_chunk(kj, False)

    def near_or_far(kjs):
        is_near = (kjs[0] - 1 <= qi) & (qi <= kjs[-1] + 1)

        @pl.when(is_near)
        def _():
            lazy_chunks(kjs, True)

        @pl.when(jnp.logical_not(is_near))
        def _():
            lazy_chunks(kjs, False)

    group = ATTN_CHUNK_GROUP

    def grouped(i, c):
        near_or_far([1 + group * i + u for u in range(group)])
        return c

    ngroups = (nk - 1) // group
    lax.fori_loop(0, ngroups, grouped, 0)
    if (nk - 1) % group:
        near_or_far(list(range(1 + group * ngroups, nk)))
    a0, a1 = a_s[0], a_s[1]
    o = a0[:dv] / a0[dv:dv + 1] - lam_ref[0] * (a1[:dv] / a1[dv:dv + 1])
    o = o * lax.rsqrt(jnp.mean(o * o, 0, keepdims=True) + LN_EPS) * g_ref[...] * out_scale
    o_ref[...] = o.T


def _diff_attention(hproj, lam_full, sub_g, bias, lambda_init):
    B, L, _ = hproj.shape
    T = bias.shape[-1]
    H = A_HEADS
    return pl.pallas_call(
        functools.partial(_attn_kernel, T=T, nk=L // T, out_scale=1.0 - lambda_init),
        out_shape=jax.ShapeDtypeStruct((B, L, D_A), F32), grid=(B, H, L // T),
        in_specs=[pl.BlockSpec(memory_space=pltpu.SMEM),
                  pl.BlockSpec((None, T, LANES), lambda b, h, i: (b, i, h)),
                  pl.BlockSpec((None, L, LANES), lambda b, h, i: (b, 0, H + h)),
                  pl.BlockSpec((None, L, LANES), lambda b, h, i: (b, 0, 2 * H + h)),
                  pl.BlockSpec((None, 5, T, T), lambda b, h, i: (h, 0, 0, 0)),
                  pl.BlockSpec((A_DV, 1), lambda b, h, i: (0, 0))],
        out_specs=pl.BlockSpec((None, T, LANES), lambda b, h, i: (b, i, h)),
        scratch_shapes=[pltpu.VMEM((L // T, T, LANES), BF), pltpu.VMEM((L // T, A_DV + BF16_SUBLANES, T), BF),
                        pltpu.VMEM((2, 1, T), F32), pltpu.VMEM((2, A_DV + BF16_SUBLANES, T), F32)],
        compiler_params=_cp(("parallel", "parallel", "arbitrary")), name="diff_attn",
    )(lam_full.reshape(1), hproj, hproj, hproj, bias, sub_g.astype(F32).reshape(A_DV, 1))


def _short_conv_kernel(u_ref, w_ref, b_ref, o_ref):
    u = u_ref[...]
    L = u.shape[0]
    row = lax.broadcasted_iota(I32, u.shape, 0)
    up = jnp.where(row == 0, 0.0, pltpu.roll(u, 1, 0))
    dn = jnp.where(row == L - 1, 0.0, pltpu.roll(u, L - 1, 0))
    w = w_ref[...]
    o_ref[...] = up * w[0:1] + u * w[1:2] + dn * w[2:3] + b_ref[...]


def _short_conv(hproj, conv_w, conv_b):
    B, L, _ = hproj.shape
    nblk = 3 * D_B // LANES
    off = 3 * D_A // LANES
    per = D_B // LANES
    return pl.pallas_call(
        _short_conv_kernel, out_shape=jax.ShapeDtypeStruct((3, B, L, D_B), F32), grid=(B, nblk),
        in_specs=[pl.BlockSpec((None, L, LANES), lambda b, j: (b, 0, off + j)),
                  pl.BlockSpec((3, LANES), lambda b, j: (0, j)),
                  pl.BlockSpec((1, LANES), lambda b, j: (0, j))],
        out_specs=pl.BlockSpec((None, None, L, LANES), lambda b, j: (j // per, b, 0, j % per)),
        compiler_params=_cp(("parallel", "parallel")), name="short_conv",
    )(hproj, conv_w.astype(F32), conv_b.astype(F32).reshape(1, -1))


def _filter_mlp_kernel(z_ref, w1_ref, b1_ref, w2_ref, b2_ref, w3_ref, fq_ref, dec_ref, h_ref, s_ref):
    i = pl.program_id(0)
    z = z_ref[...]
    fq = fq_ref[...]
    h = jnp.sin(fq[0:1] * (_dot(z.astype(BF), w1_ref[...]) + b1_ref[...]))
    h = jnp.sin(fq[1:2] * (_dot(h.astype(BF), w2_ref[...]) + b2_ref[...]))
    t = z[:, 0:1]
    h = _dot(h.astype(BF), w3_ref[...]) * jnp.exp(-t * jnp.abs(dec_ref[...]))

    @pl.when(i == 0)
    def _():
        s_ref[...] = jnp.zeros_like(s_ref)

    s_ref[...] += jnp.sum(jnp.abs(h), axis=0, keepdims=True)
    row = lax.broadcasted_iota(I32, h.shape, 0)
    col = lax.broadcasted_iota(I32, h.shape, 1)
    half = h.shape[1] // 2
    h_ref[...] = jnp.where((row + i * h.shape[0] == 0) & (col >= half), 0.0, h)


def _filter_mlp(L, w1, b1, w2, b2, w3, freq, decay, tl=FILTER_ROW_TILE):
    t = jnp.linspace(0.0, 1.0, L, dtype=F32)[:, None]
    wpos = 2.0 * math.pi * jnp.arange(L, dtype=F32)[:, None] / L
    fr = jnp.linspace(1e-4, FILTER_BANDS - 1, FILTER_BANDS, dtype=F32)[None, :]
    z = jnp.concatenate([t, jnp.cos(fr * wpos), -jnp.sin(fr * wpos)], -1)
    emb, width = w1.shape
    ch = w3.shape[1]
    z = jnp.pad(z, ((0, 0), (0, LANES - emb)))
    padw = LANES - width
    w1p = jnp.pad(w1, ((0, LANES - emb), (0, padw))).astype(BF)
    w2p = jnp.pad(w2, ((0, padw), (0, padw))).astype(BF)
    w3p = jnp.pad(w3, ((0, padw), (0, 0))).astype(BF)
    b1p = jnp.pad(b1.astype(F32), (0, padw)).reshape(1, LANES)
    b2p = jnp.pad(b2.astype(F32), (0, padw)).reshape(1, LANES)
    fqp = jnp.pad(freq.astype(F32), ((0, 0), (0, padw)))
    tl = min(tl, L)
    full = lambda shp: pl.BlockSpec(shp, lambda i: (0, 0))
    return pl.pallas_call(
        _filter_mlp_kernel,
        out_shape=(jax.ShapeDtypeStruct((L, ch), F32), jax.ShapeDtypeStruct((1, ch), F32)), grid=(L // tl,),
        in_specs=[pl.BlockSpec((tl, LANES), lambda i: (i, 0)), full((LANES, LANES)), full((1, LANES)),
                  full((LANES, LANES)), full((1, LANES)), full((LANES, ch)), full((2, LANES)), full((1, ch))],
        out_specs=(pl.BlockSpec((tl, ch), lambda i: (i, 0)), full((1, ch))),
        compiler_params=_cp(("arbitrary",)), name="filter_mlp",
    )(z, w1p, b1p, w2p, b2p, w3p, fqp, decay.astype(F32).reshape(1, ch))


def _cis(idx, n):
    ang = (2.0 * math.pi / n) * (idx % n).astype(F32)
    return jnp.cos(ang), -jnp.sin(ang)


def _blk(re, im):
    return jnp.concatenate([jnp.concatenate([re, -im], -1), jnp.concatenate([im, re], -1)], -2)


def _dft_tables(N):
    N2 = FFT_N2
    N1 = N // N2
    a = jnp.arange(N1)
    b = jnp.arange(N2)
    f1r, f1i = _cis(a[:, None] * a[None, :], N1)
    tr, ti = _cis(a[:, None] * b[None, :], N)
    f2r, f2i = _cis(b[:, None] * b[None, :], N2)
    gr = tr[:, None, :] * f2r[None] - ti[:, None, :] * f2i[None]
    gi = tr[:, None, :] * f2i[None] + ti[:, None, :] * f2r[None]
    return N1, f1r, f1i, gr, gi


def _mid_kernel(g_ref, gi_ref, kf_ref, a_ref, o_ref):
    n2, c = a_ref.shape[2], a_ref.shape[3]
    kr, ki = kf_ref[0], kf_ref[1]
    for p in range(a_ref.shape[0]):
        x = _dot(g_ref[...], a_ref[p].reshape(2 * n2, c).astype(BF))
        xr, xi = x[:n2], x[n2:]
        y = jnp.concatenate([xr * kr - xi * ki, xr * ki + xi * kr], axis=0).astype(BF)
        o_ref[p] = _dot(gi_ref[...], y).reshape(2, n2, c).astype(o_ref.dtype)


def _filt_mid_kernel(g_ref, a_ref, s_ref, o_ref):
    n2 = a_ref.shape[1]
    c = a_ref.shape[2] // 2
    x = _dot(g_ref[...], a_ref[...].reshape(2 * n2, 2 * c).astype(BF))
    s = s_ref[...]
    inv = 1.0 / (s[:, :c] + s[:, c:])
    o_ref[0] = (x[:n2, :c] + x[:n2, c:]) * inv
    o_ref[1] = (x[n2:, :c] - x[n2:, c:]) * inv


def _hy_s1_kernel(m_ref, z_ref, o_ref):
    c = z_ref.shape[2]
    for t in range(z_ref.shape[1]):
        o_ref[:, t * c:(t + 1) * c] = _dot(m_ref[...], z_ref[:, t, :].astype(BF)).astype(o_ref.dtype)


def _gate_inv_kernel(m_ref, d_ref, z_ref, gate_ref, sk_ref, o_ref, *, scale):
    conv = _dot(m_ref[...], d_ref[...].astype(BF)) * scale
    c = z_ref.shape[2]
    for t in range(z_ref.shape[1]):
        o_ref[:, t, :] = gate_ref[:, t, :] * (conv[:, t * c:(t + 1) * c] + z_ref[:, t, :] * sk_ref[...])


def _hyena(uc, filt, filt_sum, skip):
    _, B, L, C = uc.shape
    assert B % 2 == 0
    P = B // 2
    N = 2 * L
    N2 = FFT_N2
    N1, f1r, f1i, gr, gi = _dft_tables(N)
    h = N1 // 2
    cols = N2 * C
    m1 = jnp.concatenate([jnp.concatenate([f1r[:, :h], -f1i[:, :h]], 1),
                          jnp.concatenate([f1i[:, :h], f1r[:, :h]], 1)], 0).astype(BF)
    m1_real = jnp.concatenate([f1r[:, :h], f1i[:, :h]], 0).astype(BF)
    m1_inv = jnp.concatenate([jnp.concatenate([f1r[:h], f1i[:h]], 1),
                              jnp.concatenate([-f1i[:h], f1r[:h]], 1)], 0).astype(BF)
    g_fwd = _blk(gr, gi).astype(BF)
    g_inv = _blk(jnp.swapaxes(gr, 1, 2), -jnp.swapaxes(gi, 1, 2)).astype(BF)

    tc = min(cols, HY_COL_TILE)
    CF = filt.shape[1]
    fcols = N2 * CF
    tcf = min(fcols, FNET_COL_TILE)
    fa = _mm(m1_real, filt.reshape(h, fcols), jax.ShapeDtypeStruct((2 * N1, fcols), BF), (fcols // tcf,),
             pl.BlockSpec((2 * N1, h), lambda j: (0, 0)), pl.BlockSpec((h, tcf), lambda j: (0, j)),
             pl.BlockSpec((2 * N1, tcf), lambda j: (0, j)), None, name="filt_s1")
    kf = pl.pallas_call(
        _filt_mid_kernel, out_shape=jax.ShapeDtypeStruct((N1, 2, N2, CF // 2), F32), grid=(N1,),
        in_specs=[pl.BlockSpec((None, 2 * N2, 2 * N2), lambda k: (k, 0, 0)),
                  pl.BlockSpec((2, None, N2, CF), lambda k: (0, k, 0, 0)),
                  pl.BlockSpec((1, CF), lambda k: (0, 0))],
        out_specs=pl.BlockSpec((None, 2, N2, CF // 2), lambda k: (k, 0, 0, 0)),
        compiler_params=_cp(("parallel",)), name="filt_mid",
    )(g_fwd, fa.reshape(2, N1, N2, CF), filt_sum)

    nb = tc // C
    uc5 = uc.reshape(3, P, N1, N2, C)

    def natural(plane):
        if plane is None:
            return pl.BlockSpec((None, N1, nb, C), lambda p, j: (p, 0, j, 0))
        return pl.BlockSpec((None, None, N1, nb, C), lambda p, j: (plane, p, 0, j, 0))

    z, z_plane = uc5, 0
    for n in range(B_ORDER):
        a = pl.pallas_call(
            _hy_s1_kernel, out_shape=jax.ShapeDtypeStruct((P, 2 * N1, cols), BF), grid=(P, cols // tc),
            in_specs=[pl.BlockSpec((2 * N1, N1), lambda p, j: (0, 0)), natural(z_plane)],
            out_specs=pl.BlockSpec((None, 2 * N1, tc), lambda p, j: (p, 0, j)),
            compiler_params=_cp(("parallel", "parallel")), name="hy_s1")(m1, z)
        d = pl.pallas_call(
            _mid_kernel, out_shape=jax.ShapeDtypeStruct((P, 2, N1, N2, C), BF), grid=(N1,),
            in_specs=[pl.BlockSpec((None, 2 * N2, 2 * N2), lambda k: (k, 0, 0)),
                      pl.BlockSpec((None, 2 * N2, 2 * N2), lambda k: (k, 0, 0)),
                      pl.BlockSpec((None, 2, N2, C), lambda k: (k, 0, 0, n)),
                      pl.BlockSpec((P, 2, None, N2, C), lambda k: (0, 0, k, 0, 0))],
            out_specs=pl.BlockSpec((P, 2, None, N2, C), lambda k: (0, 0, k, 0, 0)),
            compiler_params=_cp(("parallel",)), name="hy_mid",
        )(g_fwd, g_inv, kf, a.reshape(P, 2, N1, N2, C))
        z = pl.pallas_call(
            functools.partial(_gate_inv_kernel, scale=1.0 / N),
            out_shape=jax.ShapeDtypeStruct((P, N1, N2, C), F32), grid=(P, cols // tc),
            in_specs=[pl.BlockSpec((N1, 2 * N1), lambda p, j: (0, 0)),
                      pl.BlockSpec((None, 2 * N1, tc), lambda p, j: (p, 0, j)),
                      natural(z_plane), natural(1 + n),
                      pl.BlockSpec((1, C), lambda p, j: (0, 0))],
            out_specs=natural(None),
            compiler_params=_cp(("parallel", "parallel")), name="hy_inv_gate",
        )(m1_inv, d.reshape(P, 2 * N1, cols), z, uc5, skip.astype(F32)[n:n + 1])
        z_plane = None
    return z.reshape(B, L, C)


def _fnet_tables(L, D):
    gw = D // C_GROUPS
    c = jnp.arange(gw)
    cr, ci = _cis(c[:, None] * c[None, :], gw)
    wch = jnp.concatenate([cr, ci], 1).astype(BF)
    N1, f1r, f1i, gr, gi = _dft_tables(L)
    m1 = _blk(f1r, f1i).astype(BF)
    g_re = jnp.concatenate([gr, -gi], -1).astype(BF)
    return N1, wch, m1, g_re


def _fnet_ch_kernel(x_ref, w_ref, o_ref):
    gw = x_ref.shape[1]
    res = _dot(x_ref[...].astype(BF), w_ref[...])
    o_ref[0] = res[:, :gw].astype(o_ref.dtype)
    o_ref[1] = res[:, gw:].astype(o_ref.dtype)


def _fnet_s1_kernel(m_ref, y_ref, o_ref):
    n1, tc = y_ref.shape[1], y_ref.shape[2]
    o_ref[...] = _dot(m_ref[...], y_ref[...].reshape(2 * n1, tc)).astype(o_ref.dtype)


def _fnet_s2_kernel(g_ref, a_ref, o_ref, *, scale):
    kb, n2 = g_ref.shape[0], g_ref.shape[1]
    d = a_ref.shape[3]
    for j in range(kb):
        res = _dot(g_ref[j], a_ref[:, j].reshape(2 * n2, d))
        o_ref[:, j, :] = res * scale


def _fourier_mix(x, B, L):
    T, D = x.shape
    N2 = FFT_N2
    N1, wch, m1, g_re = _fnet_tables(L, D)
    gw = D // C_GROUPS
    tm = min(WIDE_TOKEN_TILE, T)
    y = pl.pallas_call(
        _fnet_ch_kernel, out_shape=jax.ShapeDtypeStruct((2, T, D), BF), grid=(T // tm, C_GROUPS),
        in_specs=[pl.BlockSpec((tm, gw), lambda i, c: (i, c)), pl.BlockSpec((gw, 2 * gw), lambda i, c: (0, 0))],
        out_specs=pl.BlockSpec((2, tm, gw), lambda i, c: (0, i, c)),
        compiler_params=_cp(("parallel", "parallel")), name="fnet_ch")(x, wch)
    cols = N2 * D
    tc = min(cols, FNET_COL_TILE)
    a = pl.pallas_call(
        _fnet_s1_kernel, out_shape=jax.ShapeDtypeStruct((B, 2 * N1, cols), BF), grid=(B, cols // tc),
        in_specs=[pl.BlockSpec((2 * N1, 2 * N1), lambda b, j: (0, 0)),
                  pl.BlockSpec((2, None, N1, tc), lambda b, j: (0, b, 0, j))],
        out_specs=pl.BlockSpec((None, 2 * N1, tc), lambda b, j: (b, 0, j)),
        compiler_params=_cp(("parallel", "parallel")), name="fnet_s1")(m1, y.reshape(2, B, N1, cols))
    kb = min(FNET_FREQ_PER_STEP, N1)
    f = pl.pallas_call(
        functools.partial(_fnet_s2_kernel, scale=1.0 / math.sqrt(L * gw)),
        out_shape=jax.ShapeDtypeStruct((B, N2, N1, D), F32), grid=(B, N1 // kb),
        in_specs=[pl.BlockSpec((kb, N2, 2 * N2), lambda b, k: (k, 0, 0)),
                  pl.BlockSpec((None, 2, kb, N2, D), lambda b, k: (b, 0, k, 0, 0))],
        out_specs=pl.BlockSpec((None, N2, kb, D), lambda b, k: (b, 0, k, 0)),
        compiler_params=_cp(("parallel", "parallel")), name="fnet_s2")(g_re, a.reshape(B, 2, N1, N2, D))
    return f.reshape(T, D)


def _router_kernel(w_ref, x_ref, o_ref):
    logits = _dot_nt(w_ref[...], x_ref[...].astype(BF))
    m = jnp.max(logits, axis=0, keepdims=True)
    p = jnp.exp(logits - m)
    o_ref[...] = p / jnp.sum(p, axis=0, keepdims=True)


def _router(x, w_router_t, tm=WIDE_TOKEN_TILE):
    T, D = x.shape
    E = w_router_t.shape[0]
    tm = min(tm, T)
    return pl.pallas_call(
        _router_kernel, out_shape=jax.ShapeDtypeStruct((E, T), F32), grid=(T // tm,),
        in_specs=[pl.BlockSpec((E, D), lambda i: (0, 0)), pl.BlockSpec((tm, D), lambda i: (i, 0))],
        out_specs=pl.BlockSpec((E, tm), lambda i: (0, i)), compiler_params=_cp(("parallel",)), name="router")(
            w_router_t, x)


def _prefix_counts(mask_f, upper, lower):
    rowcs = _dot(mask_f.astype(BF), upper)
    tot = jnp.broadcast_to(rowcs[:, LANES - 1:LANES], mask_f.shape)
    offs = _dot(lower, tot.astype(BF))
    return offs + rowcs - mask_f, offs


def _select_kernel(a_ref, rank_ref, off_ref, *, cap):
    v = a_ref[...]
    R = v.shape[0]
    bits = pltpu.bitcast(v, I32)

    def step(i, thr):
        cand = thr | (jnp.int32(1) << (30 - i))
        cnt = jnp.sum((bits >= cand).astype(F32))
        return jnp.where(cnt >= cap, cand, thr)

    thr = lax.fori_loop(0, 31, step, jnp.int32(0))
    gt = bits > thr
    eq = bits == thr
    need = cap - jnp.sum(gt.astype(F32))
    li = lax.broadcasted_iota(I32, (LANES, LANES), 0)
    lj = lax.broadcasted_iota(I32, (LANES, LANES), 1)
    upper = (li <= lj).astype(BF)
    ri = lax.broadcasted_iota(I32, (R, R), 0)
    rj = lax.broadcasted_iota(I32, (R, R), 1)
    lower = (rj < ri).astype(BF)
    eq_rank, _ = _prefix_counts(eq.astype(F32), upper, lower)
    sel = gt | (eq & (eq_rank < need))
    rank, offs = _prefix_counts(sel.astype(F32), upper, lower)
    rank_ref[...] = jnp.where(sel, rank.astype(I32), -1)
    off_ref[...] = offs.astype(I32)


def _select(aff, cap):
    E, T = aff.shape
    R = T // LANES
    spec = pl.BlockSpec((None, R, LANES), lambda e: (e, 0, 0))
    return pl.pallas_call(
        functools.partial(_select_kernel, cap=cap),
        out_shape=(jax.ShapeDtypeStruct((E, R, LANES), I32), jax.ShapeDtypeStruct((E, R, LANES), I32)),
        grid=(E,), in_specs=[spec], out_specs=(spec, spec), compiler_params=_cp(("parallel",)), name="select")(
            aff.reshape(E, R, LANES))


def _compact_kernel(off_ref, rank_ref, idx_ref, *, R):
    e = pl.program_id(0)
    idx_ref[...] = jnp.zeros_like(idx_ref)
    sub = lax.broadcasted_iota(I32, (8, LANES), 0)
    lane = lax.broadcasted_iota(I32, (8, LANES), 1)
    lhs = jnp.where(sub == 0, lane, jnp.where(sub == 1, 1, 0)).astype(BF)
    slot = lax.broadcasted_iota(I32, (2 * LANES, LANES), 0)

    def chunk(r, c):
        a = off_ref[e, r] // LANES
        local = rank_ref[pl.ds(r, 1), :] - a * LANES
        onehot = (slot == local).astype(BF)
        res = _dot_nt(lhs, onehot)
        val = (res[0:1] + res[1:2] * lax.convert_element_type(r * LANES, F32)).astype(I32)
        idx_ref[pl.ds(a, 1), :] += val[:, :LANES]
        idx_ref[pl.ds(a + 1, 1), :] += val[:, LANES:]
        return c

    lax.fori_loop(0, R, chunk, 0, unroll=8)


def _compact(rank, rowoff, cap):
    E, R, _ = rank.shape
    nrow = cap // LANES + 2
    return pl.pallas_call(
        functools.partial(_compact_kernel, R=R),
        out_shape=jax.ShapeDtypeStruct((E, nrow, LANES), I32),
        grid_spec=pltpu.PrefetchScalarGridSpec(
            num_scalar_prefetch=1, grid=(E,),
            in_specs=[pl.BlockSpec((None, R, LANES), lambda e, off: (e, 0, 0))],
            out_specs=pl.BlockSpec((None, nrow, LANES), lambda e, off: (e, 0, 0))),
        compiler_params=_cp(("arbitrary",)), name="compact")(rowoff, rank)


def _ffn_kernel(idx_ref, idxn_ref, x_hbm, wg_ref, wu_ref, wd_ref, o_ref, xbuf, xb, acc, sem, *, tm, nf, ntiles):
    f = pl.program_id(2)
    tile = pl.program_id(0) * pl.num_programs(1) + pl.program_id(1)
    slot = tile % 2
    part = tm // nf

    def row_copy(t, r, slot_):
        return pltpu.make_async_copy(x_hbm.at[pl.ds(t, 1)], xbuf.at[slot_, pl.ds(r, 1)], sem.at[slot_])

    def wait_tile(slot_):
        pltpu.make_async_copy(x_hbm.at[pl.ds(0, tm)], xbuf.at[slot_], sem.at[slot_]).wait()

    @pl.when((tile == 0) & (f == 0))
    def _():
        def issue(r, c):
            row_copy(idx_ref[0, 0, r], r, 0).start()
            return c

        lax.fori_loop(0, tm, issue, 0)

    @pl.when(f == 0)
    def _():
        wait_tile(slot)
        xb[...] = xbuf[slot].astype(BF)
        if nf > 1:
            acc[...] = jnp.zeros_like(acc)

    base = f * part
    for j in range(part):
        row_copy(idxn_ref[0, 0, base + j], base + j, 1 - slot).start()

    x = xb[...]
    g = _dot(x, wg_ref[...])
    u = _dot(x, wu_ref[...])
    hcur = (g * jax.nn.sigmoid(g)) * u
    down = _dot(hcur.astype(BF), wd_ref[...])
    if nf == 1:
        o_ref[...] = down.astype(o_ref.dtype)
    else:
        acc[...] += down

        @pl.when(f == nf - 1)
        def _():
            o_ref[...] = acc[...].astype(o_ref.dtype)

    @pl.when((tile == ntiles - 1) & (f == nf - 1))
    def _():
        wait_tile(1 - slot)


def _ffn(x, idx, wg, wu, wd, layer, cap, tm=FFN_ROW_TILE):
    T, D = x.shape
    E = wg.shape[1]
    FF = wg.shape[3]
    tm = min(tm, cap)
    nt = cap // tm
    fc, nf = FF, 1
    resident = pl.Buffered(1)
    idx3 = idx[:, :cap // LANES, :].reshape(E * nt, 1, tm)
    last = E * nt - 1
    return pl.pallas_call(
        functools.partial(_ffn_kernel, tm=tm, nf=nf, ntiles=E * nt),
        out_shape=jax.ShapeDtypeStruct((E, cap, D), BF), grid=(E, nt, nf),
        in_specs=[pl.BlockSpec((1, 1, tm), lambda e, i, f: (e * nt + i, 0, 0), memory_space=pltpu.SMEM),
                  pl.BlockSpec((1, 1, tm), lambda e, i, f: (jnp.minimum(e * nt + i + 1, last), 0, 0),
                               memory_space=pltpu.SMEM),
                  pl.BlockSpec(memory_space=pl.ANY),
                  pl.BlockSpec((None, None, D, fc), lambda e, i, f: (layer, e, 0, f), pipeline_mode=resident),
                  pl.BlockSpec((None, None, D, fc), lambda e, i, f: (layer, e, 0, f), pipeline_mode=resident),
                  pl.BlockSpec((None, None, fc, D), lambda e, i, f: (layer, e, f, 0), pipeline_mode=resident)],
        out_specs=pl.BlockSpec((None, tm, D), lambda e, i, f: (e, i, 0)),
        scratch_shapes=[pltpu.VMEM((2, tm, D), F32), pltpu.VMEM((tm, D), BF),
                        pltpu.VMEM((tm, D) if nf > 1 else (8, LANES), F32), pltpu.SemaphoreType.DMA((2,))],
        compiler_params=_cp(("arbitrary", "arbitrary", "arbitrary")), name="expert_ffn")(idx3, idx3, x, wg, wu, wd)


COMBINE_WIN = 64


def _combine_kernel(off_ref, rank_ref, aff_ref, x_ref, g_ref, b_ref, out_hbm, o_ref, wins, winx, ysc, sem, semx,
                    *, tt, E, cap, rpt, ntile, alpha):
    W = COMBINE_WIN
    ti = pl.program_id(0)
    slot = ti % 2

    def start0(e, ti_):
        r0 = off_ref[e, ti_ * rpt]
        return pl.multiple_of(jnp.minimum((r0 // 16) * 16, cap - W), 16)

    def win_copy(e, ti_, slot_):
        return pltpu.make_async_copy(out_hbm.at[e, pl.ds(start0(e, ti_), W)], wins.at[slot_, pl.ds(e * W, W)],
                                     sem.at[slot_])

    @pl.when(ti == 0)
    def _():
        for e in range(E):
            win_copy(e, 0, 0).start()

    @pl.when(ti + 1 < ntile)
    def _():
        for e in range(E):
            win_copy(e, ti + 1, 1 - slot).start()

    sub = lax.broadcasted_iota(I32, (W, tt), 0)

    def gated_onehot(r_row, g_row, first_row, min_rank):
        g_hi = g_row.astype(BF).astype(F32)
        match = (sub == r_row - first_row) & (r_row >= min_rank)
        return jnp.concatenate([jnp.where(match, g_hi, 0.0), jnp.where(match, g_row - g_hi, 0.0)], axis=1).astype(BF)

    def scatter_rows(p_t, rows):
        y2 = lax.dot_general(p_t, rows, (((0,), (0,)), ((), ())), preferred_element_type=F32)
        return y2[:tt] + y2[tt:]

    p_t = jnp.concatenate([gated_onehot(rank_ref[e:e + 1, :], aff_ref[e:e + 1, :], start0(e, ti), 0)
                           for e in range(E)], axis=0)
    for e in range(E):
        win_copy(e, ti, slot).wait()
    ysc[...] = alpha * x_ref[...] + scatter_rows(p_t, wins[slot])

    def extra(e, c):
        s0 = start0(e, ti)
        r_end = off_ref[e, (ti + 1) * rpt]

        @pl.when(r_end > s0 + W)
        def _():
            r_row = rank_ref[pl.ds(e, 1), :]
            g_row = aff_ref[pl.ds(e, 1), :]
            for w in range(1, tt // W + 1):
                lo_w = s0 + w * W

                @pl.when(r_end > lo_w)
                def _():
                    sw = pl.multiple_of(jnp.minimum(lo_w, cap - W), 16)
                    cp = pltpu.make_async_copy(out_hbm.at[e, pl.ds(sw, W)], winx, semx)
                    cp.start()
                    cp.wait()
                    ysc[...] += scatter_rows(gated_onehot(r_row, g_row, sw, lo_w), winx[...])

        return c

    lax.fori_loop(0, E, extra, 0)
    o_ref[...] = _ln_rows(ysc[...], g_ref[...], b_ref[...])


def _combine_ln(out, rank, aff, rowoff, x, g, b, cap, alpha):
    T, D = x.shape
    E = out.shape[0]
    W = COMBINE_WIN
    assert cap >= W
    tt = min(COMBINE_TOKEN_TILE, cap // 2)
    rpt = tt // LANES
    ntile = T // tt
    kern = functools.partial(_combine_kernel, tt=tt, E=E, cap=cap, rpt=rpt, ntile=ntile, alpha=alpha)
    return pl.pallas_call(
        kern, out_shape=jax.ShapeDtypeStruct((T, D), F32),
        grid_spec=pltpu.PrefetchScalarGridSpec(
            num_scalar_prefetch=1, grid=(ntile,),
            in_specs=[pl.BlockSpec((E, tt), lambda i, off: (0, i)),
                      pl.BlockSpec((E, tt), lambda i, off: (0, i)),
                      pl.BlockSpec((tt, D), lambda i, off: (i, 0)),
                      pl.BlockSpec((1, D), lambda i, off: (0, 0)),
                      pl.BlockSpec((1, D), lambda i, off: (0, 0)),
                      pl.BlockSpec(memory_space=pl.ANY)],
            out_specs=pl.BlockSpec((tt, D), lambda i, off: (i, 0)),
            scratch_shapes=[pltpu.VMEM((2, E * W, D), BF), pltpu.VMEM((W, D), BF), pltpu.VMEM((tt, D), F32),
                            pltpu.SemaphoreType.DMA((2,)), pltpu.SemaphoreType.DMA]),
        compiler_params=_cp(("arbitrary",)), name="combine_ln",
    )(rowoff, rank, aff, x, g.reshape(1, D), b.reshape(1, D), out)


def _expert_choice_ln(x, w_router_t, wg, wu, wd, layer, g, b, alpha):
    T, D = x.shape
    E = N_EXPERTS
    cap = max(1, EC_FACTOR * T // E)
    aff = _router(x, w_router_t)
    rank, rowoff = _select(aff, cap)
    rowoff = jnp.concatenate([rowoff[:, :, 0], jnp.full((E, 1), cap, I32)], axis=1)
    idx = _compact(rank, rowoff, cap)
    out = _ffn(x, idx, wg, wu, wd, layer, cap)
    return _combine_ln(out, rank.reshape(E, T), aff, rowoff, x, g, b, cap, alpha)


def _trunk(x, p, depth):
    B, L, D = x.shape
    T = B * L
    alpha = (2 * depth) ** 0.25
    x = x.reshape(T, D)
    filt_cache = {}
    for layer in range(depth):
        j = layer // 2
        if layer % 2 == 0:
            hproj = _xw(x, p["ab_w_in"][j], name="ab_in").reshape(B, L, -1)
            lambda_init = 0.8 - 0.6 * math.exp(-0.3 * layer)
            lf = p["diff_lambda"][j].astype(F32)
            lam_full = jnp.exp(jnp.sum(lf[0] * lf[1])) - jnp.exp(jnp.sum(lf[2] * lf[3])) + lambda_init
            t_attn = min(ATTN_TILE, L)
            if t_attn not in p["bias_tiles"]:
                p["bias_tiles"][t_attn] = _bias_tiles(p["rel_bias"], t_attn)
            a_out = _diff_attention(hproj, lam_full, p["diff_subln_g"][j], p["bias_tiles"][t_attn], lambda_init)
            uc = _short_conv(hproj, p["hy_conv_w"][j], p["hy_conv_b"][j])
            filt, filt_sum = _filter_mlp(L, p["hy_f_w1"][j], p["hy_f_b1"][j], p["hy_f_w2"][j], p["hy_f_b2"][j],
                                         p["hy_f_w3"][j], p["hy_f_freq"][j], p["hy_decay"][j])
            b_out = _hyena(uc, filt, filt_sum, p["hy_skip"][j])
            w_out = p["ab_w_out"][j]
            pairs = [(a_out.reshape(T, D_A), w_out[:D_A]), (b_out.reshape(T, D_B), w_out[D_A:])]
        else:
            pairs = [(_fourier_mix(x, B, L), p["c_w_out"][j])]
        x = _mm_res_ln(pairs, x, p["ln_g"][layer, 0], p["ln_b"][layer, 0], alpha)
        x = _expert_choice_ln(x, p["ec_router_t"][layer], p["ec_w_gate"], p["ec_w_up"], p["ec_w_down"], layer,
                              p["ln_g"][layer, 1], p["ln_b"][layer, 1], alpha)
    return x.reshape(B, L, D)


def kernel(x_prompt, x_sample, rel_bias, ab_w_in, ab_w_out, diff_lambda, diff_subln_g, hy_conv_w, hy_conv_b,
           hy_f_w1, hy_f_b1, hy_f_w2, hy_f_b2, hy_f_w3, hy_f_freq, hy_decay, hy_skip,
           c_w_out, ec_router, ec_w_gate, ec_w_up, ec_w_down, ln_g, ln_b):
    depth = ec_router.shape[0]
    p = dict(
        rel_bias=rel_bias, ab_w_in=ab_w_in.astype(BF), ab_w_out=ab_w_out.astype(BF), diff_lambda=diff_lambda,
        diff_subln_g=diff_subln_g, hy_conv_w=hy_conv_w, hy_conv_b=hy_conv_b, hy_f_w1=hy_f_w1, hy_f_b1=hy_f_b1,
        hy_f_w2=hy_f_w2, hy_f_b2=hy_f_b2, hy_f_w3=hy_f_w3, hy_f_freq=hy_f_freq, hy_decay=hy_decay, hy_skip=hy_skip,
        c_w_out=c_w_out.astype(BF), ec_router_t=jnp.swapaxes(ec_router, 1, 2).astype(BF),
        ec_w_gate=ec_w_gate.astype(BF), ec_w_up=ec_w_up.astype(BF), ec_w_down=ec_w_down.astype(BF),
        ln_g=ln_g.astype(F32), ln_b=ln_b.astype(F32), bias_tiles={})
    return _trunk(x_prompt, p, depth), _trunk(x_sample, p, depth)
```

```python
import functools
import math

import jax
import jax.numpy as jnp
from jax import lax
from jax.experimental import pallas as pl
from jax.experimental.pallas import tpu as pltpu

BF = jnp.bfloat16
F32 = jnp.float32
I32 = jnp.int32

A_HEADS = 4
A_DV = 128
A_DK = 64
D_A = 512
D_B = 512
B_ORDER = 2
REL_BUCKETS = 32
REL_MAX_DIST = 128
FILTER_BANDS = 16
N_EXPERTS = 16
EC_FACTOR = 2
LN_EPS = 1e-5
C_GROUPS = 4
LANES = 128
BF16_SUBLANES = 16
FFT_N2 = 128
VMEM_LIMIT = 52 * 1024 * 1024

TOKEN_TILE = 512
WIDE_TOKEN_TILE = 1024
FFN_ROW_TILE = 512
COMBINE_TOKEN_TILE = 256
FILTER_ROW_TILE = 512
HY_COL_TILE = 4096
FNET_COL_TILE = 8192
FNET_FREQ_PER_STEP = 8


def _cp(sem, vmem=VMEM_LIMIT):
    return pltpu.CompilerParams(dimension_semantics=sem, vmem_limit_bytes=vmem)


def _dot(a, b):
    return jnp.dot(a, b, preferred_element_type=F32)


def _dot_nt(a, b):
    return lax.dot_general(a, b, (((1,), (1,)), ((), ())), preferred_element_type=F32)


def _mm_kernel(a_ref, b_ref, o_ref, *scratch, nk, kaxis, scale):
    prod = _dot(a_ref[...].astype(BF), b_ref[...].astype(BF))
    if nk == 1:
        o_ref[...] = (prod * scale).astype(o_ref.dtype)
        return
    acc = scratch[0]
    k = pl.program_id(kaxis)

    @pl.when(k == 0)
    def _():
        acc[...] = jnp.zeros_like(acc)

    acc[...] += prod

    @pl.when(k == nk - 1)
    def _():
        o_ref[...] = (acc[...] * scale).astype(o_ref.dtype)


def _mm(a, b, out_shape, grid, a_spec, b_spec, o_spec, mn, *, nk=1, kaxis=None, scale=1.0, name):
    sem = tuple("arbitrary" if i == kaxis else "parallel" for i in range(len(grid)))
    return pl.pallas_call(
        functools.partial(_mm_kernel, nk=nk, kaxis=kaxis, scale=scale),
        out_shape=out_shape, grid=grid, in_specs=[a_spec, b_spec], out_specs=o_spec,
        scratch_shapes=[pltpu.VMEM(mn, F32)] if nk > 1 else [],
        compiler_params=_cp(sem), name=name)(a, b)


def _xw(x, w, tm=TOKEN_TILE, name="xw"):
    M, K = x.shape
    N = w.shape[1]
    tm = min(tm, M)
    return _mm(x, w, jax.ShapeDtypeStruct((M, N), F32), (M // tm,),
               pl.BlockSpec((tm, K), lambda i: (i, 0)), pl.BlockSpec((K, N), lambda i: (0, 0)),
               pl.BlockSpec((tm, N), lambda i: (i, 0)), (tm, N), name=name)


def _ln_rows(y, g, b):
    mu = jnp.mean(y, -1, keepdims=True)
    d = y - mu
    var = jnp.mean(d * d, -1, keepdims=True)
    return d * lax.rsqrt(var + LN_EPS) * g + b


def _mm_res_ln_kernel(*refs, npairs, alpha):
    x_ref, g_ref, b_ref = refs[2 * npairs:2 * npairs + 3]
    o_ref = refs[2 * npairs + 3]
    y = alpha * x_ref[...]
    for i in range(npairs):
        y = y + _dot(refs[2 * i][...].astype(BF), refs[2 * i + 1][...])
    o_ref[...] = _ln_rows(y, g_ref[...], b_ref[...])


def _mm_res_ln(pairs, x, g, b, alpha, tm=TOKEN_TILE, name="mm_res_ln"):
    M, D = x.shape
    tm = min(tm, M)
    specs, args = [], []
    for a, w in pairs:
        specs += [pl.BlockSpec((tm, a.shape[1]), lambda i: (i, 0)), pl.BlockSpec(w.shape, lambda i: (0, 0))]
        args += [a, w]
    specs += [pl.BlockSpec((tm, D), lambda i: (i, 0)), pl.BlockSpec((1, D), lambda i: (0, 0)),
              pl.BlockSpec((1, D), lambda i: (0, 0))]
    args += [x, g.reshape(1, D), b.reshape(1, D)]
    return pl.pallas_call(
        functools.partial(_mm_res_ln_kernel, npairs=len(pairs), alpha=alpha),
        out_shape=jax.ShapeDtypeStruct((M, D), F32), grid=(M // tm,), in_specs=specs,
        out_specs=pl.BlockSpec((tm, D), lambda i: (i, 0)), compiler_params=_cp(("parallel",)), name=name)(*args)


def _rel_bucket(rel):
    nb = REL_BUCKETS // 2
    max_exact = nb // 2
    n = jnp.abs(rel)
    large = max_exact + (jnp.log(jnp.maximum(n, 1).astype(F32) / max_exact)
                         / math.log(REL_MAX_DIST / max_exact) * (nb - max_exact)).astype(I32)
    large = jnp.minimum(large, nb - 1)
    return jnp.where(rel > 0, nb, 0) + jnp.where(n < max_exact, n, large)


LOG2E = 1.4426950408889634


def _bias_tiles(rel_table, T):
    assert T >= REL_MAX_DIST
    i = jnp.arange(T)
    rel = (jnp.arange(-2, 3)[:, None, None] * T + i[None, :, None]) - i[None, None, :]
    bucket = _rel_bucket(rel)[None]
    tab = LOG2E * rel_table.astype(F32).T[:, :, None, None, None]
    out = jnp.zeros((rel_table.shape[1],) + rel.shape, F32)
    for b in range(REL_BUCKETS):
        out = jnp.where(bucket == b, tab[:, b], out)
    return out


ATTN_MAX_DRIFT = 40.0
ATTN_CHUNK_GROUP = 3
ATTN_TILE = 512


def _attn_kernel(lam_ref, q_ref, k_ref, v_ref, bias_ref, g_ref, o_ref, kb, vt, m_s, a_s, *, T, nk, out_scale):
    qi = pl.program_id(2)
    dv = A_DV

    @pl.when(qi == 0)
    def _():
        def prep(c, carry):
            st = pl.multiple_of(c * T, T)
            kb[c] = k_ref[pl.ds(st, T), :].astype(BF)
            vt[c, :dv] = v_ref[pl.ds(st, T), :].T.astype(BF)
            vt[c, dv:] = jnp.ones((vt.shape[1] - dv, T), BF)
            return carry

        lax.fori_loop(0, nk, prep, 0)

    qt = (q_ref[...] * (A_DK ** -0.5 * LOG2E)).T
    row = lax.broadcasted_iota(I32, qt.shape, 0)
    qs = (jnp.where(row < A_DK, qt, 0.0).astype(BF), jnp.where(row >= A_DK, qt, 0.0).astype(BF))

    def qk(kj, mi):
        return _dot(kb[kj], qs[mi])

    def scores(kj, mi):
        return qk(kj, mi) + bias_ref[jnp.clip(kj - qi, -2, 2) + 2]

    def rescaling_chunk(kj, first):
        for mi in range(2):
            s = scores(kj, mi)
            m_new = jnp.max(s, axis=0, keepdims=True)
            if not first:
                m_new = jnp.maximum(m_s[mi], m_new)
            pv = _dot(vt[kj], jnp.exp2(s - m_new).astype(BF))
            if first:
                a_s[mi] = pv
            else:
                a_s[mi] = jnp.exp2(m_s[mi] - m_new) * a_s[mi] + pv
            m_s[mi] = m_new

    rescaling_chunk(0, True)

    def lazy_chunks(kjs, near):
        pv, drift = [None, None], None
        if not near:
            far_bias = jnp.where(kjs[-1] < qi, bias_ref[0, 0:1, :], bias_ref[4, 0:1, :])
        for kj in kjs:
            for mi in range(2):
                if near:
                    s, m_ref = scores(kj, mi), m_s[mi]
                else:
                    s, m_ref = qk(kj, mi), m_s[mi] - far_bias
                acc = _dot(vt[kj], jnp.exp2(s - m_ref).astype(BF))
                pv[mi] = acc if pv[mi] is None else pv[mi] + acc
                d = jnp.max(s, axis=0, keepdims=True) - m_ref
                drift = d if drift is None else jnp.maximum(drift, d)
        over = jnp.max(drift) > ATTN_MAX_DRIFT

        @pl.when(jnp.logical_not(over))
        def _():
            for mi in range(2):
                a_s[mi] += pv[mi]

        @pl.when(over)
        def _():
            for kj in kjs:
                rescaling_chunk(kj, False)

    def near_or_far(kjs):
        is_near = (kjs[0] - 1 <= qi) & (qi <= kjs[-1] + 1)

        @pl.when(is_near)
        def _():
            lazy_chunks(kjs, True)

        @pl.when(jnp.logical_not(is_near))
        def _():
            lazy_chunks(kjs, False)

    group = ATTN_CHUNK_GROUP

    def grouped(i, c):
        near_or_far([1 + group * i + u for u in range(group)])
        return c

    ngroups = (nk - 1) // group
    lax.fori_loop(0, ngroups, grouped, 0)
    if (nk - 1) % group:
        near_or_far(list(range(1 + group * ngroups, nk)))
    a0, a1 = a_s[0], a_s[1]
    o = a0[:dv] / a0[dv:dv + 1] - lam_ref[0] * (a1[:dv] / a1[dv:dv + 1])
    o = o * lax.rsqrt(jnp.mean(o * o, 0, keepdims=True) + LN_EPS) * g_ref[...] * out_scale
    o_ref[...] = o.T


def _diff_attention(hproj, lam_full, sub_g, bias, lambda_init):
    B, L, _ = hproj.shape
    T = bias.shape[-1]
    H = A_HEADS
    return pl.pallas_call(
        functools.partial(_attn_kernel, T=T, nk=L // T, out_scale=1.0 - lambda_init),
        out_shape=jax.ShapeDtypeStruct((B, L, D_A), F32), grid=(B, H, L // T),
        in_specs=[pl.BlockSpec(memory_space=pltpu.SMEM),
                  pl.BlockSpec((None, T, LANES), lambda b, h, i: (b, i, h)),
                  pl.BlockSpec((None, L, LANES), lambda b, h, i: (b, 0, H + h)),
                  pl.BlockSpec((None, L, LANES), lambda b, h, i: (b, 0, 2 * H + h)),
                  pl.BlockSpec((None, 5, T, T), lambda b, h, i: (h, 0, 0, 0)),
                  pl.BlockSpec((A_DV, 1), lambda b, h, i: (0, 0))],
        out_specs=pl.BlockSpec((None, T, LANES), lambda b, h, i: (b, i, h)),
        scratch_shapes=[pltpu.VMEM((L // T, T, LANES), BF), pltpu.VMEM((L // T, A_DV + BF16_SUBLANES, T), BF),
                        pltpu.VMEM((2, 1, T), F32), pltpu.VMEM((2, A_DV + BF16_SUBLANES, T), F32)],
        compiler_params=_cp(("parallel", "parallel", "arbitrary")), name="diff_attn",
    )(lam_full.reshape(1), hproj, hproj, hproj, bias, sub_g.astype(F32).reshape(A_DV, 1))


def _short_conv_kernel(u_ref, w_ref, b_ref, o_ref):
    u = u_ref[...]
    L = u.shape[0]
    row = lax.broadcasted_iota(I32, u.shape, 0)
    up = jnp.where(row == 0, 0.0, pltpu.roll(u, 1, 0))
    dn = jnp.where(row == L - 1, 0.0, pltpu.roll(u, L - 1, 0))
    w = w_ref[...]
    o_ref[...] = up * w[0:1] + u * w[1:2] + dn * w[2:3] + b_ref[...]


def _short_conv(hproj, conv_w, conv_b):
    B, L, _ = hproj.shape
    nblk = 3 * D_B // LANES
    off = 3 * D_A // LANES
    per = D_B // LANES
    return pl.pallas_call(
        _short_conv_kernel, out_shape=jax.ShapeDtypeStruct((3, B, L, D_B), F32), grid=(B, nblk),
        in_specs=[pl.BlockSpec((None, L, LANES), lambda b, j: (b, 0, off + j)),
                  pl.BlockSpec((3, LANES), lambda b, j: (0, j)),
                  pl.BlockSpec((1, LANES), lambda b, j: (0, j))],
        out_specs=pl.BlockSpec((None, None, L, LANES), lambda b, j: (j // per, b, 0, j % per)),
        compiler_params=_cp(("parallel", "parallel")), name="short_conv",
    )(hproj, conv_w.astype(F32), conv_b.astype(F32).reshape(1, -1))


def _filter_mlp_kernel(z_ref, w1_ref, b1_ref, w2_ref, b2_ref, w3_ref, fq_ref, dec_ref, h_ref, s_ref):
    i = pl.program_id(0)
    z = z_ref[...]
    fq = fq_ref[...]
    h = jnp.sin(fq[0:1] * (_dot(z.astype(BF), w1_ref[...]) + b1_ref[...]))
    h = jnp.sin(fq[1:2] * (_dot(h.astype(BF), w2_ref[...]) + b2_ref[...]))
    t = z[:, 0:1]
    h = _dot(h.astype(BF), w3_ref[...]) * jnp.exp(-t * jnp.abs(dec_ref[...]))

    @pl.when(i == 0)
    def _():
        s_ref[...] = jnp.zeros_like(s_ref)

    s_ref[...] += jnp.sum(jnp.abs(h), axis=0, keepdims=True)
    row = lax.broadcasted_iota(I32, h.shape, 0)
    col = lax.broadcasted_iota(I32, h.shape, 1)
    half = h.shape[1] // 2
    h_ref[...] = jnp.where((row + i * h.shape[0] == 0) & (col >= half), 0.0, h)


def _filter_mlp(L, w1, b1, w2, b2, w3, freq, decay, tl=FILTER_ROW_TILE):
    t = jnp.linspace(0.0, 1.0, L, dtype=F32)[:, None]
    wpos = 2.0 * math.pi * jnp.arange(L, dtype=F32)[:, None] / L
    fr = jnp.linspace(1e-4, FILTER_BANDS - 1, FILTER_BANDS, dtype=F32)[None, :]
    z = jnp.concatenate([t, jnp.cos(fr * wpos), -jnp.sin(fr * wpos)], -1)
    emb, width = w1.shape
    ch = w3.shape[1]
    z = jnp.pad(z, ((0, 0), (0, LANES - emb)))
    padw = LANES - width
    w1p = jnp.pad(w1, ((0, LANES - emb), (0, padw))).astype(BF)
    w2p = jnp.pad(w2, ((0, padw), (0, padw))).astype(BF)
    w3p = jnp.pad(w3, ((0, padw), (0, 0))).astype(BF)
    b1p = jnp.pad(b1.astype(F32), (0, padw)).reshape(1, LANES)
    b2p = jnp.pad(b2.astype(F32), (0, padw)).reshape(1, LANES)
    fqp = jnp.pad(freq.astype(F32), ((0, 0), (0, padw)))
    tl = min(tl, L)
    full = lambda shp: pl.BlockSpec(shp, lambda i: (0, 0))
    return pl.pallas_call(
        _filter_mlp_kernel,
        out_shape=(jax.ShapeDtypeStruct((L, ch), F32), jax.ShapeDtypeStruct((1, ch), F32)), grid=(L // tl,),
        in_specs=[pl.BlockSpec((tl, LANES), lambda i: (i, 0)), full((LANES, LANES)), full((1, LANES)),
                  full((LANES, LANES)), full((1, LANES)), full((LANES, ch)), full((2, LANES)), full((1, ch))],
        out_specs=(pl.BlockSpec((tl, ch), lambda i: (i, 0)), full((1, ch))),
        compiler_params=_cp(("arbitrary",)), name="filter_mlp",
    )(z, w1p, b1p, w2p, b2p, w3p, fqp, decay.astype(F32).reshape(1, ch))


def _cis(idx, n):
    ang = (2.0 * math.pi / n) * (idx % n).astype(F32)
    return jnp.cos(ang), -jnp.sin(ang)


def _blk(re, im):
    return jnp.concatenate([jnp.concatenate([re, -im], -1), jnp.concatenate([im, re], -1)], -2)


def _dft_tables(N):
    N2 = FFT_N2
    N1 = N // N2
    a = jnp.arange(N1)
    b = jnp.arange(N2)
    f1r, f1i = _cis(a[:, None] * a[None, :], N1)
    tr, ti = _cis(a[:, None] * b[None, :], N)
    f2r, f2i = _cis(b[:, None] * b[None, :], N2)
    gr = tr[:, None, :] * f2r[None] - ti[:, None, :] * f2i[None]
    gi = tr[:, None, :] * f2i[None] + ti[:, None, :] * f2r[None]
    return N1, f1r, f1i, gr, gi


def _mid_kernel(g_ref, gi_ref, kf_ref, a_ref, o_ref):
    n2, c = a_ref.shape[2], a_ref.shape[3]
    kr, ki = kf_ref[0], kf_ref[1]
    for p in range(a_ref.shape[0]):
        x = _dot(g_ref[...], a_ref[p].reshape(2 * n2, c).astype(BF))
        xr, xi = x[:n2], x[n2:]
        y = jnp.concatenate([xr * kr - xi * ki, xr * ki + xi * kr], axis=0).astype(BF)
        o_ref[p] = _dot(gi_ref[...], y).reshape(2, n2, c).astype(o_ref.dtype)


def _filt_mid_kernel(g_ref, a_ref, s_ref, o_ref):
    n2 = a_ref.shape[1]
    c = a_ref.shape[2] // 2
    x = _dot(g_ref[...], a_ref[...].reshape(2 * n2, 2 * c).astype(BF))
    s = s_ref[...]
    inv = 1.0 / (s[:, :c] + s[:, c:])
    o_ref[0] = (x[:n2, :c] + x[:n2, c:]) * inv
    o_ref[1] = (x[n2:, :c] - x[n2:, c:]) * inv


def _hy_s1_kernel(m_ref, z_ref, o_ref):
    c = z_ref.shape[2]
    for t in range(z_ref.shape[1]):
        o_ref[:, t * c:(t + 1) * c] = _dot(m_ref[...], z_ref[:, t, :].astype(BF)).astype(o_ref.dtype)


def _gate_inv_kernel(m_ref, d_ref, z_ref, gate_ref, sk_ref, o_ref, *, scale):
    conv = _dot(m_ref[...], d_ref[...].astype(BF)) * scale
    c = z_ref.shape[2]
    for t in range(z_ref.shape[1]):
        o_ref[:, t, :] = gate_ref[:, t, :] * (conv[:, t * c:(t + 1) * c] + z_ref[:, t, :] * sk_ref[...])


def _hyena(uc, filt, filt_sum, skip):
    _, B, L, C = uc.shape
    assert B % 2 == 0
    P = B // 2
    N = 2 * L
    N2 = FFT_N2
    N1, f1r, f1i, gr, gi = _dft_tables(N)
    h = N1 // 2
    cols = N2 * C
    m1 = jnp.concatenate([jnp.concatenate([f1r[:, :h], -f1i[:, :h]], 1),
                          jnp.concatenate([f1i[:, :h], f1r[:, :h]], 1)], 0).astype(BF)
    m1_real = jnp.concatenate([f1r[:, :h], f1i[:, :h]], 0).astype(BF)
    m1_inv = jnp.concatenate([jnp.concatenate([f1r[:h], f1i[:h]], 1),
                              jnp.concatenate([-f1i[:h], f1r[:h]], 1)], 0).astype(BF)
    g_fwd = _blk(gr, gi).astype(BF)
    g_inv = _blk(jnp.swapaxes(gr, 1, 2), -jnp.swapaxes(gi, 1, 2)).astype(BF)

    tc = min(cols, HY_COL_TILE)
    CF = filt.shape[1]
    fcols = N2 * CF
    tcf = min(fcols, FNET_COL_TILE)
    fa = _mm(m1_real, filt.reshape(h, fcols), jax.ShapeDtypeStruct((2 * N1, fcols), BF), (fcols // tcf,),
             pl.BlockSpec((2 * N1, h), lambda j: (0, 0)), pl.BlockSpec((h, tcf), lambda j: (0, j)),
             pl.BlockSpec((2 * N1, tcf), lambda j: (0, j)), None, name="filt_s1")
    kf = pl.pallas_call(
        _filt_mid_kernel, out_shape=jax.ShapeDtypeStruct((N1, 2, N2, CF // 2), F32), grid=(N1,),
        in_specs=[pl.BlockSpec((None, 2 * N2, 2 * N2), lambda k: (k, 0, 0)),
                  pl.BlockSpec((2, None, N2, CF), lambda k: (0, k, 0, 0)),
                  pl.BlockSpec((1, CF), lambda k: (0, 0))],
        out_specs=pl.BlockSpec((None, 2, N2, CF // 2), lambda k: (k, 0, 0, 0)),
        compiler_params=_cp(("parallel",)), name="filt_mid",
    )(g_fwd, fa.reshape(2, N1, N2, CF), filt_sum)

    nb = tc // C
    uc5 = uc.reshape(3, P, N1, N2, C)

    def natural(plane):
        if plane is None:
            return pl.BlockSpec((None, N1, nb, C), lambda p, j: (p, 0, j, 0))
        return pl.BlockSpec((None, None, N1, nb, C), lambda p, j: (plane, p, 0, j, 0))

    z, z_plane = uc5, 0
    for n in range(B_ORDER):
        a = pl.pallas_call(
            _hy_s1_kernel, out_shape=jax.ShapeDtypeStruct((P, 2 * N1, cols), BF), grid=(P, cols // tc),
            in_specs=[pl.BlockSpec((2 * N1, N1), lambda p, j: (0, 0)), natural(z_plane)],
            out_specs=pl.BlockSpec((None, 2 * N1, tc), lambda p, j: (p, 0, j)),
            compiler_params=_cp(("parallel", "parallel")), name="hy_s1")(m1, z)
        d = pl.pallas_call(
            _mid_kernel, out_shape=jax.ShapeDtypeStruct((P, 2, N1, N2, C), BF), grid=(N1,),
            in_specs=[pl.BlockSpec((None, 2 * N2, 2 * N2), lambda k: (k, 0, 0)),
                      pl.BlockSpec((None, 2 * N2, 2 * N2), lambda k: (k, 0, 0)),
                      pl.BlockSpec((None, 2, N2, C), lambda k: (k, 0, 0, n)),
                      pl.BlockSpec((P, 2, None, N2, C), lambda k: (0, 0, k, 0, 0))],
            out_specs=pl.BlockSpec((P, 2, None, N2, C), lambda k: (0, 0, k, 0, 0)),
            compiler_params=_cp(("parallel",)), name="hy_mid",
        )(g_fwd, g_inv, kf, a.reshape(P, 2, N1, N2, C))
        z = pl.pallas_call(
            functools.partial(_gate_inv_kernel, scale=1.0 / N),
            out_shape=jax.ShapeDtypeStruct((P, N1, N2, C), F32), grid=(P, cols // tc),
            in_specs=[pl.BlockSpec((N1, 2 * N1), lambda p, j: (0, 0)),
                      pl.BlockSpec((None, 2 * N1, tc), lambda p, j: (p, 0, j)),
                      natural(z_plane), natural(1 + n),
                      pl.BlockSpec((1, C), lambda p, j: (0, 0))],
            out_specs=natural(None),
            compiler_params=_cp(("parallel", "parallel")), name="hy_inv_gate",
        )(m1_inv, d.reshape(P, 2 * N1, cols), z, uc5, skip.astype(F32)[n:n + 1])
        z_plane = None
    return z.reshape(B, L, C)


def _fnet_tables(L, D):
    gw = D // C_GROUPS
    c = jnp.arange(gw)
    cr, ci = _cis(c[:, None] * c[None, :], gw)
    wch = jnp.concatenate([cr, ci], 1).astype(BF)
    N1, f1r, f1i, gr, gi = _dft_tables(L)
    m1 = _blk(f1r, f1i).astype(BF)
    g_re = jnp.concatenate([gr, -gi], -1).astype(BF)
    return N1, wch, m1, g_re


def _fnet_ch_kernel(x_ref, w_ref, o_ref):
    gw = x_ref.shape[1]
    res = _dot(x_ref[...].astype(BF), w_ref[...])
    o_ref[0] = res[:, :gw].astype(o_ref.dtype)
    o_ref[1] = res[:, gw:].astype(o_ref.dtype)


def _fnet_s1_kernel(m_ref, y_ref, o_ref):
    n1, tc = y_ref.shape[1], y_ref.shape[2]
    o_ref[...] = _dot(m_ref[...], y_ref[...].reshape(2 * n1, tc)).astype(o_ref.dtype)


def _fnet_s2_kernel(g_ref, a_ref, o_ref, *, scale):
    kb, n2 = g_ref.shape[0], g_ref.shape[1]
    d = a_ref.shape[3]
    for j in range(kb):
        res = _dot(g_ref[j], a_ref[:, j].reshape(2 * n2, d))
        o_ref[:, j, :] = res * scale


def _fourier_mix(x, B, L):
    T, D = x.shape
    N2 = FFT_N2
    N1, wch, m1, g_re = _fnet_tables(L, D)
    gw = D // C_GROUPS
    tm = min(WIDE_TOKEN_TILE, T)
    y = pl.pallas_call(
        _fnet_ch_kernel, out_shape=jax.ShapeDtypeStruct((2, T, D), BF), grid=(T // tm, C_GROUPS),
        in_specs=[pl.BlockSpec((tm, gw), lambda i, c: (i, c)), pl.BlockSpec((gw, 2 * gw), lambda i, c: (0, 0))],
        out_specs=pl.BlockSpec((2, tm, gw), lambda i, c: (0, i, c)),
        compiler_params=_cp(("parallel", "parallel")), name="fnet_ch")(x, wch)
    cols = N2 * D
    tc = min(cols, FNET_COL_TILE)
    a = pl.pallas_call(
        _fnet_s1_kernel, out_shape=jax.ShapeDtypeStruct((B, 2 * N1, cols), BF), grid=(B, cols // tc),
        in_specs=[pl.BlockSpec((2 * N1, 2 * N1), lambda b, j: (0, 0)),
                  pl.BlockSpec((2, None, N1, tc), lambda b, j: (0, b, 0, j))],
        out_specs=pl.BlockSpec((None, 2 * N1, tc), lambda b, j: (b, 0, j)),
        compiler_params=_cp(("parallel", "parallel")), name="fnet_s1")(m1, y.reshape(2, B, N1, cols))
    kb = min(FNET_FREQ_PER_STEP, N1)
    f = pl.pallas_call(
        functools.partial(_fnet_s2_kernel, scale=1.0 / math.sqrt(L * gw)),
        out_shape=jax.ShapeDtypeStruct((B, N2, N1, D), F32), grid=(B, N1 // kb),
        in_specs=[pl.BlockSpec((kb, N2, 2 * N2), lambda b, k: (k, 0, 0)),
                  pl.BlockSpec((None, 2, kb, N2, D), lambda b, k: (b, 0, k, 0, 0))],
        out_specs=pl.BlockSpec((None, N2, kb, D), lambda b, k: (b, 0, k, 0)),
        compiler_params=_cp(("parallel", "parallel")), name="fnet_s2")(g_re, a.reshape(B, 2, N1, N2, D))
    return f.reshape(T, D)


def _router_kernel(w_ref, x_ref, o_ref):
    logits = _dot_nt(w_ref[...], x_ref[...].astype(BF))
    m = jnp.max(logits, axis=0, keepdims=True)
    p = jnp.exp(logits - m)
    o_ref[...] = p / jnp.sum(p, axis=0, keepdims=True)


def _router(x, w_router_t, tm=WIDE_TOKEN_TILE):
    T, D = x.shape
    E = w_router_t.shape[0]
    tm = min(tm, T)
    return pl.pallas_call(
        _router_kernel, out_shape=jax.ShapeDtypeStruct((E, T), F32), grid=(T // tm,),
        in_specs=[pl.BlockSpec((E, D), lambda i: (0, 0)), pl.BlockSpec((tm, D), lambda i: (i, 0))],
        out_specs=pl.BlockSpec((E, tm), lambda i: (0, i)), compiler_params=_cp(("parallel",)), name="router")(
            w_router_t, x)


def _prefix_counts(mask_f, upper, lower):
    rowcs = _dot(mask_f.astype(BF), upper)
    tot = jnp.broadcast_to(rowcs[:, LANES - 1:LANES], mask_f.shape)
    offs = _dot(lower, tot.astype(BF))
    return offs + rowcs - mask_f, offs


def _select_kernel(a_ref, rank_ref, off_ref, *, cap):
    v = a_ref[...]
    R = v.shape[0]
    bits = pltpu.bitcast(v, I32)

    def enough(cand):
        return jnp.sum((bits >= cand).astype(F32)) >= cap

    def step(i, thr):
        b0 = jnp.int32(1) << (28 - 2 * i)
        b1 = b0 << 1
        return jnp.where(enough(thr | b1 | b0), thr | b1 | b0,
                         jnp.where(enough(thr | b1), thr | b1, jnp.where(enough(thr | b0), thr | b0, thr)))

    top = jnp.int32(1 << 30)
    thr = lax.fori_loop(0, 15, step, jnp.where(enough(top), top, jnp.int32(0)))
    gt = bits > thr
    eq = bits == thr
    need = cap - jnp.sum(gt.astype(F32))
    li = lax.broadcasted_iota(I32, (LANES, LANES), 0)
    lj = lax.broadcasted_iota(I32, (LANES, LANES), 1)
    upper = (li <= lj).astype(BF)
    ri = lax.broadcasted_iota(I32, (R, R), 0)
    rj = lax.broadcasted_iota(I32, (R, R), 1)
    lower = (rj < ri).astype(BF)
    eq_rank, _ = _prefix_counts(eq.astype(F32), upper, lower)
    sel = gt | (eq & (eq_rank < need))
    rank, offs = _prefix_counts(sel.astype(F32), upper, lower)
    rank_ref[...] = jnp.where(sel, rank.astype(I32), -1)
    off_ref[...] = offs.astype(I32)


def _select(aff, cap):
    E, T = aff.shape
    R = T // LANES
    spec = pl.BlockSpec((None, R, LANES), lambda e: (e, 0, 0))
    return pl.pallas_call(
        functools.partial(_select_kernel, cap=cap),
        out_shape=(jax.ShapeDtypeStruct((E, R, LANES), I32), jax.ShapeDtypeStruct((E, R, LANES), I32)),
        grid=(E,), in_specs=[spec], out_specs=(spec, spec), compiler_params=_cp(("parallel",)), name="select")(
            aff.reshape(E, R, LANES))


def _compact_kernel(off_ref, rank_ref, idx_ref, *, R):
    e = pl.program_id(0)
    idx_ref[...] = jnp.zeros_like(idx_ref)
    sub = lax.broadcasted_iota(I32, (8, LANES), 0)
    lane = lax.broadcasted_iota(I32, (8, LANES), 1)
    lhs = jnp.where(sub == 0, lane, jnp.where(sub == 1, 1, 0)).astype(BF)
    slot = lax.broadcasted_iota(I32, (2 * LANES, LANES), 0)

    def chunk(r, c):
        a = off_ref[e, r] // LANES
        local = rank_ref[pl.ds(r, 1), :] - a * LANES
        onehot = (slot == local).astype(BF)
        res = _dot_nt(lhs, onehot)
        val = (res[0:1] + res[1:2] * lax.convert_element_type(r * LANES, F32)).astype(I32)
        idx_ref[pl.ds(a, 1), :] += val[:, :LANES]
        idx_ref[pl.ds(a + 1, 1), :] += val[:, LANES:]
        return c

    lax.fori_loop(0, R, chunk, 0, unroll=8)


def _compact(rank, rowoff, cap):
    E, R, _ = rank.shape
    nrow = cap // LANES + 2
    return pl.pallas_call(
        functools.partial(_compact_kernel, R=R),
        out_shape=jax.ShapeDtypeStruct((E, nrow, LANES), I32),
        grid_spec=pltpu.PrefetchScalarGridSpec(
            num_scalar_prefetch=1, grid=(E,),
            in_specs=[pl.BlockSpec((None, R, LANES), lambda e, off: (e, 0, 0))],
            out_specs=pl.BlockSpec((None, nrow, LANES), lambda e, off: (e, 0, 0))),
        compiler_params=_cp(("arbitrary",)), name="compact")(rowoff, rank)


def _ffn_kernel(idx_ref, idxn_ref, x_hbm, wg_ref, wu_ref, wd_ref, o_ref, xbuf, xb, acc, sem, *, tm, nf, ntiles):
    f = pl.program_id(2)
    tile = pl.program_id(0) * pl.num_programs(1) + pl.program_id(1)
    slot = tile % 2
    part = tm // nf

    def row_copy(t, r, slot_):
        return pltpu.make_async_copy(x_hbm.at[pl.ds(t, 1)], xbuf.at[slot_, pl.ds(r, 1)], sem.at[slot_])

    def wait_tile(slot_):
        pltpu.make_async_copy(x_hbm.at[pl.ds(0, tm)], xbuf.at[slot_], sem.at[slot_]).wait()

    @pl.when((tile == 0) & (f == 0))
    def _():
        def issue(r, c):
            row_copy(idx_ref[0, 0, r], r, 0).start()
            return c

        lax.fori_loop(0, tm, issue, 0)

    @pl.when(f == 0)
    def _():
        wait_tile(slot)
        xb[...] = xbuf[slot].astype(BF)
        if nf > 1:
            acc[...] = jnp.zeros_like(acc)

    base = f * part
    for j in range(part):
        row_copy(idxn_ref[0, 0, base + j], base + j, 1 - slot).start()

    x = xb[...]
    g = _dot(x, wg_ref[...])
    u = _dot(x, wu_ref[...])
    hcur = (g * jax.nn.sigmoid(g)) * u
    down = _dot(hcur.astype(BF), wd_ref[...])
    if nf == 1:
        o_ref[...] = down.astype(o_ref.dtype)
    else:
        acc[...] += down

        @pl.when(f == nf - 1)
        def _():
            o_ref[...] = acc[...].astype(o_ref.dtype)

    @pl.when((tile == ntiles - 1) & (f == nf - 1))
    def _():
        wait_tile(1 - slot)


def _ffn(x, idx, wg, wu, wd, layer, cap, tm=FFN_ROW_TILE):
    T, D = x.shape
    E = wg.shape[1]
    FF = wg.shape[3]
    tm = min(tm, cap)
    nt = cap // tm
    fc, nf = FF, 1
    resident = pl.Buffered(1)
    idx3 = idx[:, :cap // LANES, :].reshape(E * nt, 1, tm)
    last = E * nt - 1
    return pl.pallas_call(
        functools.partial(_ffn_kernel, tm=tm, nf=nf, ntiles=E * nt),
        out_shape=jax.ShapeDtypeStruct((E, cap, D), BF), grid=(E, nt, nf),
        in_specs=[pl.BlockSpec((1, 1, tm), lambda e, i, f: (e * nt + i, 0, 0), memory_space=pltpu.SMEM),
                  pl.BlockSpec((1, 1, tm), lambda e, i, f: (jnp.minimum(e * nt + i + 1, last), 0, 0),
                               memory_space=pltpu.SMEM),
                  pl.BlockSpec(memory_space=pl.ANY),
                  pl.BlockSpec((None, None, D, fc), lambda e, i, f: (layer, e, 0, f), pipeline_mode=resident),
                  pl.BlockSpec((None, None, D, fc), lambda e, i, f: (layer, e, 0, f), pipeline_mode=resident),
                  pl.BlockSpec((None, None, fc, D), lambda e, i, f: (layer, e, f, 0), pipeline_mode=resident)],
        out_specs=pl.BlockSpec((None, tm, D), lambda e, i, f: (e, i, 0)),
        scratch_shapes=[pltpu.VMEM((2, tm, D), F32), pltpu.VMEM((tm, D), BF),
                        pltpu.VMEM((tm, D) if nf > 1 else (8, LANES), F32), pltpu.SemaphoreType.DMA((2,))],
        compiler_params=_cp(("arbitrary", "arbitrary", "arbitrary")), name="expert_ffn")(idx3, idx3, x, wg, wu, wd)


COMBINE_WIN = 64


def _combine_kernel(off_ref, rank_ref, aff_ref, x_ref, g_ref, b_ref, out_hbm, o_ref, wins, winx, ysc, sem, semx,
                    *, tt, E, cap, rpt, ntile, alpha):
    W = COMBINE_WIN
    ti = pl.program_id(0)
    slot = ti % 2

    def start0(e, ti_):
        r0 = off_ref[e, ti_ * rpt]
        return pl.multiple_of(jnp.minimum((r0 // 16) * 16, cap - W), 16)

    def win_copy(e, ti_, slot_):
        return pltpu.make_async_copy(out_hbm.at[e, pl.ds(start0(e, ti_), W)], wins.at[slot_, pl.ds(e * W, W)],
                                     sem.at[slot_])

    @pl.when(ti == 0)
    def _():
        for e in range(E):
            win_copy(e, 0, 0).start()

    @pl.when(ti + 1 < ntile)
    def _():
        for e in range(E):
            win_copy(e, ti + 1, 1 - slot).start()

    sub = lax.broadcasted_iota(I32, (W, tt), 0)

    def gated_onehot(r_row, g_row, first_row, min_rank):
        g_hi = g_row.astype(BF).astype(F32)
        match = (sub == r_row - first_row) & (r_row >= min_rank)
        return jnp.concatenate([jnp.where(match, g_hi, 0.0), jnp.where(match, g_row - g_hi, 0.0)], axis=1).astype(BF)

    def scatter_rows(p_t, rows):
        y2 = lax.dot_general(p_t, rows, (((0,), (0,)), ((), ())), preferred_element_type=F32)
        return y2[:tt] + y2[tt:]

    p_t = jnp.concatenate([gated_onehot(rank_ref[e:e + 1, :], aff_ref[e:e + 1, :], start0(e, ti), 0)
                           for e in range(E)], axis=0)
    for e in range(E):
        win_copy(e, ti, slot).wait()
    ysc[...] = alpha * x_ref[...] + scatter_rows(p_t, wins[slot])

    def extra(e, c):
        s0 = start0(e, ti)
        r_end = off_ref[e, (ti + 1) * rpt]

        @pl.when(r_end > s0 + W)
        def _():
            r_row = rank_ref[pl.ds(e, 1), :]
            g_row = aff_ref[pl.ds(e, 1), :]
            for w in range(1, tt // W + 1):
                lo_w = s0 + w * W

                @pl.when(r_end > lo_w)
                def _():
                    sw = pl.multiple_of(jnp.minimum(lo_w, cap - W), 16)
                    cp = pltpu.make_async_copy(out_hbm.at[e, pl.ds(sw, W)], winx, semx)
                    cp.start()
                    cp.wait()
                    ysc[...] += scatter_rows(gated_onehot(r_row, g_row, sw, lo_w), winx[...])

        return c

    any_extra = off_ref[0, (ti + 1) * rpt] > start0(0, ti) + W
    for e in range(1, E):
        any_extra = any_extra | (off_ref[e, (ti + 1) * rpt] > start0(e, ti) + W)

    @pl.when(any_extra)
    def _():
        lax.fori_loop(0, E, extra, 0)

    o_ref[...] = _ln_rows(ysc[...], g_ref[...], b_ref[...])


def _combine_ln(out, rank, aff, rowoff, x, g, b, cap, alpha):
    T, D = x.shape
    E = out.shape[0]
    W = COMBINE_WIN
    assert cap >= W
    tt = min(COMBINE_TOKEN_TILE, cap // 2)
    rpt = tt // LANES
    ntile = T // tt
    kern = functools.partial(_combine_kernel, tt=tt, E=E, cap=cap, rpt=rpt, ntile=ntile, alpha=alpha)
    return pl.pallas_call(
        kern, out_shape=jax.ShapeDtypeStruct((T, D), F32),
        grid_spec=pltpu.PrefetchScalarGridSpec(
            num_scalar_prefetch=1, grid=(ntile,),
            in_specs=[pl.BlockSpec((E, tt), lambda i, off: (0, i)),
                      pl.BlockSpec((E, tt), lambda i, off: (0, i)),
                      pl.BlockSpec((tt, D), lambda i, off: (i, 0)),
                      pl.BlockSpec((1, D), lambda i, off: (0, 0)),
                      pl.BlockSpec((1, D), lambda i, off: (0, 0)),
                      pl.BlockSpec(memory_space=pl.ANY)],
            out_specs=pl.BlockSpec((tt, D), lambda i, off: (i, 0)),
            scratch_shapes=[pltpu.VMEM((2, E * W, D), BF), pltpu.VMEM((W, D), BF), pltpu.VMEM((tt, D), F32),
                            pltpu.SemaphoreType.DMA((2,)), pltpu.SemaphoreType.DMA]),
        compiler_params=_cp(("arbitrary",)), name="combine_ln",
    )(rowoff, rank, aff, x, g.reshape(1, D), b.reshape(1, D), out)


def _expert_choice_ln(x, w_router_t, wg, wu, wd, layer, g, b, alpha):
    T, D = x.shape
    E = N_EXPERTS
    cap = max(1, EC_FACTOR * T // E)
    aff = _router(x, w_router_t)
    rank, rowoff = _select(aff, cap)
    rowoff = jnp.concatenate([rowoff[:, :, 0], jnp.full((E, 1), cap, I32)], axis=1)
    idx = _compact(rank, rowoff, cap)
    out = _ffn(x, idx, wg, wu, wd, layer, cap)
    return _combine_ln(out, rank.reshape(E, T), aff, rowoff, x, g, b, cap, alpha)


def _trunk(x, p, depth):
    B, L, D = x.shape
    T = B * L
    alpha = (2 * depth) ** 0.25
    x = x.reshape(T, D)
    filt_cache = {}
    for layer in range(depth):
        j = layer // 2
        if layer % 2 == 0:
            hproj = _xw(x, p["ab_w_in"][j], name="ab_in").reshape(B, L, -1)
            lambda_init = 0.8 - 0.6 * math.exp(-0.3 * layer)
            lf = p["diff_lambda"][j].astype(F32)
            lam_full = jnp.exp(jnp.sum(lf[0] * lf[1])) - jnp.exp(jnp.sum(lf[2] * lf[3])) + lambda_init
            t_attn = min(ATTN_TILE, L)
            if t_attn not in p["bias_tiles"]:
                p["bias_tiles"][t_attn] = _bias_tiles(p["rel_bias"], t_attn)
            a_out = _diff_attention(hproj, lam_full, p["diff_subln_g"][j], p["bias_tiles"][t_attn], lambda_init)
            uc = _short_conv(hproj, p["hy_conv_w"][j], p["hy_conv_b"][j])
            filt, filt_sum = _filter_mlp(L, p["hy_f_w1"][j], p["hy_f_b1"][j], p["hy_f_w2"][j], p["hy_f_b2"][j],
                                         p["hy_f_w3"][j], p["hy_f_freq"][j], p["hy_decay"][j])
            b_out = _hyena(uc, filt, filt_sum, p["hy_skip"][j])
            w_out = p["ab_w_out"][j]
            pairs = [(a_out.reshape(T, D_A), w_out[:D_A]), (b_out.reshape(T, D_B), w_out[D_A:])]
        else:
            pairs = [(_fourier_mix(x, B, L), p["c_w_out"][j])]
        x = _mm_res_ln(pairs, x, p["ln_g"][layer, 0], p["ln_b"][layer, 0], alpha)
        x = _expert_choice_ln(x, p["ec_router_t"][layer], p["ec_w_gate"], p["ec_w_up"], p["ec_w_down"], layer,
                              p["ln_g"][layer, 1], p["ln_b"][layer, 1], alpha)
    return x.reshape(B, L, D)


def kernel(x_prompt, x_sample, rel_bias, ab_w_in, ab_w_out, diff_lambda, diff_subln_g, hy_conv_w, hy_conv_b,
           hy_f_w1, hy_f_b1, hy_f_w2, hy_f_b2, hy_f_w3, hy_f_freq, hy_decay, hy_skip,
           c_w_out, ec_router, ec_w_gate, ec_w_up, ec_w_down, ln_g, ln_b):
    depth = ec_router.shape[0]
    p = dict(
        rel_bias=rel_bias, ab_w_in=ab_w_in.astype(BF), ab_w_out=ab_w_out.astype(BF), diff_lambda=diff_lambda,
        diff_subln_g=diff_subln_g, hy_conv_w=hy_conv_w, hy_conv_b=hy_conv_b, hy_f_w1=hy_f_w1, hy_f_b1=hy_f_b1,
        hy_f_w2=hy_f_w2, hy_f_b2=hy_f_b2, hy_f_w3=hy_f_w3, hy_f_freq=hy_f_freq, hy_decay=hy_decay, hy_skip=hy_skip,
        c_w_out=c_w_out.astype(BF), ec_router_t=jnp.swapaxes(ec_router, 1, 2).astype(BF),
        ec_w_gate=ec_w_gate.astype(BF), ec_w_up=ec_w_up.astype(BF), ec_w_down=ec_w_down.astype(BF),
        ln_g=ln_g.astype(F32), ln_b=ln_b.astype(F32), bias_tiles={})
    return _trunk(x_prompt, p, depth), _trunk(x_sample, p, depth)
```

```python
import functools
import math

import jax
import jax.numpy as jnp
from jax import lax
from jax.experimental import pallas as pl
from jax.experimental.pallas import tpu as pltpu

BF = jnp.bfloat16
F32 = jnp.float32
I32 = jnp.int32

A_HEADS = 4
A_DV = 128
A_DK = 64
D_A = 512
D_B = 512
B_ORDER = 2
REL_BUCKETS = 32
REL_MAX_DIST = 128
FILTER_BANDS = 16
N_EXPERTS = 16
EC_FACTOR = 2
LN_EPS = 1e-5
C_GROUPS = 4
LANES = 128
BF16_SUBLANES = 16
FFT_N2 = 128
VMEM_LIMIT = 52 * 1024 * 1024

TOKEN_TILE = 512
WIDE_TOKEN_TILE = 1024
FFN_ROW_TILE = 512
COMBINE_TOKEN_TILE = 256
FILTER_ROW_TILE = 512
HY_COL_TILE = 4096
FNET_COL_TILE = 8192
FNET_FREQ_PER_STEP = 8


def _cp(sem, vmem=VMEM_LIMIT):
    return pltpu.CompilerParams(dimension_semantics=sem, vmem_limit_bytes=vmem)


def _dot(a, b):
    return jnp.dot(a, b, preferred_element_type=F32)


def _dot_nt(a, b):
    return lax.dot_general(a, b, (((1,), (1,)), ((), ())), preferred_element_type=F32)


def _mm_kernel(a_ref, b_ref, o_ref, *scratch, nk, kaxis, scale):
    prod = _dot(a_ref[...].astype(BF), b_ref[...].astype(BF))
    if nk == 1:
        o_ref[...] = (prod * scale).astype(o_ref.dtype)
        return
    acc = scratch[0]
    k = pl.program_id(kaxis)

    @pl.when(k == 0)
    def _():
        acc[...] = jnp.zeros_like(acc)

    acc[...] += prod

    @pl.when(k == nk - 1)
    def _():
        o_ref[...] = (acc[...] * scale).astype(o_ref.dtype)


def _mm(a, b, out_shape, grid, a_spec, b_spec, o_spec, mn, *, nk=1, kaxis=None, scale=1.0, name):
    sem = tuple("arbitrary" if i == kaxis else "parallel" for i in range(len(grid)))
    return pl.pallas_call(
        functools.partial(_mm_kernel, nk=nk, kaxis=kaxis, scale=scale),
        out_shape=out_shape, grid=grid, in_specs=[a_spec, b_spec], out_specs=o_spec,
        scratch_shapes=[pltpu.VMEM(mn, F32)] if nk > 1 else [],
        compiler_params=_cp(sem), name=name)(a, b)


def _xw(x, w, tm=TOKEN_TILE, name="xw"):
    M, K = x.shape
    N = w.shape[1]
    tm = min(tm, M)
    return _mm(x, w, jax.ShapeDtypeStruct((M, N), F32), (M // tm,),
               pl.BlockSpec((tm, K), lambda i: (i, 0)), pl.BlockSpec((K, N), lambda i: (0, 0)),
               pl.BlockSpec((tm, N), lambda i: (i, 0)), (tm, N), name=name)


def _ln_rows(y, g, b):
    mu = jnp.mean(y, -1, keepdims=True)
    d = y - mu
    var = jnp.mean(d * d, -1, keepdims=True)
    return d * lax.rsqrt(var + LN_EPS) * g + b


def _mm_res_ln_kernel(*refs, npairs, alpha):
    x_ref, g_ref, b_ref = refs[2 * npairs:2 * npairs + 3]
    o_ref = refs[2 * npairs + 3]
    y = alpha * x_ref[...]
    for i in range(npairs):
        y = y + _dot(refs[2 * i][...].astype(BF), refs[2 * i + 1][...])
    o_ref[...] = _ln_rows(y, g_ref[...], b_ref[...])


def _mm_res_ln(pairs, x, g, b, alpha, tm=TOKEN_TILE, name="mm_res_ln"):
    M, D = x.shape
    tm = min(tm, M)
    specs, args = [], []
    for a, w in pairs:
        specs += [pl.BlockSpec((tm, a.shape[1]), lambda i: (i, 0)), pl.BlockSpec(w.shape, lambda i: (0, 0))]
        args += [a, w]
    specs += [pl.BlockSpec((tm, D), lambda i: (i, 0)), pl.BlockSpec((1, D), lambda i: (0, 0)),
              pl.BlockSpec((1, D), lambda i: (0, 0))]
    args += [x, g.reshape(1, D), b.reshape(1, D)]
    return pl.pallas_call(
        functools.partial(_mm_res_ln_kernel, npairs=len(pairs), alpha=alpha),
        out_shape=jax.ShapeDtypeStruct((M, D), F32), grid=(M // tm,), in_specs=specs,
        out_specs=pl.BlockSpec((tm, D), lambda i: (i, 0)), compiler_params=_cp(("parallel",)), name=name)(*args)


def _rel_bucket(rel):
    nb = REL_BUCKETS // 2
    max_exact = nb // 2
    n = jnp.abs(rel)
    large = max_exact + (jnp.log(jnp.maximum(n, 1).astype(F32) / max_exact)
                         / math.log(REL_MAX_DIST / max_exact) * (nb - max_exact)).astype(I32)
    large = jnp.minimum(large, nb - 1)
    return jnp.where(rel > 0, nb, 0) + jnp.where(n < max_exact, n, large)


LOG2E = 1.4426950408889634


def _bias_tiles(rel_table, T):
    assert T >= REL_MAX_DIST
    i = jnp.arange(T)
    rel = (jnp.arange(-2, 3)[:, None, None] * T + i[None, :, None]) - i[None, None, :]
    bucket = _rel_bucket(rel)[None]
    tab = LOG2E * rel_table.astype(F32).T[:, :, None, None, None]
    out = jnp.zeros((rel_table.shape[1],) + rel.shape, F32)
    for b in range(REL_BUCKETS):
        out = jnp.where(bucket == b, tab[:, b], out)
    return out


ATTN_MAX_DRIFT = 40.0
ATTN_CHUNK_GROUP = 3
ATTN_TILE = 512


def _attn_kernel(lam_ref, q_ref, k_ref, v_ref, bias_ref, g_ref, o_ref, kb, vt, m_s, a_s, *, T, nk, out_scale):
    qi = pl.program_id(2)
    dv = A_DV

    @pl.when(qi == 0)
    def _():
        def prep(c, carry):
            st = pl.multiple_of(c * T, T)
            kb[c] = k_ref[pl.ds(st, T), :].astype(BF)
            vt[c, :dv] = v_ref[pl.ds(st, T), :].T.astype(BF)
            vt[c, dv:] = jnp.ones((vt.shape[1] - dv, T), BF)
            return carry

        lax.fori_loop(0, nk, prep, 0)

    qt = (q_ref[...] * (A_DK ** -0.5 * LOG2E)).T
    row = lax.broadcasted_iota(I32, qt.shape, 0)
    qs = (jnp.where(row < A_DK, qt, 0.0).astype(BF), jnp.where(row >= A_DK, qt, 0.0).astype(BF))

    def qk(kj, mi):
        return _dot(kb[kj], qs[mi])

    def scores(kj, mi):
        return qk(kj, mi) + bias_ref[jnp.clip(kj - qi, -2, 2) + 2]

    def rescaling_chunk(kj, first):
        for mi in range(2):
            s = scores(kj, mi)
            m_new = jnp.max(s, axis=0, keepdims=True)
            if not first:
                m_new = jnp.maximum(m_s[mi], m_new)
            pv = _dot(vt[kj], jnp.exp2(s - m_new).astype(BF))
            if first:
                a_s[mi] = pv
            else:
                a_s[mi] = jnp.exp2(m_s[mi] - m_new) * a_s[mi] + pv
            m_s[mi] = m_new

    rescaling_chunk(0, True)

    def lazy_chunks(kjs, near):
        pv, drift = [None, None], None
        if not near:
            far_bias = jnp.where(kjs[-1] < qi, bias_ref[0, 0:1, :], bias_ref[4, 0:1, :])
        for kj in kjs:
            for mi in range(2):
                if near:
                    s, m_ref = scores(kj, mi), m_s[mi]
                else:
                    s, m_ref = qk(kj, mi), m_s[mi] - far_bias
                acc = _dot(vt[kj], jnp.exp2(s - m_ref).astype(BF))
                pv[mi] = acc if pv[mi] is None else pv[mi] + acc
                d = jnp.max(s, axis=0, keepdims=True) - m_ref
                drift = d if drift is None else jnp.maximum(drift, d)
        over = jnp.max(drift) > ATTN_MAX_DRIFT

        @pl.when(jnp.logical_not(over))
        def _():
            for mi in range(2):
                a_s[mi] += pv[mi]

        @pl.when(over)
        def _():
            for kj in kjs:
                rescaling_chunk(kj, False)

    def near_or_far(kjs):
        is_near = (kjs[0] - 1 <= qi) & (qi <= kjs[-1] + 1)

        @pl.when(is_near)
        def _():
            lazy_chunks(kjs, True)

        @pl.when(jnp.logical_not(is_near))
        def _():
            lazy_chunks(kjs, False)

    group = ATTN_CHUNK_GROUP

    def grouped(i, c):
        near_or_far([1 + group * i + u for u in range(group)])
        return c

    ngroups = (nk - 1) // group
    lax.fori_loop(0, ngroups, grouped, 0)
    if (nk - 1) % group:
        near_or_far(list(range(1 + group * ngroups, nk)))
    a0, a1 = a_s[0], a_s[1]
    o = a0[:dv] / a0[dv:dv + 1] - lam_ref[0] * (a1[:dv] / a1[dv:dv + 1])
    o = o * lax.rsqrt(jnp.mean(o * o, 0, keepdims=True) + LN_EPS) * g_ref[...] * out_scale
    o_ref[...] = o.T


def _diff_attention(hproj, lam_full, sub_g, bias, lambda_init):
    B, L, _ = hproj.shape
    T = bias.shape[-1]
    H = A_HEADS
    return pl.pallas_call(
        functools.partial(_attn_kernel, T=T, nk=L // T, out_scale=1.0 - lambda_init),
        out_shape=jax.ShapeDtypeStruct((B, L, D_A), F32), grid=(B, H, L // T),
        in_specs=[pl.BlockSpec(memory_space=pltpu.SMEM),
                  pl.BlockSpec((None, T, LANES), lambda b, h, i: (b, i, h)),
                  pl.BlockSpec((None, L, LANES), lambda b, h, i: (b, 0, H + h)),
                  pl.BlockSpec((None, L, LANES), lambda b, h, i: (b, 0, 2 * H + h)),
                  pl.BlockSpec((None, 5, T, T), lambda b, h, i: (h, 0, 0, 0)),
                  pl.BlockSpec((A_DV, 1), lambda b, h, i: (0, 0))],
        out_specs=pl.BlockSpec((None, T, LANES), lambda b, h, i: (b, i, h)),
        scratch_shapes=[pltpu.VMEM((L // T, T, LANES), BF), pltpu.VMEM((L // T, A_DV + BF16_SUBLANES, T), BF),
                        pltpu.VMEM((2, 1, T), F32), pltpu.VMEM((2, A_DV + BF16_SUBLANES, T), F32)],
        compiler_params=_cp(("parallel", "parallel", "arbitrary")), name="diff_attn",
    )(lam_full.reshape(1), hproj, hproj, hproj, bias, sub_g.astype(F32).reshape(A_DV, 1))


def _short_conv_kernel(u_ref, w_ref, b_ref, o_ref):
    u = u_ref[...]
    L = u.shape[0]
    row = lax.broadcasted_iota(I32, u.shape, 0)
    up = jnp.where(row == 0, 0.0, pltpu.roll(u, 1, 0))
    dn = jnp.where(row == L - 1, 0.0, pltpu.roll(u, L - 1, 0))
    w = w_ref[...]
    o_ref[...] = up * w[0:1] + u * w[1:2] + dn * w[2:3] + b_ref[...]


def _short_conv(hproj, conv_w, conv_b):
    B, L, _ = hproj.shape
    nblk = 3 * D_B // LANES
    off = 3 * D_A // LANES
    per = D_B // LANES
    return pl.pallas_call(
        _short_conv_kernel, out_shape=jax.ShapeDtypeStruct((3, B, L, D_B), F32), grid=(B, nblk),
        in_specs=[pl.BlockSpec((None, L, LANES), lambda b, j: (b, 0, off + j)),
                  pl.BlockSpec((3, LANES), lambda b, j: (0, j)),
                  pl.BlockSpec((1, LANES), lambda b, j: (0, j))],
        out_specs=pl.BlockSpec((None, None, L, LANES), lambda b, j: (j // per, b, 0, j % per)),
        compiler_params=_cp(("parallel", "parallel")), name="short_conv",
    )(hproj, conv_w.astype(F32), conv_b.astype(F32).reshape(1, -1))


def _filter_mlp_kernel(z_ref, w1_ref, b1_ref, w2_ref, b2_ref, w3_ref, fq_ref, dec_ref, h_ref, s_ref):
    i = pl.program_id(0)
    z = z_ref[...]
    fq = fq_ref[...]
    h = jnp.sin(fq[0:1] * (_dot(z.astype(BF), w1_ref[...]) + b1_ref[...]))
    h = jnp.sin(fq[1:2] * (_dot(h.astype(BF), w2_ref[...]) + b2_ref[...]))
    t = z[:, 0:1]
    h = _dot(h.astype(BF), w3_ref[...]) * jnp.exp(-t * jnp.abs(dec_ref[...]))

    @pl.when(i == 0)
    def _():
        s_ref[...] = jnp.zeros_like(s_ref)

    s_ref[...] += jnp.sum(jnp.abs(h), axis=0, keepdims=True)
    row = lax.broadcasted_iota(I32, h.shape, 0)
    col = lax.broadcasted_iota(I32, h.shape, 1)
    half = h.shape[1] // 2
    h_ref[...] = jnp.where((row + i * h.shape[0] == 0) & (col >= half), 0.0, h)


def _filter_mlp(L, w1, b1, w2, b2, w3, freq, decay, tl=FILTER_ROW_TILE):
    t = jnp.linspace(0.0, 1.0, L, dtype=F32)[:, None]
    wpos = 2.0 * math.pi * jnp.arange(L, dtype=F32)[:, None] / L
    fr = jnp.linspace(1e-4, FILTER_BANDS - 1, FILTER_BANDS, dtype=F32)[None, :]
    z = jnp.concatenate([t, jnp.cos(fr * wpos), -jnp.sin(fr * wpos)], -1)
    emb, width = w1.shape
    ch = w3.shape[1]
    z = jnp.pad(z, ((0, 0), (0, LANES - emb)))
    padw = LANES - width
    w1p = jnp.pad(w1, ((0, LANES - emb), (0, padw))).astype(BF)
    w2p = jnp.pad(w2, ((0, padw), (0, padw))).astype(BF)
    w3p = jnp.pad(w3, ((0, padw), (0, 0))).astype(BF)
    b1p = jnp.pad(b1.astype(F32), (0, padw)).reshape(1, LANES)
    b2p = jnp.pad(b2.astype(F32), (0, padw)).reshape(1, LANES)
    fqp = jnp.pad(freq.astype(F32), ((0, 0), (0, padw)))
    tl = min(tl, L)
    full = lambda shp: pl.BlockSpec(shp, lambda i: (0, 0))
    return pl.pallas_call(
        _filter_mlp_kernel,
        out_shape=(jax.ShapeDtypeStruct((L, ch), F32), jax.ShapeDtypeStruct((1, ch), F32)), grid=(L // tl,),
        in_specs=[pl.BlockSpec((tl, LANES), lambda i: (i, 0)), full((LANES, LANES)), full((1, LANES)),
                  full((LANES, LANES)), full((1, LANES)), full((LANES, ch)), full((2, LANES)), full((1, ch))],
        out_specs=(pl.BlockSpec((tl, ch), lambda i: (i, 0)), full((1, ch))),
        compiler_params=_cp(("arbitrary",)), name="filter_mlp",
    )(z, w1p, b1p, w2p, b2p, w3p, fqp, decay.astype(F32).reshape(1, ch))


def _cis(idx, n):
    ang = (2.0 * math.pi / n) * (idx % n).astype(F32)
    return jnp.cos(ang), -jnp.sin(ang)


def _blk(re, im):
    return jnp.concatenate([jnp.concatenate([re, -im], -1), jnp.concatenate([im, re], -1)], -2)


def _dft_tables(N):
    N2 = FFT_N2
    N1 = N // N2
    a = jnp.arange(N1)
    b = jnp.arange(N2)
    f1r, f1i = _cis(a[:, None] * a[None, :], N1)
    tr, ti = _cis(a[:, None] * b[None, :], N)
    f2r, f2i = _cis(b[:, None] * b[None, :], N2)
    gr = tr[:, None, :] * f2r[None] - ti[:, None, :] * f2i[None]
    gi = tr[:, None, :] * f2i[None] + ti[:, None, :] * f2r[None]
    return N1, f1r, f1i, gr, gi


def _mid_kernel(g_ref, gi_ref, kf_ref, a_ref, o_ref):
    n2, c = a_ref.shape[2], a_ref.shape[3]
    kr, ki = kf_ref[0], kf_ref[1]
    for p in range(a_ref.shape[0]):
        x = _dot(g_ref[...], a_ref[p].reshape(2 * n2, c).astype(BF))
        xr, xi = x[:n2], x[n2:]
        y = jnp.concatenate([xr * kr - xi * ki, xr * ki + xi * kr], axis=0).astype(BF)
        o_ref[p] = _dot(gi_ref[...], y).reshape(2, n2, c).astype(o_ref.dtype)


def _filt_mid_kernel(g_ref, a_ref, s_ref, o_ref):
    n2 = a_ref.shape[1]
    c = a_ref.shape[2] // 2
    x = _dot(g_ref[...], a_ref[...].reshape(2 * n2, 2 * c).astype(BF))
    s = s_ref[...]
    inv = 1.0 / (s[:, :c] + s[:, c:])
    o_ref[0] = (x[:n2, :c] + x[:n2, c:]) * inv
    o_ref[1] = (x[n2:, :c] - x[n2:, c:]) * inv


def _hy_s1_kernel(m_ref, z_ref, o_ref):
    c = z_ref.shape[2]
    for t in range(z_ref.shape[1]):
        o_ref[:, t * c:(t + 1) * c] = _dot(m_ref[...], z_ref[:, t, :].astype(BF)).astype(o_ref.dtype)


def _gate_inv_kernel(m_ref, d_ref, z_ref, gate_ref, sk_ref, o_ref, *, scale):
    conv = _dot(m_ref[...], d_ref[...].astype(BF)) * scale
    c = z_ref.shape[2]
    for t in range(z_ref.shape[1]):
        o_ref[:, t, :] = gate_ref[:, t, :] * (conv[:, t * c:(t + 1) * c] + z_ref[:, t, :] * sk_ref[...])


def _hyena(uc, filt, filt_sum, skip):
    _, B, L, C = uc.shape
    assert B % 2 == 0
    P = B // 2
    N = 2 * L
    N2 = FFT_N2
    N1, f1r, f1i, gr, gi = _dft_tables(N)
    h = N1 // 2
    cols = N2 * C
    m1 = jnp.concatenate([jnp.concatenate([f1r[:, :h], -f1i[:, :h]], 1),
                          jnp.concatenate([f1i[:, :h], f1r[:, :h]], 1)], 0).astype(BF)
    m1_real = jnp.concatenate([f1r[:, :h], f1i[:, :h]], 0).astype(BF)
    m1_inv = jnp.concatenate([jnp.concatenate([f1r[:h], f1i[:h]], 1),
                              jnp.concatenate([-f1i[:h], f1r[:h]], 1)], 0).astype(BF)
    g_fwd = _blk(gr, gi).astype(BF)
    g_inv = _blk(jnp.swapaxes(gr, 1, 2), -jnp.swapaxes(gi, 1, 2)).astype(BF)

    tc = min(cols, HY_COL_TILE)
    CF = filt.shape[1]
    fcols = N2 * CF
    tcf = min(fcols, FNET_COL_TILE)
    fa = _mm(m1_real, filt.reshape(h, fcols), jax.ShapeDtypeStruct((2 * N1, fcols), BF), (fcols // tcf,),
             pl.BlockSpec((2 * N1, h), lambda j: (0, 0)), pl.BlockSpec((h, tcf), lambda j: (0, j)),
             pl.BlockSpec((2 * N1, tcf), lambda j: (0, j)), None, name="filt_s1")
    kf = pl.pallas_call(
        _filt_mid_kernel, out_shape=jax.ShapeDtypeStruct((N1, 2, N2, CF // 2), F32), grid=(N1,),
        in_specs=[pl.BlockSpec((None, 2 * N2, 2 * N2), lambda k: (k, 0, 0)),
                  pl.BlockSpec((2, None, N2, CF), lambda k: (0, k, 0, 0)),
                  pl.BlockSpec((1, CF), lambda k: (0, 0))],
        out_specs=pl.BlockSpec((None, 2, N2, CF // 2), lambda k: (k, 0, 0, 0)),
        compiler_params=_cp(("parallel",)), name="filt_mid",
    )(g_fwd, fa.reshape(2, N1, N2, CF), filt_sum)

    nb = tc // C
    uc5 = uc.reshape(3, P, N1, N2, C)

    def natural(plane):
        if plane is None:
            return pl.BlockSpec((None, N1, nb, C), lambda p, j: (p, 0, j, 0))
        return pl.BlockSpec((None, None, N1, nb, C), lambda p, j: (plane, p, 0, j, 0))

    z, z_plane = uc5, 0
    for n in range(B_ORDER):
        a = pl.pallas_call(
            _hy_s1_kernel, out_shape=jax.ShapeDtypeStruct((P, 2 * N1, cols), BF), grid=(P, cols // tc),
            in_specs=[pl.BlockSpec((2 * N1, N1), lambda p, j: (0, 0)), natural(z_plane)],
            out_specs=pl.BlockSpec((None, 2 * N1, tc), lambda p, j: (p, 0, j)),
            compiler_params=_cp(("parallel", "parallel")), name="hy_s1")(m1, z)
        d = pl.pallas_call(
            _mid_kernel, out_shape=jax.ShapeDtypeStruct((P, 2, N1, N2, C), BF), grid=(N1,),
            in_specs=[pl.BlockSpec((None, 2 * N2, 2 * N2), lambda k: (k, 0, 0)),
                      pl.BlockSpec((None, 2 * N2, 2 * N2), lambda k: (k, 0, 0)),
                      pl.BlockSpec((None, 2, N2, C), lambda k: (k, 0, 0, n)),
                      pl.BlockSpec((P, 2, None, N2, C), lambda k: (0, 0, k, 0, 0))],
            out_specs=pl.BlockSpec((P, 2, None, N2, C), lambda k: (0, 0, k, 0, 0)),
            compiler_params=_cp(("parallel",)), name="hy_mid",
        )(g_fwd, g_inv, kf, a.reshape(P, 2, N1, N2, C))
        z = pl.pallas_call(
            functools.partial(_gate_inv_kernel, scale=1.0 / N),
            out_shape=jax.ShapeDtypeStruct((P, N1, N2, C), F32), grid=(P, cols // tc),
            in_specs=[pl.BlockSpec((N1, 2 * N1), lambda p, j: (0, 0)),
                      pl.BlockSpec((None, 2 * N1, tc), lambda p, j: (p, 0, j)),
                      natural(z_plane), natural(1 + n),
                      pl.BlockSpec((1, C), lambda p, j: (0, 0))],
            out_specs=natural(None),
            compiler_params=_cp(("parallel", "parallel")), name="hy_inv_gate",
        )(m1_inv, d.reshape(P, 2 * N1, cols), z, uc5, skip.astype(F32)[n:n + 1])
        z_plane = None
    return z.reshape(B, L, C)


def _fnet_tables(L, D):
    gw = D // C_GROUPS
    c = jnp.arange(gw)
    cr, ci = _cis(c[:, None] * c[None, :], gw)
    wch = jnp.concatenate([cr, ci], 1).astype(BF)
    N1, f1r, f1i, gr, gi = _dft_tables(L)
    m1 = _blk(f1r, f1i).astype(BF)
    g_re = jnp.concatenate([gr, -gi], -1).astype(BF)
    return N1, wch, m1, g_re


def _fnet_ch_kernel(x_ref, w_ref, o_ref):
    gw = x_ref.shape[1]
    res = _dot(x_ref[...].astype(BF), w_ref[...])
    o_ref[0] = res[:, :gw].astype(o_ref.dtype)
    o_ref[1] = res[:, gw:].astype(o_ref.dtype)


def _fnet_s1_kernel(m_ref, y_ref, o_ref):
    n1, tc = y_ref.shape[1], y_ref.shape[2]
    o_ref[...] = _dot(m_ref[...], y_ref[...].reshape(2 * n1, tc)).astype(o_ref.dtype)


def _fnet_s2_kernel(g_ref, a_ref, o_ref, *, scale):
    kb, n2 = g_ref.shape[0], g_ref.shape[1]
    d = a_ref.shape[3]
    for j in range(kb):
        res = _dot(g_ref[j], a_ref[:, j].reshape(2 * n2, d))
        o_ref[:, j, :] = res * scale


def _fourier_mix(x, B, L):
    T, D = x.shape
    N2 = FFT_N2
    N1, wch, m1, g_re = _fnet_tables(L, D)
    gw = D // C_GROUPS
    tm = min(WIDE_TOKEN_TILE, T)
    y = pl.pallas_call(
        _fnet_ch_kernel, out_shape=jax.ShapeDtypeStruct((2, T, D), BF), grid=(T // tm, C_GROUPS),
        in_specs=[pl.BlockSpec((tm, gw), lambda i, c: (i, c)), pl.BlockSpec((gw, 2 * gw), lambda i, c: (0, 0))],
        out_specs=pl.BlockSpec((2, tm, gw), lambda i, c: (0, i, c)),
        compiler_params=_cp(("parallel", "parallel")), name="fnet_ch")(x, wch)
    cols = N2 * D
    tc = min(cols, FNET_COL_TILE)
    a = pl.pallas_call(
        _fnet_s1_kernel, out_shape=jax.ShapeDtypeStruct((B, 2 * N1, cols), BF), grid=(B, cols // tc),
        in_specs=[pl.BlockSpec((2 * N1, 2 * N1), lambda b, j: (0, 0)),
                  pl.BlockSpec((2, None, N1, tc), lambda b, j: (0, b, 0, j))],
        out_specs=pl.BlockSpec((None, 2 * N1, tc), lambda b, j: (b, 0, j)),
        compiler_params=_cp(("parallel", "parallel")), name="fnet_s1")(m1, y.reshape(2, B, N1, cols))
    kb = min(FNET_FREQ_PER_STEP, N1)
    f = pl.pallas_call(
        functools.partial(_fnet_s2_kernel, scale=1.0 / math.sqrt(L * gw)),
        out_shape=jax.ShapeDtypeStruct((B, N2, N1, D), F32), grid=(B, N1 // kb),
        in_specs=[pl.BlockSpec((kb, N2, 2 * N2), lambda b, k: (k, 0, 0)),
                  pl.BlockSpec((None, 2, kb, N2, D), lambda b, k: (b, 0, k, 0, 0))],
        out_specs=pl.BlockSpec((None, N2, kb, D), lambda b, k: (b, 0, k, 0)),
        compiler_params=_cp(("parallel", "parallel")), name="fnet_s2")(g_re, a.reshape(B, 2, N1, N2, D))
    return f.reshape(T, D)


def _router_kernel(w_ref, x_ref, o_ref):
    logits = _dot_nt(w_ref[...], x_ref[...].astype(BF))
    m = jnp.max(logits, axis=0, keepdims=True)
    p = jnp.exp(logits - m)
    o_ref[...] = p / jnp.sum(p, axis=0, keepdims=True)


def _router(x, w_router_t, tm=WIDE_TOKEN_TILE):
    T, D = x.shape
    E = w_router_t.shape[0]
    tm = min(tm, T)
    return pl.pallas_call(
        _router_kernel, out_shape=jax.ShapeDtypeStruct((E, T), F32), grid=(T // tm,),
        in_specs=[pl.BlockSpec((E, D), lambda i: (0, 0)), pl.BlockSpec((tm, D), lambda i: (i, 0))],
        out_specs=pl.BlockSpec((E, tm), lambda i: (0, i)), compiler_params=_cp(("parallel",)), name="router")(
            w_router_t, x)


def _prefix_counts(mask_f, upper, lower):
    rowcs = _dot(mask_f.astype(BF), upper)
    tot = jnp.broadcast_to(rowcs[:, LANES - 1:LANES], mask_f.shape)
    offs = _dot(lower, tot.astype(BF))
    return offs + rowcs - mask_f, offs


def _select_kernel(a_ref, rank_ref, off_ref, *, cap):
    v = a_ref[...]
    R = v.shape[0]
    bits = pltpu.bitcast(v, I32)

    def enough(cand):
        return jnp.sum((bits >= cand).astype(F32)) >= cap

    def step(i, thr):
        b0 = jnp.int32(1) << (28 - 2 * i)
        b1 = b0 << 1
        return jnp.where(enough(thr | b1 | b0), thr | b1 | b0,
                         jnp.where(enough(thr | b1), thr | b1, jnp.where(enough(thr | b0), thr | b0, thr)))

    top = jnp.int32(1 << 30)
    thr = lax.fori_loop(0, 15, step, jnp.where(enough(top), top, jnp.int32(0)))
    gt = bits > thr
    eq = bits == thr
    need = cap - jnp.sum(gt.astype(F32))
    li = lax.broadcasted_iota(I32, (LANES, LANES), 0)
    lj = lax.broadcasted_iota(I32, (LANES, LANES), 1)
    upper = (li <= lj).astype(BF)
    ri = lax.broadcasted_iota(I32, (R, R), 0)
    rj = lax.broadcasted_iota(I32, (R, R), 1)
    lower = (rj < ri).astype(BF)
    eq_rank, _ = _prefix_counts(eq.astype(F32), upper, lower)
    sel = gt | (eq & (eq_rank < need))
    rank, offs = _prefix_counts(sel.astype(F32), upper, lower)
    rank_ref[...] = jnp.where(sel, rank.astype(I32), -1)
    off_ref[...] = offs.astype(I32)


def _select(aff, cap):
    E, T = aff.shape
    R = T // LANES
    spec = pl.BlockSpec((None, R, LANES), lambda e: (e, 0, 0))
    return pl.pallas_call(
        functools.partial(_select_kernel, cap=cap),
        out_shape=(jax.ShapeDtypeStruct((E, R, LANES), I32), jax.ShapeDtypeStruct((E, R, LANES), I32)),
        grid=(E,), in_specs=[spec], out_specs=(spec, spec), compiler_params=_cp(("parallel",)), name="select")(
            aff.reshape(E, R, LANES))


def _compact_kernel(off_ref, rank_ref, idx_ref, *, R):
    e = pl.program_id(0)
    idx_ref[...] = jnp.zeros_like(idx_ref)
    sub = lax.broadcasted_iota(I32, (8, LANES), 0)
    lane = lax.broadcasted_iota(I32, (8, LANES), 1)
    lhs = jnp.where(sub == 0, lane, jnp.where(sub == 1, 1, 0)).astype(BF)
    slot = lax.broadcasted_iota(I32, (2 * LANES, LANES), 0)

    def chunk(r, c):
        a = off_ref[e, r] // LANES
        local = rank_ref[pl.ds(r, 1), :] - a * LANES
        onehot = (slot == local).astype(BF)
        res = _dot_nt(lhs, onehot)
        val = (res[0:1] + res[1:2] * lax.convert_element_type(r * LANES, F32)).astype(I32)
        idx_ref[pl.ds(a, 1), :] += val[:, :LANES]
        idx_ref[pl.ds(a + 1, 1), :] += val[:, LANES:]
        return c

    lax.fori_loop(0, R, chunk, 0, unroll=16)


def _compact(rank, rowoff, cap):
    E, R, _ = rank.shape
    nrow = cap // LANES + 2
    return pl.pallas_call(
        functools.partial(_compact_kernel, R=R),
        out_shape=jax.ShapeDtypeStruct((E, nrow, LANES), I32),
        grid_spec=pltpu.PrefetchScalarGridSpec(
            num_scalar_prefetch=1, grid=(E,),
            in_specs=[pl.BlockSpec((None, R, LANES), lambda e, off: (e, 0, 0))],
            out_specs=pl.BlockSpec((None, nrow, LANES), lambda e, off: (e, 0, 0))),
        compiler_params=_cp(("arbitrary",)), name="compact")(rowoff, rank)


def _ffn_kernel(idx_ref, idxn_ref, x_hbm, wg_ref, wu_ref, wd_ref, o_ref, xbuf, xb, acc, sem, *, tm, nf, ntiles):
    f = pl.program_id(2)
    tile = pl.program_id(0) * pl.num_programs(1) + pl.program_id(1)
    slot = tile % 2
    part = tm // nf

    def row_copy(t, r, slot_):
        return pltpu.make_async_copy(x_hbm.at[pl.ds(t, 1)], xbuf.at[slot_, pl.ds(r, 1)], sem.at[slot_])

    def wait_tile(slot_):
        pltpu.make_async_copy(x_hbm.at[pl.ds(0, tm)], xbuf.at[slot_], sem.at[slot_]).wait()

    @pl.when((tile == 0) & (f == 0))
    def _():
        def issue(r, c):
            row_copy(idx_ref[0, 0, r], r, 0).start()
            return c

        lax.fori_loop(0, tm, issue, 0)

    @pl.when(f == 0)
    def _():
        wait_tile(slot)
        xb[...] = xbuf[slot].astype(BF)
        if nf > 1:
            acc[...] = jnp.zeros_like(acc)

    base = f * part
    for j in range(part):
        row_copy(idxn_ref[0, 0, base + j], base + j, 1 - slot).start()

    x = xb[...]
    g = _dot(x, wg_ref[...])
    u = _dot(x, wu_ref[...])
    hcur = (g * jax.nn.sigmoid(g)) * u
    down = _dot(hcur.astype(BF), wd_ref[...])
    if nf == 1:
        o_ref[...] = down.astype(o_ref.dtype)
    else:
        acc[...] += down

        @pl.when(f == nf - 1)
        def _():
            o_ref[...] = acc[...].astype(o_ref.dtype)

    @pl.when((tile == ntiles - 1) & (f == nf - 1))
    def _():
        wait_tile(1 - slot)


def _ffn(x, idx, wg, wu, wd, layer, cap, tm=FFN_ROW_TILE):
    T, D = x.shape
    E = wg.shape[1]
    FF = wg.shape[3]
    tm = min(tm, cap)
    nt = cap // tm
    fc, nf = FF, 1
    resident = pl.Buffered(1)
    idx3 = idx[:, :cap // LANES, :].reshape(E * nt, 1, tm)
    last = E * nt - 1
    return pl.pallas_call(
        functools.partial(_ffn_kernel, tm=tm, nf=nf, ntiles=E * nt),
        out_shape=jax.ShapeDtypeStruct((E, cap, D), BF), grid=(E, nt, nf),
        in_specs=[pl.BlockSpec((1, 1, tm), lambda e, i, f: (e * nt + i, 0, 0), memory_space=pltpu.SMEM),
                  pl.BlockSpec((1, 1, tm), lambda e, i, f: (jnp.minimum(e * nt + i + 1, last), 0, 0),
                               memory_space=pltpu.SMEM),
                  pl.BlockSpec(memory_space=pl.ANY),
                  pl.BlockSpec((None, None, D, fc), lambda e, i, f: (layer, e, 0, f), pipeline_mode=resident),
                  pl.BlockSpec((None, None, D, fc), lambda e, i, f: (layer, e, 0, f), pipeline_mode=resident),
                  pl.BlockSpec((None, None, fc, D), lambda e, i, f: (layer, e, f, 0), pipeline_mode=resident)],
        out_specs=pl.BlockSpec((None, tm, D), lambda e, i, f: (e, i, 0)),
        scratch_shapes=[pltpu.VMEM((2, tm, D), F32), pltpu.VMEM((tm, D), BF),
                        pltpu.VMEM((tm, D) if nf > 1 else (8, LANES), F32), pltpu.SemaphoreType.DMA((2,))],
        compiler_params=_cp(("arbitrary", "arbitrary", "arbitrary")), name="expert_ffn")(idx3, idx3, x, wg, wu, wd)


COMBINE_WIN = 64


def _combine_kernel(off_ref, rank_ref, aff_ref, x_ref, g_ref, b_ref, out_hbm, o_ref, wins, winx, ysc, sem, semx,
                    *, tt, E, cap, rpt, ntile, alpha):
    W = COMBINE_WIN
    ti = pl.program_id(0)
    slot = ti % 2

    def start0(e, ti_):
        r0 = off_ref[e, ti_ * rpt]
        return pl.multiple_of(jnp.minimum((r0 // 16) * 16, cap - W), 16)

    def win_copy(e, ti_, slot_):
        return pltpu.make_async_copy(out_hbm.at[e, pl.ds(start0(e, ti_), W)], wins.at[slot_, pl.ds(e * W, W)],
                                     sem.at[slot_])

    @pl.when(ti == 0)
    def _():
        for e in range(E):
            win_copy(e, 0, 0).start()

    @pl.when(ti + 1 < ntile)
    def _():
        for e in range(E):
            win_copy(e, ti + 1, 1 - slot).start()

    sub = lax.broadcasted_iota(I32, (W, tt), 0)

    def gated_onehot(r_row, g_row, first_row, min_rank):
        g_hi = g_row.astype(BF).astype(F32)
        match = (sub == r_row - first_row) & (r_row >= min_rank)
        return jnp.concatenate([jnp.where(match, g_hi, 0.0), jnp.where(match, g_row - g_hi, 0.0)], axis=1).astype(BF)

    def scatter_rows(p_t, rows):
        y2 = lax.dot_general(p_t, rows, (((0,), (0,)), ((), ())), preferred_element_type=F32)
        return y2[:tt] + y2[tt:]

    p_t = jnp.concatenate([gated_onehot(rank_ref[e:e + 1, :], aff_ref[e:e + 1, :], start0(e, ti), 0)
                           for e in range(E)], axis=0)
    for e in range(E):
        win_copy(e, ti, slot).wait()
    ysc[...] = alpha * x_ref[...] + scatter_rows(p_t, wins[slot])

    def extra(e, c):
        s0 = start0(e, ti)
        r_end = off_ref[e, (ti + 1) * rpt]

        @pl.when(r_end > s0 + W)
        def _():
            r_row = rank_ref[pl.ds(e, 1), :]
            g_row = aff_ref[pl.ds(e, 1), :]
            for w in range(1, tt // W + 1):
                lo_w = s0 + w * W

                @pl.when(r_end > lo_w)
                def _():
                    sw = pl.multiple_of(jnp.minimum(lo_w, cap - W), 16)
                    cp = pltpu.make_async_copy(out_hbm.at[e, pl.ds(sw, W)], winx, semx)
                    cp.start()
                    cp.wait()
                    ysc[...] += scatter_rows(gated_onehot(r_row, g_row, sw, lo_w), winx[...])

        return c

    any_extra = off_ref[0, (ti + 1) * rpt] > start0(0, ti) + W
    for e in range(1, E):
        any_extra = any_extra | (off_ref[e, (ti + 1) * rpt] > start0(e, ti) + W)

    @pl.when(any_extra)
    def _():
        lax.fori_loop(0, E, extra, 0)

    o_ref[...] = _ln_rows(ysc[...], g_ref[...], b_ref[...])


def _combine_ln(out, rank, aff, rowoff, x, g, b, cap, alpha):
    T, D = x.shape
    E = out.shape[0]
    W = COMBINE_WIN
    assert cap >= W
    tt = min(COMBINE_TOKEN_TILE, cap // 2)
    rpt = tt // LANES
    ntile = T // tt
    kern = functools.partial(_combine_kernel, tt=tt, E=E, cap=cap, rpt=rpt, ntile=ntile, alpha=alpha)
    return pl.pallas_call(
        kern, out_shape=jax.ShapeDtypeStruct((T, D), F32),
        grid_spec=pltpu.PrefetchScalarGridSpec(
            num_scalar_prefetch=1, grid=(ntile,),
            in_specs=[pl.BlockSpec((E, tt), lambda i, off: (0, i)),
                      pl.BlockSpec((E, tt), lambda i, off: (0, i)),
                      pl.BlockSpec((tt, D), lambda i, off: (i, 0)),
                      pl.BlockSpec((1, D), lambda i, off: (0, 0)),
                      pl.BlockSpec((1, D), lambda i, off: (0, 0)),
                      pl.BlockSpec(memory_space=pl.ANY)],
            out_specs=pl.BlockSpec((tt, D), lambda i, off: (i, 0)),
            scratch_shapes=[pltpu.VMEM((2, E * W, D), BF), pltpu.VMEM((W, D), BF), pltpu.VMEM((tt, D), F32),
                            pltpu.SemaphoreType.DMA((2,)), pltpu.SemaphoreType.DMA]),
        compiler_params=_cp(("arbitrary",)), name="combine_ln",
    )(rowoff, rank, aff, x, g.reshape(1, D), b.reshape(1, D), out)


def _expert_choice_ln(x, w_router_t, wg, wu, wd, layer, g, b, alpha):
    T, D = x.shape
    E = N_EXPERTS
    cap = max(1, EC_FACTOR * T // E)
    aff = _router(x, w_router_t)
    rank, rowoff = _select(aff, cap)
    rowoff = jnp.concatenate([rowoff[:, :, 0], jnp.full((E, 1), cap, I32)], axis=1)
    idx = _compact(rank, rowoff, cap)
    out = _ffn(x, idx, wg, wu, wd, layer, cap)
    return _combine_ln(out, rank.reshape(E, T), aff, rowoff, x, g, b, cap, alpha)


def _trunk(x, p, depth):
    B, L, D = x.shape
    T = B * L
    alpha = (2 * depth) ** 0.25
    x = x.reshape(T, D)
    for layer in range(depth):
        j = layer // 2
        if layer % 2 == 0:
            hproj = _xw(x, p["ab_w_in"][j], name="ab_in").reshape(B, L, -1)
            lambda_init = 0.8 - 0.6 * math.exp(-0.3 * layer)
            lf = p["diff_lambda"][j].astype(F32)
            lam_full = jnp.exp(jnp.sum(lf[0] * lf[1])) - jnp.exp(jnp.sum(lf[2] * lf[3])) + lambda_init
            t_attn = min(ATTN_TILE, L)
            if t_attn not in p["bias_tiles"]:
                p["bias_tiles"][t_attn] = _bias_tiles(p["rel_bias"], t_attn)
            a_out = _diff_attention(hproj, lam_full, p["diff_subln_g"][j], p["bias_tiles"][t_attn], lambda_init)
            uc = _short_conv(hproj, p["hy_conv_w"][j], p["hy_conv_b"][j])
            filt, filt_sum = _filter_mlp(L, p["hy_f_w1"][j], p["hy_f_b1"][j], p["hy_f_w2"][j], p["hy_f_b2"][j],
                                         p["hy_f_w3"][j], p["hy_f_freq"][j], p["hy_decay"][j])
            b_out = _hyena(uc, filt, filt_sum, p["hy_skip"][j])
            w_out = p["ab_w_out"][j]
            pairs = [(a_out.reshape(T, D_A), w_out[:D_A]), (b_out.reshape(T, D_B), w_out[D_A:])]
        else:
            pairs = [(_fourier_mix(x, B, L), p["c_w_out"][j])]
        x = _mm_res_ln(pairs, x, p["ln_g"][layer, 0], p["ln_b"][layer, 0], alpha)
        x = _expert_choice_ln(x, p["ec_router_t"][layer], p["ec_w_gate"], p["ec_w_up"], p["ec_w_down"], layer,
                              p["ln_g"][layer, 1], p["ln_b"][layer, 1], alpha)
    return x.reshape(B, L, D)


def kernel(x_prompt, x_sample, rel_bias, ab_w_in, ab_w_out, diff_lambda, diff_subln_g, hy_conv_w, hy_conv_b,
           hy_f_w1, hy_f_b1, hy_f_w2, hy_f_b2, hy_f_w3, hy_f_freq, hy_decay, hy_skip,
           c_w_out, ec_router, ec_w_gate, ec_w_up, ec_w_down, ln_g, ln_b):
    depth = ec_router.shape[0]
    p = dict(
        rel_bias=rel_bias, ab_w_in=ab_w_in.astype(BF), ab_w_out=ab_w_out.astype(BF), diff_lambda=diff_lambda,
        diff_subln_g=diff_subln_g, hy_conv_w=hy_conv_w, hy_conv_b=hy_conv_b, hy_f_w1=hy_f_w1, hy_f_b1=hy_f_b1,
        hy_f_w2=hy_f_w2, hy_f_b2=hy_f_b2, hy_f_w3=hy_f_w3, hy_f_freq=hy_f_freq, hy_decay=hy_decay, hy_skip=hy_skip,
        c_w_out=c_w_out.astype(BF), ec_router_t=jnp.swapaxes(ec_router, 1, 2).astype(BF),
        ec_w_gate=ec_w_gate.astype(BF), ec_w_up=ec_w_up.astype(BF), ec_w_down=ec_w_down.astype(BF),
        ln_g=ln_g.astype(F32), ln_b=ln_b.astype(F32), bias_tiles={})
    return _trunk(x_prompt, p, depth), _trunk(x_sample, p, depth)
```
